```python
import math
import jax
import jax.numpy as jnp
from jax import lax
import numpy as np

D_MODEL = 4096
BATCH = 4
SEQ = 4096
DEPTH = 2

D_MIX = D_MODEL
EPS = 1e-6
ROPE_THETA = 10000.0

RET_HEADS = 4
RET_DV = D_MIX // 16
RET_DK = RET_DV // 2
RET_WIDTH = RET_HEADS * RET_DV
RET_CHUNK = 128

SSM_WIDTH = D_MIX // 4
SSM_GROUP = 16
SSM_GROUPS = SSM_WIDTH // SSM_GROUP
SSM_STATE = 64
SSM_DT_MIN = 1e-3
SSM_DT_MAX = 1e-1

SWA_HEAD_DIM = 64
SWA_Q_HEADS = (D_MIX // 4) // SWA_HEAD_DIM
SWA_KV_HEADS = max(1, SWA_Q_HEADS // 8)
SWA_WIDTH = SWA_Q_HEADS * SWA_HEAD_DIM
SWA_WINDOW = 128

POOL_WIDTH = D_MIX - RET_WIDTH - SSM_WIDTH - SWA_WIDTH
POOL_WINDOWS = (2, 4, 8, 16)
POOL_GROUP = POOL_WIDTH // len(POOL_WINDOWS)

IN_SPLITS = (RET_HEADS * RET_DK, RET_HEADS * RET_DK, RET_WIDTH, RET_WIDTH,
             SSM_WIDTH,
             SWA_WIDTH, SWA_KV_HEADS * SWA_HEAD_DIM, SWA_KV_HEADS * SWA_HEAD_DIM,
             POOL_WIDTH)
IN_COLS = sum(IN_SPLITS)
OUT_SPLITS = (RET_WIDTH, SSM_WIDTH, SWA_WIDTH, POOL_WIDTH)

PEER_N_KEYS = 128
PEER_EXPERTS = PEER_N_KEYS ** 2
PEER_HEADS = 8
PEER_TOPK = 16
PEER_QDIM = 256
PEER_TOKEN_BLOCK = 128

kernel_name = "hybrid_retnet_s5_swa_pool_peer_adaln"


def _split(a, sizes):
    return jnp.split(a, np.cumsum(sizes)[:-1].tolist(), axis=-1)


def rms_norm(x, g):
    xf = x.astype(jnp.float32)
    y = xf * lax.rsqrt(jnp.mean(xf * xf, axis=-1, keepdims=True) + EPS)
    return (y * g.astype(jnp.float32)).astype(x.dtype)


def rotary(x, positions, theta=ROPE_THETA):
    d = x.shape[-1]
    half = d // 2
    inv = theta ** (-jnp.arange(half, dtype=jnp.float32) * 2.0 / d)
    ang = positions.astype(jnp.float32)[..., None] * inv
    cos = jnp.cos(ang)[:, :, None, :]
    sin = jnp.sin(ang)[:, :, None, :]
    xf = x.astype(jnp.float32)
    x1, x2 = xf[..., :half], xf[..., half:]
    return jnp.concatenate([x1 * cos - x2 * sin, x2 * cos + x1 * sin], axis=-1).astype(x.dtype)


def retention(q, k, v, g, positions):
    B, S, H, dk = q.shape
    dv = v.shape[-1]
    L = RET_CHUNK
    n = S // L
    q = rotary(q, positions)
    k = rotary(k, positions) * (dk ** -0.5)
    log_g = jnp.log(1.0 - 2.0 ** (-5.0 - jnp.arange(H, dtype=jnp.float32)))
    idx = jnp.arange(L, dtype=jnp.float32)
    diff = idx[:, None] - idx[None, :]
    decay = jnp.where(diff >= 0, jnp.exp(jnp.maximum(diff, 0.0)[None] * log_g[:, None, None]), 0.0)
    w_k = jnp.exp((L - 1 - idx)[None, :] * log_g[:, None])
    w_q = jnp.exp((idx + 1)[None, :] * log_g[:, None])
    gamma_L = jnp.exp(L * log_g)[None, :, None, None]

    qc = q.reshape(B, n, L, H, dk)
    kc = k.reshape(B, n, L, H, dk)
    vc = v.reshape(B, n, L, H, dv)
    scores = jnp.einsum('bnlhd,bnmhd->bnhlm', qc, kc) * decay[None, None]
    o_in = jnp.einsum('bnhlm,bnmhe->bnlhe', scores, vc)
    kv = jnp.einsum('bnmhd,hm,bnmhe->bnhde', kc, w_k, vc)

    def step(R, kv_n):
        return gamma_L * R + kv_n, R

    _, r_prev = lax.scan(step, jnp.zeros((B, H, dk, dv), kv.dtype), jnp.moveaxis(kv, 1, 0))
    r_prev = jnp.moveaxis(r_prev, 0, 1)
    o_cross = jnp.einsum('bnlhd,hl,bnhde->bnlhe', qc, w_q, r_prev)

    o = (o_in + o_cross).reshape(B, S, H, dv).astype(jnp.float32)
    mu = jnp.mean(o, axis=-1, keepdims=True)
    var = jnp.mean(jnp.square(o - mu), axis=-1, keepdims=True)
    o = (o - mu) * lax.rsqrt(var + EPS)
    return (jax.nn.silu(g.astype(jnp.float32)) * o).reshape(B, S, H * dv).astype(g.dtype)


def s5_ssm(u, a_re, a_im, log_dt, b_re, b_im, c_re, c_im, d, w_glu, b_glu):
    B, S, W = u.shape
    f32 = jnp.float32
    uf = u.astype(f32)
    ug = uf.reshape(B, S, SSM_GROUPS, SSM_GROUP)
    ar = a_re.astype(f32)
    ai = a_im.astype(f32)
    dt = jnp.exp(log_dt.astype(f32))[:, None]
    mag = jnp.exp(ar * dt)
    ab_re = mag * jnp.cos(ai * dt)
    ab_im = mag * jnp.sin(ai * dt)
    den = ar * ar + ai * ai
    nr = ab_re - 1.0
    f_re = (nr * ar + ab_im * ai) / den
    f_im = (ab_im * ar - nr * ai) / den
    br = b_re.astype(f32)
    bi = b_im.astype(f32)
    bb_re = f_re[..., None] * br - f_im[..., None] * bi
    bb_im = f_re[..., None] * bi + f_im[..., None] * br
    x_re = jnp.einsum('bsgc,gpc->sbgp', ug, bb_re)
    x_im = jnp.einsum('bsgc,gpc->sbgp', ug, bb_im)
    a_seq_re = jnp.broadcast_to(ab_re[None, None], (S, 1) + ab_re.shape)
    a_seq_im = jnp.broadcast_to(ab_im[None, None], (S, 1) + ab_im.shape)

    def combine(e_i, e_j):
        ar_i, ai_i, br_i, bi_i = e_i
        ar_j, ai_j, br_j, bi_j = e_j
        return (ar_j * ar_i - ai_j * ai_i,
                ar_j * ai_i + ai_j * ar_i,
                ar_j * br_i - ai_j * bi_i + br_j,
                ar_j * bi_i + ai_j * br_i + bi_j)

    _, _, h_re, h_im = lax.associative_scan(combine, (a_seq_re, a_seq_im, x_re, x_im), axis=0)
    y = (jnp.einsum('sbgp,gcp->bsgc', h_re, c_re.astype(f32))
         - jnp.einsum('sbgp,gcp->bsgc', h_im, c_im.astype(f32)))
    y = y.reshape(B, S, W) + d.astype(f32) * uf
    y = jax.nn.gelu(y, approximate=False)
    y = y * jax.nn.sigmoid(y @ w_glu.astype(f32) + b_glu.astype(f32))
    return y.astype(u.dtype)


def sliding_window_attention(q, k, v, q_norm, k_norm, sinks, positions):
    B, S, Hq, d = q.shape
    Hkv = k.shape[2]
    G = Hq // Hkv
    W = SWA_WINDOW
    nb = S // W
    q = rotary(rms_norm(q, q_norm), positions)
    k = rotary(rms_norm(k, k_norm), positions)
    qb = q.reshape(B, nb, W, Hkv, G, d)
    pad = ((0, 0), (W, 0), (0, 0), (0, 0))
    kb = jnp.pad(k, pad).reshape(B, nb + 1, W, Hkv, d)
    vb = jnp.pad(v, pad).reshape(B, nb + 1, W, Hkv, d)
    k_band = jnp.concatenate([kb[:, :-1], kb[:, 1:]], axis=2)
    v_band = jnp.concatenate([vb[:, :-1], vb[:, 1:]], axis=2)
    scores = jnp.einsum('bnqhgd,bnkhd->bnhgqk', qb, k_band).astype(jnp.float32) * (d ** -0.5)
    qi = jnp.arange(W)[:, None] + W
    ki = jnp.arange(2 * W)[None, :]
    rel = qi - ki
    kglob = jnp.arange(nb)[:, None, None] * W - W + ki[None]
    mask = (rel >= 0)[None] & (rel < W)[None] & (kglob >= 0)
    scores = jnp.where(mask[None, :, None, None], scores, -1e30)
    sink = jnp.broadcast_to(sinks.astype(jnp.float32).reshape(Hkv, G)[None, None, :, :, None, None],
                            scores.shape[:-1] + (1,))
    probs = jax.nn.softmax(jnp.concatenate([scores, sink], axis=-1), axis=-1)[..., :-1]
    out = jnp.einsum('bnhgqk,bnkhd->bnqhgd', probs.astype(v.dtype), v_band)
    return out.reshape(B, S, Hq * d)


def multiscale_pool(u, pool_w, pool_scale):
    B, S, _ = u.shape
    nG = len(POOL_WINDOWS)
    uf = u.astype(jnp.float32).reshape(B, S, nG, POOL_GROUP)
    cs = jnp.pad(jnp.cumsum(uf, axis=1), ((0, 0), (1, 0), (0, 0), (0, 0)))
    t = jnp.arange(S)[:, None]
    win = jnp.array(POOL_WINDOWS, dtype=jnp.int32)[None, :]
    lo = jnp.maximum(t + 1 - win, 0)
    grp = jnp.arange(nG)[None, :]
    window_sum = cs[:, t + 1, grp] - cs[:, lo, grp]
    count = (t + 1 - lo).astype(jnp.float32)[None, :, :, None]
    pooled = window_sum / count - uf
    y = jnp.einsum('bsgc,gcd->bsgd', pooled, pool_w.astype(jnp.float32)).reshape(B, S, POOL_WIDTH)
    return (y * pool_scale.astype(jnp.float32)).astype(u.dtype)


def peer_ffn(h, w_query, sub_keys, u_tab, v_tab):
    B, S, D = h.shape
    T = B * S
    K = PEER_TOPK
    hf = h.reshape(T, D)
    q = (hf @ w_query).reshape(T, PEER_HEADS, 2, PEER_QDIM // 2)
    scores = jnp.einsum('thpd,pnd->thpn', q, sub_keys).astype(jnp.float32)
    top_s, top_i = lax.top_k(scores, K)
    cand_s = top_s[:, :, 0, :, None] + top_s[:, :, 1, None, :]
    cand_e = top_i[:, :, 0, :, None] * PEER_N_KEYS + top_i[:, :, 1, None, :]
    best_s, best_pos = lax.top_k(cand_s.reshape(T, PEER_HEADS, K * K), K)
    experts = jnp.take_along_axis(cand_e.reshape(T, PEER_HEADS, K * K), best_pos, axis=-1)
    gates = jax.nn.softmax(best_s, axis=-1)

    nblk = T // PEER_TOKEN_BLOCK
    HK = PEER_HEADS * K

    def block(args):
        xb, eb, gb = args
        ub = u_tab[eb]
        a = jnp.einsum('ted,td->te', ub, xb).astype(jnp.float32)
        a = gb * jax.nn.gelu(a, approximate=False)
        vb = v_tab[eb]
        return jnp.einsum('te,ted->td', a.astype(vb.dtype), vb)

    out = lax.map(block, (hf.reshape(nblk, PEER_TOKEN_BLOCK, D),
                          experts.reshape(nblk, PEER_TOKEN_BLOCK, HK),
                          gates.reshape(nblk, PEER_TOKEN_BLOCK, HK)))
    return out.reshape(B, S, D).astype(h.dtype)


def hybrid_layer(x, c, positions, ada_w, ada_b, norm1_g, norm2_g, w_in, w_out, out_norm_g,
                 ssm_a_re, ssm_a_im, ssm_log_dt, ssm_b_re, ssm_b_im, ssm_c_re, ssm_c_im,
                 ssm_d, ssm_w_glu, ssm_b_glu, attn_q_norm, attn_k_norm, attn_sinks,
                 pool_w, pool_scale, peer_w_query, peer_sub_keys, peer_u, peer_v):
    B, S, _ = x.shape
    mod = jax.nn.silu(c) @ ada_w + ada_b
    sh1, sc1, g1, sh2, sc2, g2 = [m[:, None, :] for m in jnp.split(mod, 6, axis=-1)]

    h = rms_norm(x, norm1_g) * (1.0 + sc1) + sh1
    proj = h @ w_in
    q_r, k_r, v_r, g_r, u_s, q_a, k_a, v_a, u_p = _split(proj, IN_SPLITS)
    ret = retention(q_r.reshape(B, S, RET_HEADS, RET_DK), k_r.reshape(B, S, RET_HEADS, RET_DK),
                    v_r.reshape(B, S, RET_HEADS, RET_DV), g_r.reshape(B, S, RET_HEADS, RET_DV), positions)
    ssm = s5_ssm(u_s, ssm_a_re, ssm_a_im, ssm_log_dt, ssm_b_re, ssm_b_im, ssm_c_re, ssm_c_im,
                 ssm_d, ssm_w_glu, ssm_b_glu)
    swa = sliding_window_attention(q_a.reshape(B, S, SWA_Q_HEADS, SWA_HEAD_DIM),
                                   k_a.reshape(B, S, SWA_KV_HEADS, SWA_HEAD_DIM),
                                   v_a.reshape(B, S, SWA_KV_HEADS, SWA_HEAD_DIM),
                                   attn_q_norm, attn_k_norm, attn_sinks, positions)
    pool = multiscale_pool(u_p, pool_w, pool_scale)
    gains = _split(out_norm_g, OUT_SPLITS)
    mix = jnp.concatenate([rms_norm(o, gg) for o, gg in zip((ret, ssm, swa, pool), gains)], axis=-1)
    x = x + g1 * (mix @ w_out)

    h = rms_norm(x, norm2_g) * (1.0 + sc2) + sh2
    x = x + g2 * peer_ffn(h, peer_w_query, peer_sub_keys, peer_u, peer_v)
    return x


def setup_inputs(seed: int = 0) -> dict:
    key = jax.random.key(seed)
    ks = jax.random.split(key, 32)
    f32 = jnp.float32
    L = DEPTH

    def normal(k, shape, scale):
        return jax.random.normal(k, shape, f32) * scale

    x = normal(ks[0], (BATCH, SEQ, D_MODEL), 1.0)
    c = normal(ks[1], (BATCH, D_MODEL), 1.0)
    offset = jax.random.randint(ks[2], (BATCH, 1), 0, 1024, dtype=jnp.int32)
    positions = offset + jnp.arange(SEQ, dtype=jnp.int32)[None, :]

    ada_w = normal(ks[3], (L, D_MODEL, 6 * D_MODEL), 0.5 * D_MODEL ** -0.5)
    ada_b = normal(ks[4], (L, 6 * D_MODEL), 0.01)
    norm1_g = 1.0 + normal(ks[5], (L, D_MODEL), 0.01)
    norm2_g = 1.0 + normal(ks[6], (L, D_MODEL), 0.01)
    w_in = normal(ks[7], (L, D_MODEL, IN_COLS), D_MODEL ** -0.5)
    w_out = normal(ks[8], (L, D_MIX, D_MODEL), D_MIX ** -0.5)
    out_norm_g = 1.0 + normal(ks[9], (L, D_MIX), 0.01)

    ssm_a_re = -0.5 + normal(ks[10], (L, SSM_GROUPS, SSM_STATE), 0.01)
    ssm_a_im = (jnp.pi * jnp.arange(SSM_STATE, dtype=f32))[None, None, :] + normal(ks[11], (L, SSM_GROUPS, SSM_STATE), 0.01)
    ssm_log_dt = jax.random.uniform(ks[12], (L, SSM_GROUPS), f32, math.log(SSM_DT_MIN), math.log(SSM_DT_MAX))
    b_scale = (2.0 * SSM_GROUP) ** -0.5
    c_scale = (2.0 * SSM_STATE) ** -0.5
    ssm_b_re = normal(ks[13], (L, SSM_GROUPS, SSM_STATE, SSM_GROUP), b_scale)
    ssm_b_im = normal(ks[14], (L, SSM_GROUPS, SSM_STATE, SSM_GROUP), b_scale)
    ssm_c_re = normal(ks[15], (L, SSM_GROUPS, SSM_GROUP, SSM_STATE), c_scale)
    ssm_c_im = normal(ks[16], (L, SSM_GROUPS, SSM_GROUP, SSM_STATE), c_scale)
    ssm_d = normal(ks[17], (L, SSM_WIDTH), 1.0)
    ssm_w_glu = normal(ks[18], (L, SSM_WIDTH, SSM_WIDTH), SSM_WIDTH ** -0.5)
    ssm_b_glu = normal(ks[19], (L, SSM_WIDTH), 0.01)

    attn_q_norm = 1.0 + normal(ks[20], (L, SWA_HEAD_DIM), 0.01)
    attn_k_norm = 1.0 + normal(ks[21], (L, SWA_HEAD_DIM), 0.01)
    attn_sinks = normal(ks[22], (L, SWA_Q_HEADS), 1.0)

    pool_w = normal(ks[23], (L, len(POOL_WINDOWS), POOL_GROUP, POOL_GROUP), POOL_GROUP ** -0.5)
    pool_scale = 1.0 + normal(ks[24], (L, POOL_WIDTH), 0.1)

    peer_w_query = normal(ks[25], (L, D_MODEL, PEER_HEADS * PEER_QDIM), D_MODEL ** -0.5)
    peer_sub_keys = normal(ks[26], (L, 2, PEER_N_KEYS, PEER_QDIM // 2), (PEER_QDIM // 2) ** -0.5)
    peer_u = normal(ks[27], (L, PEER_EXPERTS, D_MODEL), D_MODEL ** -0.5)
    peer_v = normal(ks[28], (L, PEER_EXPERTS, D_MODEL), PEER_HEADS ** -0.5)

    return {"x": x, "c": c, "positions": positions,
            "ada_w": ada_w, "ada_b": ada_b, "norm1_g": norm1_g, "norm2_g": norm2_g,
            "w_in": w_in, "w_out": w_out, "out_norm_g": out_norm_g,
            "ssm_a_re": ssm_a_re, "ssm_a_im": ssm_a_im, "ssm_log_dt": ssm_log_dt,
            "ssm_b_re": ssm_b_re, "ssm_b_im": ssm_b_im, "ssm_c_re": ssm_c_re, "ssm_c_im": ssm_c_im,
            "ssm_d": ssm_d, "ssm_w_glu": ssm_w_glu, "ssm_b_glu": ssm_b_glu,
            "attn_q_norm": attn_q_norm, "attn_k_norm": attn_k_norm, "attn_sinks": attn_sinks,
            "pool_w": pool_w, "pool_scale": pool_scale,
            "peer_w_query": peer_w_query, "peer_sub_keys": peer_sub_keys,
            "peer_u": peer_u, "peer_v": peer_v}


def reference(x, c, positions, ada_w, ada_b, norm1_g, norm2_g, w_in, w_out, out_norm_g,
              ssm_a_re, ssm_a_im, ssm_log_dt, ssm_b_re, ssm_b_im, ssm_c_re, ssm_c_im,
              ssm_d, ssm_w_glu, ssm_b_glu, attn_q_norm, attn_k_norm, attn_sinks,
              pool_w, pool_scale, peer_w_query, peer_sub_keys, peer_u, peer_v):
    for i in range(DEPTH):
        x = hybrid_layer(x, c, positions, ada_w[i], ada_b[i], norm1_g[i], norm2_g[i], w_in[i], w_out[i],
                         out_norm_g[i], ssm_a_re[i], ssm_a_im[i], ssm_log_dt[i], ssm_b_re[i], ssm_b_im[i],
                         ssm_c_re[i], ssm_c_im[i], ssm_d[i], ssm_w_glu[i], ssm_b_glu[i],
                         attn_q_norm[i], attn_k_norm[i], attn_sinks[i], pool_w[i], pool_scale[i],
                         peer_w_query[i], peer_sub_keys[i], peer_u[i], peer_v[i])
    return x
```

```python
import functools
import math

import numpy as np
import jax
import jax.numpy as jnp
from jax import lax
from jax.experimental import pallas as pl
from jax.experimental.pallas import tpu as pltpu

F32 = jnp.float32
BF16 = jnp.bfloat16
HIGHEST = lax.Precision.HIGHEST

V7X_LANES = 128
V7X_SUBLANES = 8
V7X_VMEM_BYTES = 64 * 1024 * 1024
VMEM_LIMIT = 52 * 1024 * 1024

D_MODEL = 4096
EPS = 1e-6
ROPE_THETA = 10000.0

RET_HEADS = 4
RET_DV = 256
RET_DK = 128
RET_CHUNK = 128

SSM_WIDTH = 1024
SSM_GROUP = 16
SSM_GROUPS = 64
SSM_STATE = 64
SSM_COLS = 128
SSM_CH = SSM_COLS // SSM_GROUP * SSM_STATE
SSM_CHUNK = 256

SWA_HEAD_DIM = 64
SWA_Q_HEADS = 16
SWA_KV_HEADS = 2
SWA_WINDOW = 128

POOL_WINDOWS = (2, 4, 8, 16)
POOL_GROUP = 256
POOL_MAXW = 16

OFF_QR, OFF_KR, OFF_VR, OFF_GR, OFF_US, OFF_QA, OFF_KA, OFF_VA, OFF_UP = (
    0, 512, 1024, 2048, 3072, 4096, 5120, 5248, 5376)
IN_COLS = 6400

PEER_N_KEYS = 128
PEER_HEADS = 8
PEER_TOPK = 16
PEER_HK = PEER_HEADS * PEER_TOPK
PEER_TB = 8
PEER_SLOTS = 4
PEER_AHEAD = PEER_SLOTS - 1


def _cparams(sem):
    return pltpu.CompilerParams(dimension_semantics=sem, vmem_limit_bytes=VMEM_LIMIT)


def _gelu(x):
    return 0.5 * x * (1.0 + lax.erf(x * math.sqrt(0.5)))


def _sigmoid(x):
    return 1.0 / (1.0 + jnp.exp(-x))


def _ada_kernel(c_ref, w_ref, b_ref, o_ref):
    c = c_ref[...]
    a = c * _sigmoid(c)
    o_ref[0] = jnp.dot(a, w_ref[0], preferred_element_type=F32, precision=HIGHEST) + b_ref[0]


def _ada_mod(c8, ada_w, ada_b):
    L, D, N = ada_w.shape
    tn = 512
    return pl.pallas_call(
        _ada_kernel,
        grid=(L, N // tn),
        in_specs=[pl.BlockSpec((8, D), lambda l, j: (0, 0)),
                  pl.BlockSpec((1, D, tn), lambda l, j: (l, 0, j)),
                  pl.BlockSpec((1, 1, tn), lambda l, j: (l, 0, j))],
        out_specs=pl.BlockSpec((1, 8, tn), lambda l, j: (l, 0, j)),
        out_shape=jax.ShapeDtypeStruct((L, 8, N), F32),
        compiler_params=_cparams(("arbitrary", "arbitrary")),
        name="ada_mod",
    )(c8, ada_w, ada_b.reshape(L, 1, N))


def _norm_mod_kernel(x_ref, g_ref, sc_ref, sh_ref, *o_refs):
    x = x_ref[...]
    ms = jnp.mean(x * x, axis=-1, keepdims=True)
    y = x * lax.rsqrt(ms + EPS) * g_ref[...]
    h = y * (1.0 + sc_ref[0]) + sh_ref[0]
    for o in o_refs:
        o[...] = h.astype(o.dtype)


def _norm_mod(x2, gain, sc, sh, seq, out_dtypes):
    T, D = x2.shape
    B = sc.shape[0]
    tm = 256
    per_b = seq // tm
    row = pl.BlockSpec((tm, D), lambda i: (i, 0))
    bspec = pl.BlockSpec((1, 1, D), lambda i: (i // per_b, 0, 0))
    outs = pl.pallas_call(
        _norm_mod_kernel,
        grid=(T // tm,),
        in_specs=[row, pl.BlockSpec((1, D), lambda i: (0, 0)), bspec, bspec],
        out_specs=[row for _ in out_dtypes],
        out_shape=[jax.ShapeDtypeStruct((T, D), dt) for dt in out_dtypes],
        compiler_params=_cparams(("arbitrary",)),
        name="norm_mod",
    )(x2, gain.reshape(1, D), sc.reshape(B, 1, D), sh.reshape(B, 1, D))
    return outs


def _mm_kernel(a_ref, w_ref, o_ref):
    o_ref[...] = jnp.dot(a_ref[...], w_ref[...], preferred_element_type=F32)


def _matmul(a, w, tm=512, tn=512):
    M, K = a.shape
    N = w.shape[1]
    return pl.pallas_call(
        _mm_kernel,
        grid=(M // tm, N // tn),
        in_specs=[pl.BlockSpec((tm, K), lambda i, j: (i, 0)),
                  pl.BlockSpec((K, tn), lambda i, j: (0, j))],
        out_specs=pl.BlockSpec((tm, tn), lambda i, j: (i, j)),
        out_shape=jax.ShapeDtypeStruct((M, N), F32),
        compiler_params=_cparams(("arbitrary", "arbitrary")),
        name="matmul",
    )(a, w)


def _rope_tables(positions, d):
    half = d // 2
    inv = ROPE_THETA ** (-jnp.arange(half, dtype=F32) * 2.0 / d)
    ang = positions.astype(F32).reshape(-1, 1) * inv
    cos, sin = jnp.cos(ang), jnp.sin(ang)
    reps = V7X_LANES // d
    cosf = jnp.concatenate([cos, cos] * reps, axis=-1)
    sinf = jnp.concatenate([-sin, sin] * reps, axis=-1)
    return cosf, sinf


def _ret_consts():
    L, H = RET_CHUNK, RET_HEADS
    log_g = np.log(1.0 - 2.0 ** (-5.0 - np.arange(H, dtype=np.float64)))
    idx = np.arange(L, dtype=np.float64)
    diff = idx[:, None] - idx[None, :]
    decay = np.where(diff >= 0, np.exp(np.maximum(diff, 0.0)[None] * log_g[:, None, None]), 0.0)
    w_k = np.exp((L - 1 - idx)[None, :] * log_g[:, None])
    w_q = np.exp((idx + 1)[None, :] * log_g[:, None])
    gam = np.exp(L * log_g)
    wk_full = np.broadcast_to(w_k[:, :, None], (H, L, RET_DK))
    wq_full = np.broadcast_to(w_q[:, :, None], (H, L, RET_DK))
    gam_full = np.broadcast_to(gam[:, None, None], (H, 1, RET_DV))
    return (jnp.asarray(decay, F32), jnp.asarray(wq_full, F32), jnp.asarray(wk_full, F32),
            jnp.asarray(gam_full, F32))


def _ret_kernel(q_ref, k_ref, v_ref, g_ref, cos_ref, sin_ref, dec_ref, wq_ref, wk_ref, gam_ref,
                o_ref, r_ref):
    n = pl.program_id(2)

    @pl.when(n == 0)
    def _():
        r_ref[...] = jnp.zeros_like(r_ref)

    cos = cos_ref[...]
    sin = sin_ref[...]
    q = q_ref[...]
    k = k_ref[...]
    q = q * cos + pltpu.roll(q, RET_DK // 2, 1) * sin
    k = (k * cos + pltpu.roll(k, RET_DK // 2, 1) * sin) * (RET_DK ** -0.5)
    vb = v_ref[...].astype(BF16)
    s = lax.dot_general(q.astype(BF16), k.astype(BF16), (((1,), (1,)), ((), ())),
                        preferred_element_type=F32) * dec_ref[0]
    o = jnp.dot(s.astype(BF16), vb, preferred_element_type=F32)
    r = r_ref[...]
    o = o + jnp.dot((q * wq_ref[0]).astype(BF16), r.astype(BF16), preferred_element_type=F32)
    kv = lax.dot_general((k * wk_ref[0]).astype(BF16), vb, (((0,), (0,)), ((), ())),
                         preferred_element_type=F32)
    r_ref[...] = gam_ref[0] * r + kv
    mu = jnp.mean(o, axis=-1, keepdims=True)
    oc = o - mu
    var = jnp.mean(oc * oc, axis=-1, keepdims=True)
    g = g_ref[...]
    o_ref[...] = g * _sigmoid(g) * (oc * lax.rsqrt(var + EPS))


def _retention(proj, cosf, sinf, batch, seq):
    T = proj.shape[0]
    L = RET_CHUNK
    n = seq // L
    dec, wq, wk, gam = _ret_consts()

    def rows(b, h, c):
        return b * n + c

    return pl.pallas_call(
        _ret_kernel,
        grid=(batch, RET_HEADS, n),
        in_specs=[
            pl.BlockSpec((L, RET_DK), lambda b, h, c: (rows(b, h, c), OFF_QR // RET_DK + h)),
            pl.BlockSpec((L, RET_DK), lambda b, h, c: (rows(b, h, c), OFF_KR // RET_DK + h)),
            pl.BlockSpec((L, RET_DV), lambda b, h, c: (rows(b, h, c), OFF_VR // RET_DV + h)),
            pl.BlockSpec((L, RET_DV), lambda b, h, c: (rows(b, h, c), OFF_GR // RET_DV + h)),
            pl.BlockSpec((L, RET_DK), lambda b, h, c: (rows(b, h, c), 0)),
            pl.BlockSpec((L, RET_DK), lambda b, h, c: (rows(b, h, c), 0)),
            pl.BlockSpec((1, L, L), lambda b, h, c: (h, 0, 0)),
            pl.BlockSpec((1, L, RET_DK), lambda b, h, c: (h, 0, 0)),
            pl.BlockSpec((1, L, RET_DK), lambda b, h, c: (h, 0, 0)),
            pl.BlockSpec((1, 1, RET_DV), lambda b, h, c: (h, 0, 0)),
        ],
        out_specs=pl.BlockSpec((L, RET_DV), lambda b, h, c: (rows(b, h, c), h)),
        out_shape=jax.ShapeDtypeStruct((T, RET_HEADS * RET_DV), F32),
        scratch_shapes=[pltpu.VMEM((RET_DK, RET_DV), F32)],
        compiler_params=_cparams(("arbitrary", "arbitrary", "arbitrary")),
        name="retention",
    )(proj, proj, proj, proj, cosf, sinf, dec, wq, wk, gam)


def _ssm_params(a_re, a_im, log_dt, b_re, b_im, c_re, c_im):
    dt = jnp.exp(log_dt)[:, None]
    mag = jnp.exp(a_re * dt)
    ab_re = mag * jnp.cos(a_im * dt)
    ab_im = mag * jnp.sin(a_im * dt)
    den = a_re * a_re + a_im * a_im
    nr = ab_re - 1.0
    f_re = (nr * a_re + ab_im * a_im) / den
    f_im = (ab_im * a_re - nr * a_im) / den
    bb_re = f_re[..., None] * b_re - f_im[..., None] * b_im
    bb_im = f_re[..., None] * b_im + f_im[..., None] * b_re
    nblk = SSM_WIDTH // SSM_COLS
    gpb = SSM_COLS // SSM_GROUP
    eye = jnp.eye(gpb, dtype=F32)

    def blockdiag_in(bb):
        t = bb.reshape(nblk, gpb, SSM_STATE, SSM_GROUP)
        m = jnp.einsum('ngpc,gh->ngchp', t, eye)
        return m.reshape(nblk, SSM_COLS, SSM_CH)

    def blockdiag_out(cc):
        t = cc.reshape(nblk, gpb, SSM_GROUP, SSM_STATE)
        m = jnp.einsum('ngcp,gh->ngphc', t, eye)
        return m.reshape(nblk, SSM_CH, SSM_COLS)

    rounds = int(math.log2(SSM_CHUNK))
    pr, pi = [ab_re], [ab_im]
    for _ in range(rounds - 1):
        r, i = pr[-1], pi[-1]
        pr.append(r * r - i * i)
        pi.append(2.0 * r * i)
    apr = jnp.stack(pr, 0).reshape(rounds, nblk, SSM_CH).transpose(1, 0, 2)
    api = jnp.stack(pi, 0).reshape(rounds, nblk, SSM_CH).transpose(1, 0, 2)
    return (blockdiag_in(bb_re).astype(BF16), blockdiag_in(bb_im).astype(BF16),
            blockdiag_out(c_re).astype(BF16), blockdiag_out(c_im).astype(BF16), apr, api)


def _ssm_kernel(u_ref, bre_ref, bim_ref, cre_ref, cim_ref, apr_ref, api_ref, d_ref, o_ref,
                cr_ref, ci_ref):
    n = pl.program_id(2)

    @pl.when(n == 0)
    def _():
        cr_ref[...] = jnp.zeros_like(cr_ref)
        ci_ref[...] = jnp.zeros_like(ci_ref)

    u = u_ref[...]
    ub = u.astype(BF16)
    xr = jnp.dot(ub, bre_ref[0], preferred_element_type=F32)
    xi = jnp.dot(ub, bim_ref[0], preferred_element_type=F32)
    row = lax.broadcasted_iota(jnp.int32, xr.shape, 0)
    a_r = apr_ref[0, 0:1, :]
    a_i = api_ref[0, 0:1, :]
    c_r = cr_ref[...]
    c_i = ci_ref[...]
    first = row == 0
    xr = xr + jnp.where(first, a_r * c_r - a_i * c_i, 0.0)
    xi = xi + jnp.where(first, a_r * c_i + a_i * c_r, 0.0)
    for k in range(int(math.log2(SSM_CHUNK))):
        s = 1 << k
        p_r = apr_ref[0, k:k + 1, :]
        p_i = api_ref[0, k:k + 1, :]
        keep = row >= s
        sr = jnp.where(keep, pltpu.roll(xr, s, 0), 0.0)
        si = jnp.where(keep, pltpu.roll(xi, s, 0), 0.0)
        xr, xi = xr + p_r * sr - p_i * si, xi + p_r * si + p_i * sr
    cr_ref[...] = xr[SSM_CHUNK - 1:SSM_CHUNK, :]
    ci_ref[...] = xi[SSM_CHUNK - 1:SSM_CHUNK, :]
    y = (jnp.dot(xr.astype(BF16), cre_ref[0], preferred_element_type=F32)
         - jnp.dot(xi.astype(BF16), cim_ref[0], preferred_element_type=F32))
    o_ref[...] = _gelu(y + d_ref[...] * u)


def _ssm_scan(proj, params, d, batch, seq):
    T = proj.shape[0]
    bre, bim, cre, cim, apr, api = params
    nblk = SSM_WIDTH // SSM_COLS
    nt = seq // SSM_CHUNK
    rounds = apr.shape[1]
    wspec_in = pl.BlockSpec((1, SSM_COLS, SSM_CH), lambda b, c, n: (c, 0, 0))
    wspec_out = pl.BlockSpec((1, SSM_CH, SSM_COLS), lambda b, c, n: (c, 0, 0))
    pspec = pl.BlockSpec((1, rounds, SSM_CH), lambda b, c, n: (c, 0, 0))
    return pl.pallas_call(
        _ssm_kernel,
        grid=(batch, nblk, nt),
        in_specs=[pl.BlockSpec((SSM_CHUNK, SSM_COLS), lambda b, c, n: (b * nt + n, OFF_US // SSM_COLS + c)),
                  wspec_in, wspec_in, wspec_out, wspec_out, pspec, pspec,
                  pl.BlockSpec((1, SSM_COLS), lambda b, c, n: (0, c))],
        out_specs=pl.BlockSpec((SSM_CHUNK, SSM_COLS), lambda b, c, n: (b * nt + n, c)),
        out_shape=jax.ShapeDtypeStruct((T, SSM_WIDTH), F32),
        scratch_shapes=[pltpu.VMEM((1, SSM_CH), F32), pltpu.VMEM((1, SSM_CH), F32)],
        compiler_params=_cparams(("arbitrary", "arbitrary", "arbitrary")),
        name="ssm_scan",
    )(proj, bre, bim, cre, cim, apr, api, d.reshape(1, SSM_WIDTH))


def _glu_kernel(y_ref, w_ref, b_ref, o_ref):
    y = y_ref[...]
    z = jnp.dot(y.astype(BF16), w_ref[...], preferred_element_type=F32) + b_ref[...]
    o_ref[...] = y * _sigmoid(z)


def _glu(y, w, b):
    T, W = y.shape
    tm = 512
    return pl.pallas_call(
        _glu_kernel,
        grid=(T // tm,),
        in_specs=[pl.BlockSpec((tm, W), lambda i: (i, 0)),
                  pl.BlockSpec((W, W), lambda i: (0, 0)),
                  pl.BlockSpec((1, W), lambda i: (0, 0))],
        out_specs=pl.BlockSpec((tm, W), lambda i: (i, 0)),
        out_shape=jax.ShapeDtypeStruct((T, W), F32),
        compiler_params=_cparams(("arbitrary",)),
        name="ssm_glu",
    )(y, w, b.reshape(1, W))


def _swa_kernel(sink_ref, q_ref, k_ref, v_ref, cos_ref, sin_ref, qn_ref, kn_ref, o_ref, pk_ref, pv_ref):
    n = pl.program_id(1)
    W = SWA_WINDOW
    hd = SWA_HEAD_DIM

    @pl.when(n == 0)
    def _():
        pk_ref[...] = jnp.zeros_like(pk_ref)
        pv_ref[...] = jnp.zeros_like(pv_ref)

    cos = cos_ref[...]
    sin = sin_ref[...]
    lane = lax.broadcasted_iota(jnp.int32, (W, V7X_LANES), 1)
    lo = lane < hd
    first_half = (lane % hd) < (hd // 2)

    def head_norm(t, gain):
        tt = t * t
        s_all = jnp.sum(tt, axis=-1, keepdims=True)
        s_lo = jnp.sum(jnp.where(lo, tt, 0.0), axis=-1, keepdims=True)
        ms = jnp.where(lo, s_lo, s_all - s_lo) * (1.0 / hd)
        return t * lax.rsqrt(ms + EPS) * gain

    def rope(t):
        sw = jnp.where(first_half, pltpu.roll(t, V7X_LANES - hd // 2, 1), pltpu.roll(t, hd // 2, 1))
        return t * cos + sw * sin

    kc = rope(head_norm(k_ref[...], kn_ref[...]))
    vc = v_ref[...]
    keys = jnp.concatenate([pk_ref[...], kc], axis=0)
    vals = jnp.concatenate([pv_ref[...], vc], axis=0)
    lane2 = lax.broadcasted_iota(jnp.int32, (2 * W, V7X_LANES), 1)
    lo2 = lane2 < hd

    def dup(x, h):
        sw = pltpu.roll(x, hd, 1)
        return (jnp.where(lo2, x, sw) if h == 0 else jnp.where(lo2, sw, x)).astype(BF16)

    kk = [dup(keys, h) for h in range(SWA_KV_HEADS)]
    vv = [dup(vals, h) for h in range(SWA_KV_HEADS)]
    qi = lax.broadcasted_iota(jnp.int32, (W, 2 * W), 0)
    ci = lax.broadcasted_iota(jnp.int32, (W, 2 * W), 1)
    prev_floor = qi + jnp.where(n > 0, 0, 2 * W)
    valid = jnp.logical_or(jnp.logical_and(ci < W, ci > prev_floor),
                           jnp.logical_and(ci >= W, (ci - W) <= qi))
    tiles = SWA_Q_HEADS * hd // V7X_LANES
    per_kv = tiles // SWA_KV_HEADS
    for m in range(tiles):
        sl = slice(m * V7X_LANES, (m + 1) * V7X_LANES)
        qt = rope(head_norm(q_ref[:, sl], qn_ref[:, sl]))
        h = m // per_kv
        outs = []
        for half in range(2):
            sel = lo if half == 0 else jnp.logical_not(lo)
            qm = jnp.where(sel, qt, 0.0).astype(BF16)
            s = lax.dot_general(qm, kk[h], (((1,), (1,)), ((), ())),
                                preferred_element_type=F32) * (hd ** -0.5)
            s = jnp.where(valid, s, -1e30)
            sink = sink_ref[2 * m + half]
            mx = jnp.maximum(jnp.max(s, axis=-1, keepdims=True), sink)
            p = jnp.exp(s - mx)
            den = jnp.sum(p, axis=-1, keepdims=True) + jnp.exp(sink - mx)
            p = p / den
            outs.append(jnp.dot(p.astype(BF16), vv[h], preferred_element_type=F32))
        o_ref[:, sl] = jnp.where(lo, outs[0], outs[1])
    pk_ref[...] = kc
    pv_ref[...] = vc


def _swa(proj, cosf, sinf, q_norm, k_norm, sinks, batch, seq):
    T = proj.shape[0]
    W = SWA_WINDOW
    nb = seq // W
    qw = SWA_Q_HEADS * SWA_HEAD_DIM
    qn = jnp.tile(q_norm, SWA_Q_HEADS).reshape(1, qw)
    kn = jnp.tile(k_norm, SWA_KV_HEADS).reshape(1, V7X_LANES)
    return pl.pallas_call(
        _swa_kernel,
        grid=(batch, nb),
        in_specs=[pl.BlockSpec(memory_space=pltpu.SMEM),
                  pl.BlockSpec((W, qw), lambda b, n: (b * nb + n, OFF_QA // qw)),
                  pl.BlockSpec((W, V7X_LANES), lambda b, n: (b * nb + n, OFF_KA // V7X_LANES)),
                  pl.BlockSpec((W, V7X_LANES), lambda b, n: (b * nb + n, OFF_VA // V7X_LANES)),
                  pl.BlockSpec((W, V7X_LANES), lambda b, n: (b * nb + n, 0)),
                  pl.BlockSpec((W, V7X_LANES), lambda b, n: (b * nb + n, 0)),
                  pl.BlockSpec((1, qw), lambda b, n: (0, 0)),
                  pl.BlockSpec((1, V7X_LANES), lambda b, n: (0, 0))],
        out_specs=pl.BlockSpec((W, qw), lambda b, n: (b * nb + n, 0)),
        out_shape=jax.ShapeDtypeStruct((T, qw), F32),
        scratch_shapes=[pltpu.VMEM((W, V7X_LANES), F32), pltpu.VMEM((W, V7X_LANES), F32)],
        compiler_params=_cparams(("arbitrary", "arbitrary")),
        name="swa",
    )(sinks, proj, proj, proj, cosf, sinf, qn, kn)


POOL_CHUNK = 256


def _pool_kernel(u0_ref, u1_ref, u2_ref, u3_ref, w_ref, sc_ref, o_ref, ext_ref):
    n = pl.program_id(1)
    Lc = POOL_CHUNK

    @pl.when(n == 0)
    def _():
        ext_ref[:, 0:POOL_MAXW, :] = jnp.zeros((len(POOL_WINDOWS), POOL_MAXW, POOL_GROUP), F32)

    t = n * Lc + lax.broadcasted_iota(jnp.int32, (Lc, POOL_GROUP), 0)
    for g, (u_ref, w) in enumerate(zip((u0_ref, u1_ref, u2_ref, u3_ref), POOL_WINDOWS)):
        u = u_ref[...]
        ext_ref[g, POOL_MAXW:POOL_MAXW + Lc, :] = u
        acc = u
        for k in range(1, w):
            acc = acc + ext_ref[g, POOL_MAXW - k:POOL_MAXW - k + Lc, :]
        cnt = jnp.minimum(t + 1, w).astype(F32)
        pooled = acc / cnt - u
        y = jnp.dot(pooled.astype(BF16), w_ref[g], preferred_element_type=F32)
        sl = slice(g * POOL_GROUP, (g + 1) * POOL_GROUP)
        o_ref[:, sl] = y * sc_ref[:, sl]
        ext_ref[g, 0:POOL_MAXW, :] = u[Lc - POOL_MAXW:, :]


def _pool(proj, pool_w, pool_scale, batch, seq):
    T = proj.shape[0]
    Lc = POOL_CHUNK
    nt = seq // Lc
    ng = len(POOL_WINDOWS)
    width = ng * POOL_GROUP
    uspecs = [pl.BlockSpec((Lc, POOL_GROUP), functools.partial(
        lambda b, n, g: (b * nt + n, OFF_UP // POOL_GROUP + g), g=g)) for g in range(ng)]
    return pl.pallas_call(
        _pool_kernel,
        grid=(batch, nt),
        in_specs=uspecs + [pl.BlockSpec((ng, POOL_GROUP, POOL_GROUP), lambda b, n: (0, 0, 0)),
                           pl.BlockSpec((1, width), lambda b, n: (0, 0))],
        out_specs=pl.BlockSpec((Lc, width), lambda b, n: (b * nt + n, 0)),
        out_shape=jax.ShapeDtypeStruct((T, width), F32),
        scratch_shapes=[pltpu.VMEM((ng, POOL_MAXW + Lc, POOL_GROUP), F32)],
        compiler_params=_cparams(("arbitrary", "arbitrary")),
        name="pool",
    )(proj, proj, proj, proj, pool_w.astype(BF16), pool_scale.reshape(1, width))


def _out_kernel(r_ref, s_ref, a_ref, p_ref, gain_ref, w_ref, x_ref, g1_ref, o_ref, mix_ref):
    j = pl.program_id(1)

    @pl.when(j == 0)
    def _():
        for idx, ref in enumerate((r_ref, s_ref, a_ref, p_ref)):
            v = ref[...]
            width = v.shape[1]
            sl = slice(idx * width, (idx + 1) * width)
            ms = jnp.mean(v * v, axis=-1, keepdims=True)
            mix_ref[:, sl] = (v * lax.rsqrt(ms + EPS) * gain_ref[:, sl]).astype(BF16)

    acc = jnp.dot(mix_ref[...], w_ref[...], preferred_element_type=F32)
    o_ref[...] = x_ref[...] + g1_ref[0] * acc


def _out_proj(branches, gain, w_out_bf, x2, g1, seq):
    T, D = x2.shape
    B = g1.shape[0]
    tm, tn = 512, 512
    per_b = seq // tm
    width = branches[0].shape[1]
    bspec = pl.BlockSpec((tm, width), lambda i, j: (i, 0))
    return pl.pallas_call(
        _out_kernel,
        grid=(T // tm, D // tn),
        in_specs=[bspec, bspec, bspec, bspec,
                  pl.BlockSpec((1, D), lambda i, j: (0, 0)),
                  pl.BlockSpec((D, tn), lambda i, j: (0, j)),
                  pl.BlockSpec((tm, tn), lambda i, j: (i, j)),
                  pl.BlockSpec((1, 1, tn), lambda i, j: (i // per_b, 0, j))],
        out_specs=pl.BlockSpec((tm, tn), lambda i, j: (i, j)),
        out_shape=jax.ShapeDtypeStruct((T, D), F32),
        scratch_shapes=[pltpu.VMEM((tm, D), BF16)],
        compiler_params=_cparams(("arbitrary", "arbitrary")),
        name="out_proj",
    )(*branches, gain.reshape(1, D), w_out_bf, x2, g1.reshape(B, 1, D))


PEER_TOPK_TB = 128


def _iter_topk(s, k):
    n = s.shape[0]
    rows = lax.broadcasted_iota(jnp.int32, s.shape, 0).astype(F32)
    vals, idxs = [], []
    for _ in range(k):
        m = jnp.max(s, axis=0, keepdims=True)
        idx = jnp.min(jnp.where(s == m, rows, float(n)), axis=0, keepdims=True)
        s = jnp.where(rows == idx, -jnp.inf, s)
        vals.append(m)
        idxs.append(idx)
    return jnp.concatenate(vals, axis=0), jnp.concatenate(idxs, axis=0)


def _topk_kernel(q_ref, keys_ref, e_ref, g_ref):
    K = PEER_TOPK
    half = PEER_N_KEYS
    e_all, g_all = [], []
    for h in range(PEER_HEADS):
        tops = []
        for p in range(2):
            c0 = h * 2 * half + p * half
            qhp = q_ref[:, c0:c0 + half].astype(BF16)
            s = lax.dot_general(keys_ref[p], qhp, (((1,), (1,)), ((), ())), preferred_element_type=F32)
            tops.append(_iter_topk(s, K))
        (s0, i0), (s1, i1) = tops
        cand_s = jnp.concatenate([s0[a:a + 1, :] + s1 for a in range(K)], axis=0)
        cand_e = jnp.concatenate([i0[a:a + 1, :] * PEER_N_KEYS + i1 for a in range(K)], axis=0)
        rows = lax.broadcasted_iota(jnp.int32, cand_s.shape, 0).astype(F32)
        bs, be = [], []
        for _ in range(K):
            m = jnp.max(cand_s, axis=0, keepdims=True)
            pos = jnp.min(jnp.where(cand_s == m, rows, float(K * K)), axis=0, keepdims=True)
            hit = rows == pos
            be.append(jnp.sum(jnp.where(hit, cand_e, 0.0), axis=0, keepdims=True))
            cand_s = jnp.where(hit, -jnp.inf, cand_s)
            bs.append(m)
        best = jnp.concatenate(bs, axis=0)
        pexp = jnp.exp(best - best[0:1, :])
        g_all.append(pexp / jnp.sum(pexp, axis=0, keepdims=True))
        e_all.append(jnp.concatenate(be, axis=0))
    e_ref[...] = jnp.concatenate(e_all, axis=0).T.astype(jnp.int32)
    g_ref[...] = jnp.concatenate(g_all, axis=0).T


def _peer_topk(q, sub_keys_bf):
    T, QW = q.shape
    tb = PEER_TOPK_TB
    return pl.pallas_call(
        _topk_kernel,
        grid=(T // tb,),
        in_specs=[pl.BlockSpec((tb, QW), lambda i: (i, 0)),
                  pl.BlockSpec((2, PEER_N_KEYS, PEER_N_KEYS), lambda i: (0, 0, 0))],
        out_specs=[pl.BlockSpec((tb, PEER_HK), lambda i: (i, 0)),
                   pl.BlockSpec((tb, PEER_HK), lambda i: (i, 0))],
        out_shape=[jax.ShapeDtypeStruct((T, PEER_HK), jnp.int32),
                   jax.ShapeDtypeStruct((T, PEER_HK), F32)],
        compiler_params=_cparams(("arbitrary",)),
        name="peer_topk",
    )(q, sub_keys_bf)


def _peer_kernel(ea_ref, e0_ref, h_ref, gate_ref, x_ref, g2_ref, u_hbm, v_hbm, o_ref,
                 ubuf, vbuf, cmat_ref, sem, *, n_tokens):
    i = pl.program_id(0)
    D = h_ref.shape[1]
    nch = D // V7X_LANES
    ngrp = PEER_HK // V7X_SUBLANES

    def row_copies(e, slot, k):
        return (pltpu.make_async_copy(u_hbm.at[pl.ds(e, 1), :], ubuf.at[slot, pl.ds(k, 1), :], sem.at[0, slot]),
                pltpu.make_async_copy(v_hbm.at[pl.ds(e, 1), :], vbuf.at[slot, pl.ds(k, 1), :], sem.at[1, slot]))

    def issue(idx_ref, r, slot):
        def body(k, carry):
            e = idx_ref[r, k]
            cu, cv = row_copies(e, slot, k)
            cu.start()
            cv.start()
            return carry
        lax.fori_loop(0, PEER_HK, body, 0)

    @pl.when(i == 0)
    def _():
        for tt in range(PEER_AHEAD):
            issue(e0_ref, tt, tt)

    ones_row = jnp.ones((V7X_SUBLANES, V7X_LANES), F32)
    ones_sq = jnp.ones((V7X_LANES, V7X_LANES), F32)
    r_i = lax.broadcasted_iota(jnp.int32, (PEER_HK, V7X_LANES), 0)
    c_i = lax.broadcasted_iota(jnp.int32, (PEER_HK, V7X_LANES), 1)
    eye = r_i == c_i

    for tt in range(PEER_TB):
        gt = i * PEER_TB + tt
        slot = tt % PEER_SLOTS

        @pl.when(gt + PEER_AHEAD < n_tokens)
        def _():
            issue(ea_ref, tt, (tt + PEER_AHEAD) % PEER_SLOTS)

        pltpu.make_async_copy(ubuf.at[slot], ubuf.at[slot], sem.at[0, slot]).wait()
        pltpu.make_async_copy(vbuf.at[slot], vbuf.at[slot], sem.at[1, slot]).wait()

        accs = [jnp.zeros((V7X_SUBLANES, V7X_LANES), F32) for _ in range(ngrp)]
        for c in range(nch):
            cs = slice(c * V7X_LANES, (c + 1) * V7X_LANES)
            hb = jnp.broadcast_to(h_ref[pl.ds(tt, 1), cs], (V7X_SUBLANES, V7X_LANES))
            for j in range(ngrp):
                accs[j] = accs[j] + ubuf[slot, j * V7X_SUBLANES:(j + 1) * V7X_SUBLANES, cs] * hb
        acc = jnp.concatenate(accs, axis=0)
        a_row = lax.dot_general(ones_row, acc, (((1,), (1,)), ((), ())),
                                preferred_element_type=F32, precision=HIGHEST)[0:1, :]
        c_row = gate_ref[pl.ds(tt, 1), :] * _gelu(a_row)
        c_diag = jnp.where(eye, jnp.broadcast_to(c_row, (PEER_HK, V7X_LANES)), 0.0)
        cmat_ref[...] = jnp.dot(c_diag, ones_sq, preferred_element_type=F32, precision=HIGHEST)

        for c in range(nch):
            cs = slice(c * V7X_LANES, (c + 1) * V7X_LANES)
            o = jnp.zeros((V7X_SUBLANES, V7X_LANES), F32)
            for j in range(ngrp):
                rs = slice(j * V7X_SUBLANES, (j + 1) * V7X_SUBLANES)
                o = o + cmat_ref[rs, :] * vbuf[slot, rs, cs]
            o_ref[pl.ds(tt, 1), cs] = jnp.sum(o, axis=0, keepdims=True)

    o_ref[...] = x_ref[...] + g2_ref[0] * o_ref[...]


def _peer_gather(experts, gates, h32, x2, g2, u_tab, v_tab, seq):
    T, D = x2.shape
    B = g2.shape[0]
    tb = PEER_TB
    per_b = seq // tb
    e_ahead = jnp.concatenate([experts[PEER_AHEAD:], experts[:PEER_AHEAD]], axis=0)
    row = pl.BlockSpec((tb, D), lambda i: (i, 0))
    kern = functools.partial(_peer_kernel, n_tokens=T)
    return pl.pallas_call(
        kern,
        grid=(T // tb,),
        in_specs=[pl.BlockSpec((tb, PEER_HK), lambda i: (i, 0), memory_space=pltpu.SMEM),
                  pl.BlockSpec((tb, PEER_HK), lambda i: (0, 0), memory_space=pltpu.SMEM),
                  row,
                  pl.BlockSpec((tb, PEER_HK), lambda i: (i, 0)),
                  row,
                  pl.BlockSpec((1, 1, D), lambda i: (i // per_b, 0, 0)),
                  pl.BlockSpec(memory_space=pl.ANY),
                  pl.BlockSpec(memory_space=pl.ANY)],
        out_specs=row,
        out_shape=jax.ShapeDtypeStruct((T, D), F32),
        scratch_shapes=[pltpu.VMEM((PEER_SLOTS, PEER_HK, D), F32),
                        pltpu.VMEM((PEER_SLOTS, PEER_HK, D), F32),
                        pltpu.VMEM((PEER_HK, V7X_LANES), F32),
                        pltpu.SemaphoreType.DMA((2, PEER_SLOTS))],
        compiler_params=_cparams(("arbitrary",)),
        name="peer_gather",
    )(e_ahead, experts, h32, gates, x2, g2.reshape(B, 1, D), u_tab, v_tab)


def _layer(x2, mod, cos_r, sin_r, cos_a, sin_a, batch, seq, norm1_g, norm2_g, w_in, w_out, out_norm_g,
           ssm_a_re, ssm_a_im, ssm_log_dt, ssm_b_re, ssm_b_im, ssm_c_re, ssm_c_im,
           ssm_d, ssm_w_glu, ssm_b_glu, attn_q_norm, attn_k_norm, attn_sinks,
           pool_w, pool_scale, peer_w_query, peer_sub_keys, peer_u, peer_v):
    D = D_MODEL
    sh1, sc1, g1, sh2, sc2, g2 = [mod[:batch, k * D:(k + 1) * D] for k in range(6)]

    (h1,) = _norm_mod(x2, norm1_g, sc1, sh1, seq, (BF16,))
    proj = _matmul(h1, w_in.astype(BF16), tm=512, tn=640)

    ret = _retention(proj, cos_r, sin_r, batch, seq)
    ssm_params = _ssm_params(ssm_a_re, ssm_a_im, ssm_log_dt, ssm_b_re, ssm_b_im, ssm_c_re, ssm_c_im)
    ssm = _glu(_ssm_scan(proj, ssm_params, ssm_d, batch, seq), ssm_w_glu.astype(BF16), ssm_b_glu)
    swa = _swa(proj, cos_a, sin_a, attn_q_norm, attn_k_norm, attn_sinks, batch, seq)
    pool = _pool(proj, pool_w, pool_scale, batch, seq)
    x2 = _out_proj((ret, ssm, swa, pool), out_norm_g, w_out.astype(BF16), x2, g1, seq)

    h2b, h2f = _norm_mod(x2, norm2_g, sc2, sh2, seq, (BF16, F32))
    q = _matmul(h2b, peer_w_query.astype(BF16), tm=512, tn=512)
    experts, gates = _peer_topk(q, peer_sub_keys.astype(BF16))
    return _peer_gather(experts, gates, h2f, x2, g2, peer_u, peer_v, seq)


def kernel(x, c, positions, ada_w, ada_b, norm1_g, norm2_g, w_in, w_out, out_norm_g, ssm_a_re, ssm_a_im, ssm_log_dt, ssm_b_re, ssm_b_im, ssm_c_re, ssm_c_im, ssm_d, ssm_w_glu, ssm_b_glu, attn_q_norm, attn_k_norm, attn_sinks, pool_w, pool_scale, peer_w_query, peer_sub_keys, peer_u, peer_v):
    B, S, D = x.shape
    depth = ada_w.shape[0]
    c8 = jnp.concatenate([c, jnp.zeros((V7X_SUBLANES - B, D), c.dtype)], axis=0)
    mod = _ada_mod(c8, ada_w, ada_b)
    cos_r, sin_r = _rope_tables(positions, RET_DK)
    cos_a, sin_a = _rope_tables(positions, SWA_HEAD_DIM)
    x2 = x.reshape(B * S, D)
    for i in range(depth):
        x2 = _layer(x2, mod[i], cos_r, sin_r, cos_a, sin_a, B, S, norm1_g[i], norm2_g[i], w_in[i], w_out[i],
                    out_norm_g[i], ssm_a_re[i], ssm_a_im[i], ssm_log_dt[i], ssm_b_re[i], ssm_b_im[i],
                    ssm_c_re[i], ssm_c_im[i], ssm_d[i], ssm_w_glu[i], ssm_b_glu[i],
                    attn_q_norm[i], attn_k_norm[i], attn_sinks[i], pool_w[i], pool_scale[i],
                    peer_w_query[i], peer_sub_keys[i], peer_u[i], peer_v[i])
    return x2.reshape(B, S, D)
```

```python
import functools
import math

import numpy as np
import jax
import jax.numpy as jnp
from jax import lax
from jax.experimental import pallas as pl
from jax.experimental.pallas import tpu as pltpu

F32 = jnp.float32
BF16 = jnp.bfloat16
HIGHEST = lax.Precision.HIGHEST

V7X_LANES = 128
V7X_SUBLANES = 8
V7X_VMEM_BYTES = 64 * 1024 * 1024
VMEM_LIMIT = 52 * 1024 * 1024

D_MODEL = 4096
EPS = 1e-6
ROPE_THETA = 10000.0

RET_HEADS = 4
RET_DV = 256
RET_DK = 128
RET_CHUNK = 128

SSM_WIDTH = 1024
SSM_GROUP = 16
SSM_GROUPS = 64
SSM_STATE = 64
SSM_COLS = 128
SSM_CH = SSM_COLS // SSM_GROUP * SSM_STATE
SSM_CHUNK = 256

SWA_HEAD_DIM = 64
SWA_Q_HEADS = 16
SWA_KV_HEADS = 2
SWA_WINDOW = 128

POOL_WINDOWS = (2, 4, 8, 16)
POOL_GROUP = 256
POOL_MAXW = 16

OFF_QR, OFF_KR, OFF_VR, OFF_GR, OFF_US, OFF_QA, OFF_KA, OFF_VA, OFF_UP = (
    0, 512, 1024, 2048, 3072, 4096, 5120, 5248, 5376)
IN_COLS = 6400

PEER_N_KEYS = 128
PEER_HEADS = 8
PEER_TOPK = 16
PEER_HK = PEER_HEADS * PEER_TOPK
PEER_TB = 8
PEER_GROUPS = PEER_HK // V7X_SUBLANES
PEER_S1_GROUPS = 2
PEER_S2_GROUPS = 4


def _cparams(sem):
    return pltpu.CompilerParams(dimension_semantics=sem, vmem_limit_bytes=VMEM_LIMIT)


def _gelu(x):
    return 0.5 * x * (1.0 + lax.erf(x * math.sqrt(0.5)))


def _sigmoid(x):
    return 1.0 / (1.0 + jnp.exp(-x))


def _ada_kernel(c_ref, w_ref, b_ref, o_ref):
    c = c_ref[...]
    a = c * _sigmoid(c)
    o_ref[0] = jnp.dot(a, w_ref[0], preferred_element_type=F32, precision=HIGHEST) + b_ref[0]


def _ada_mod(c8, ada_w, ada_b):
    L, D, N = ada_w.shape
    tn = 512
    return pl.pallas_call(
        _ada_kernel,
        grid=(L, N // tn),
        in_specs=[pl.BlockSpec((8, D), lambda l, j: (0, 0)),
                  pl.BlockSpec((1, D, tn), lambda l, j: (l, 0, j)),
                  pl.BlockSpec((1, 1, tn), lambda l, j: (l, 0, j))],
        out_specs=pl.BlockSpec((1, 8, tn), lambda l, j: (l, 0, j)),
        out_shape=jax.ShapeDtypeStruct((L, 8, N), F32),
        compiler_params=_cparams(("arbitrary", "arbitrary")),
        name="ada_mod",
    )(c8, ada_w, ada_b.reshape(L, 1, N))


def _norm_mod_kernel(x_ref, g_ref, sc_ref, sh_ref, *o_refs):
    x = x_ref[...]
    ms = jnp.mean(x * x, axis=-1, keepdims=True)
    y = x * lax.rsqrt(ms + EPS) * g_ref[...]
    h = y * (1.0 + sc_ref[0]) + sh_ref[0]
    for o in o_refs:
        o[...] = h.astype(o.dtype)


def _norm_mod(x2, gain, sc, sh, seq, out_dtypes):
    T, D = x2.shape
    B = sc.shape[0]
    tm = 256
    per_b = seq // tm
    row = pl.BlockSpec((tm, D), lambda i: (i, 0))
    bspec = pl.BlockSpec((1, 1, D), lambda i: (i // per_b, 0, 0))
    outs = pl.pallas_call(
        _norm_mod_kernel,
        grid=(T // tm,),
        in_specs=[row, pl.BlockSpec((1, D), lambda i: (0, 0)), bspec, bspec],
        out_specs=[row for _ in out_dtypes],
        out_shape=[jax.ShapeDtypeStruct((T, D), dt) for dt in out_dtypes],
        compiler_params=_cparams(("arbitrary",)),
        name="norm_mod",
    )(x2, gain.reshape(1, D), sc.reshape(B, 1, D), sh.reshape(B, 1, D))
    return outs


def _mm_kernel(a_ref, w_ref, o_ref):
    o_ref[...] = jnp.dot(a_ref[...], w_ref[...], preferred_element_type=F32)


def _matmul(a, w, tm=512, tn=512):
    M, K = a.shape
    N = w.shape[1]
    return pl.pallas_call(
        _mm_kernel,
        grid=(M // tm, N // tn),
        in_specs=[pl.BlockSpec((tm, K), lambda i, j: (i, 0)),
                  pl.BlockSpec((K, tn), lambda i, j: (0, j))],
        out_specs=pl.BlockSpec((tm, tn), lambda i, j: (i, j)),
        out_shape=jax.ShapeDtypeStruct((M, N), F32),
        compiler_params=_cparams(("arbitrary", "arbitrary")),
        name="matmul",
    )(a, w)


def _rope_tables(positions, d):
    half = d // 2
    inv = ROPE_THETA ** (-jnp.arange(half, dtype=F32) * 2.0 / d)
    ang = positions.astype(F32).reshape(-1, 1) * inv
    cos, sin = jnp.cos(ang), jnp.sin(ang)
    reps = V7X_LANES // d
    cosf = jnp.concatenate([cos, cos] * reps, axis=-1)
    sinf = jnp.concatenate([-sin, sin] * reps, axis=-1)
    return cosf, sinf


def _ret_consts():
    L, H = RET_CHUNK, RET_HEADS
    log_g = np.log(1.0 - 2.0 ** (-5.0 - np.arange(H, dtype=np.float64)))
    idx = np.arange(L, dtype=np.float64)
    diff = idx[:, None] - idx[None, :]
    decay = np.where(diff >= 0, np.exp(np.maximum(diff, 0.0)[None] * log_g[:, None, None]), 0.0)
    w_k = np.exp((L - 1 - idx)[None, :] * log_g[:, None])
    w_q = np.exp((idx + 1)[None, :] * log_g[:, None])
    gam = np.exp(L * log_g)
    wk_full = np.broadcast_to(w_k[:, :, None], (H, L, RET_DK))
    wq_full = np.broadcast_to(w_q[:, :, None], (H, L, RET_DK))
    gam_full = np.broadcast_to(gam[:, None, None], (H, 1, RET_DV))
    return (jnp.asarray(decay, F32), jnp.asarray(wq_full, F32), jnp.asarray(wk_full, F32),
            jnp.asarray(gam_full, F32))


def _ret_kernel(q_ref, k_ref, v_ref, g_ref, cos_ref, sin_ref, dec_ref, wq_ref, wk_ref, gam_ref,
                o_ref, r_ref):
    n = pl.program_id(2)

    @pl.when(n == 0)
    def _():
        r_ref[...] = jnp.zeros_like(r_ref)

    cos = cos_ref[...]
    sin = sin_ref[...]
    q = q_ref[...]
    k = k_ref[...]
    q = q * cos + pltpu.roll(q, RET_DK // 2, 1) * sin
    k = (k * cos + pltpu.roll(k, RET_DK // 2, 1) * sin) * (RET_DK ** -0.5)
    vb = v_ref[...].astype(BF16)
    s = lax.dot_general(q.astype(BF16), k.astype(BF16), (((1,), (1,)), ((), ())),
                        preferred_element_type=F32) * dec_ref[0]
    o = jnp.dot(s.astype(BF16), vb, preferred_element_type=F32)
    r = r_ref[...]
    o = o + jnp.dot((q * wq_ref[0]).astype(BF16), r.astype(BF16), preferred_element_type=F32)
    kv = lax.dot_general((k * wk_ref[0]).astype(BF16), vb, (((0,), (0,)), ((), ())),
                         preferred_element_type=F32)
    r_ref[...] = gam_ref[0] * r + kv
    mu = jnp.mean(o, axis=-1, keepdims=True)
    oc = o - mu
    var = jnp.mean(oc * oc, axis=-1, keepdims=True)
    g = g_ref[...]
    o_ref[...] = g * _sigmoid(g) * (oc * lax.rsqrt(var + EPS))


def _retention(proj, cosf, sinf, batch, seq):
    T = proj.shape[0]
    L = RET_CHUNK
    n = seq // L
    dec, wq, wk, gam = _ret_consts()

    def rows(b, h, c):
        return b * n + c

    return pl.pallas_call(
        _ret_kernel,
        grid=(batch, RET_HEADS, n),
        in_specs=[
            pl.BlockSpec((L, RET_DK), lambda b, h, c: (rows(b, h, c), OFF_QR // RET_DK + h)),
            pl.BlockSpec((L, RET_DK), lambda b, h, c: (rows(b, h, c), OFF_KR // RET_DK + h)),
            pl.BlockSpec((L, RET_DV), lambda b, h, c: (rows(b, h, c), OFF_VR // RET_DV + h)),
            pl.BlockSpec((L, RET_DV), lambda b, h, c: (rows(b, h, c), OFF_GR // RET_DV + h)),
            pl.BlockSpec((L, RET_DK), lambda b, h, c: (rows(b, h, c), 0)),
            pl.BlockSpec((L, RET_DK), lambda b, h, c: (rows(b, h, c), 0)),
            pl.BlockSpec((1, L, L), lambda b, h, c: (h, 0, 0)),
            pl.BlockSpec((1, L, RET_DK), lambda b, h, c: (h, 0, 0)),
            pl.BlockSpec((1, L, RET_DK), lambda b, h, c: (h, 0, 0)),
            pl.BlockSpec((1, 1, RET_DV), lambda b, h, c: (h, 0, 0)),
        ],
        out_specs=pl.BlockSpec((L, RET_DV), lambda b, h, c: (rows(b, h, c), h)),
        out_shape=jax.ShapeDtypeStruct((T, RET_HEADS * RET_DV), F32),
        scratch_shapes=[pltpu.VMEM((RET_DK, RET_DV), F32)],
        compiler_params=_cparams(("arbitrary", "arbitrary", "arbitrary")),
        name="retention",
    )(proj, proj, proj, proj, cosf, sinf, dec, wq, wk, gam)


def _ssm_params(a_re, a_im, log_dt, b_re, b_im, c_re, c_im):
    dt = jnp.exp(log_dt)[:, None]
    mag = jnp.exp(a_re * dt)
    ab_re = mag * jnp.cos(a_im * dt)
    ab_im = mag * jnp.sin(a_im * dt)
    den = a_re * a_re + a_im * a_im
    nr = ab_re - 1.0
    f_re = (nr * a_re + ab_im * a_im) / den
    f_im = (ab_im * a_re - nr * a_im) / den
    bb_re = f_re[..., None] * b_re - f_im[..., None] * b_im
    bb_im = f_re[..., None] * b_im + f_im[..., None] * b_re
    nblk = SSM_WIDTH // SSM_COLS
    gpb = SSM_COLS // SSM_GROUP
    eye = jnp.eye(gpb, dtype=F32)

    def blockdiag_in(bb):
        t = bb.reshape(nblk, gpb, SSM_STATE, SSM_GROUP)
        m = jnp.einsum('ngpc,gh->ngchp', t, eye)
        return m.reshape(nblk, SSM_COLS, SSM_CH)

    def blockdiag_out(cc):
        t = cc.reshape(nblk, gpb, SSM_GROUP, SSM_STATE)
        m = jnp.einsum('ngcp,gh->ngphc', t, eye)
        return m.reshape(nblk, SSM_CH, SSM_COLS)

    rounds = int(math.log2(SSM_CHUNK))
    pr, pi = [ab_re], [ab_im]
    for _ in range(rounds - 1):
        r, i = pr[-1], pi[-1]
        pr.append(r * r - i * i)
        pi.append(2.0 * r * i)
    apr = jnp.stack(pr, 0).reshape(rounds, nblk, SSM_CH).transpose(1, 0, 2)
    api = jnp.stack(pi, 0).reshape(rounds, nblk, SSM_CH).transpose(1, 0, 2)
    return (blockdiag_in(bb_re).astype(BF16), blockdiag_in(bb_im).astype(BF16),
            blockdiag_out(c_re).astype(BF16), blockdiag_out(c_im).astype(BF16), apr, api)


def _ssm_kernel(u_ref, bre_ref, bim_ref, cre_ref, cim_ref, apr_ref, api_ref, d_ref, o_ref,
                cr_ref, ci_ref):
    n = pl.program_id(2)

    @pl.when(n == 0)
    def _():
        cr_ref[...] = jnp.zeros_like(cr_ref)
        ci_ref[...] = jnp.zeros_like(ci_ref)

    u = u_ref[...]
    ub = u.astype(BF16)
    xr = jnp.dot(ub, bre_ref[0], preferred_element_type=F32)
    xi = jnp.dot(ub, bim_ref[0], preferred_element_type=F32)
    row = lax.broadcasted_iota(jnp.int32, xr.shape, 0)
    a_r = apr_ref[0, 0:1, :]
    a_i = api_ref[0, 0:1, :]
    c_r = cr_ref[...]
    c_i = ci_ref[...]
    first = row == 0
    xr = xr + jnp.where(first, a_r * c_r - a_i * c_i, 0.0)
    xi = xi + jnp.where(first, a_r * c_i + a_i * c_r, 0.0)
    for k in range(int(math.log2(SSM_CHUNK))):
        s = 1 << k
        p_r = apr_ref[0, k:k + 1, :]
        p_i = api_ref[0, k:k + 1, :]
        keep = row >= s
        sr = jnp.where(keep, pltpu.roll(xr, s, 0), 0.0)
        si = jnp.where(keep, pltpu.roll(xi, s, 0), 0.0)
        xr, xi = xr + p_r * sr - p_i * si, xi + p_r * si + p_i * sr
    cr_ref[...] = xr[SSM_CHUNK - 1:SSM_CHUNK, :]
    ci_ref[...] = xi[SSM_CHUNK - 1:SSM_CHUNK, :]
    y = (jnp.dot(xr.astype(BF16), cre_ref[0], preferred_element_type=F32)
         - jnp.dot(xi.astype(BF16), cim_ref[0], preferred_element_type=F32))
    o_ref[...] = _gelu(y + d_ref[...] * u)


def _ssm_scan(proj, params, d, batch, seq):
    T = proj.shape[0]
    bre, bim, cre, cim, apr, api = params
    nblk = SSM_WIDTH // SSM_COLS
    nt = seq // SSM_CHUNK
    rounds = apr.shape[1]
    wspec_in = pl.BlockSpec((1, SSM_COLS, SSM_CH), lambda b, c, n: (c, 0, 0))
    wspec_out = pl.BlockSpec((1, SSM_CH, SSM_COLS), lambda b, c, n: (c, 0, 0))
    pspec = pl.BlockSpec((1, rounds, SSM_CH), lambda b, c, n: (c, 0, 0))
    return pl.pallas_call(
        _ssm_kernel,
        grid=(batch, nblk, nt),
        in_specs=[pl.BlockSpec((SSM_CHUNK, SSM_COLS), lambda b, c, n: (b * nt + n, OFF_US // SSM_COLS + c)),
                  wspec_in, wspec_in, wspec_out, wspec_out, pspec, pspec,
                  pl.BlockSpec((1, SSM_COLS), lambda b, c, n: (0, c))],
        out_specs=pl.BlockSpec((SSM_CHUNK, SSM_COLS), lambda b, c, n: (b * nt + n, c)),
        out_shape=jax.ShapeDtypeStruct((T, SSM_WIDTH), F32),
        scratch_shapes=[pltpu.VMEM((1, SSM_CH), F32), pltpu.VMEM((1, SSM_CH), F32)],
        compiler_params=_cparams(("arbitrary", "arbitrary", "arbitrary")),
        name="ssm_scan",
    )(proj, bre, bim, cre, cim, apr, api, d.reshape(1, SSM_WIDTH))


def _glu_kernel(y_ref, w_ref, b_ref, o_ref):
    y = y_ref[...]
    z = jnp.dot(y.astype(BF16), w_ref[...], preferred_element_type=F32) + b_ref[...]
    o_ref[...] = y * _sigmoid(z)


def _glu(y, w, b):
    T, W = y.shape
    tm = 512
    return pl.pallas_call(
        _glu_kernel,
        grid=(T // tm,),
        in_specs=[pl.BlockSpec((tm, W), lambda i: (i, 0)),
                  pl.BlockSpec((W, W), lambda i: (0, 0)),
                  pl.BlockSpec((1, W), lambda i: (0, 0))],
        out_specs=pl.BlockSpec((tm, W), lambda i: (i, 0)),
        out_shape=jax.ShapeDtypeStruct((T, W), F32),
        compiler_params=_cparams(("arbitrary",)),
        name="ssm_glu",
    )(y, w, b.reshape(1, W))


def _swa_kernel(sink_ref, q_ref, k_ref, v_ref, cos_ref, sin_ref, qn_ref, kn_ref, o_ref, pk_ref, pv_ref):
    n = pl.program_id(1)
    W = SWA_WINDOW
    hd = SWA_HEAD_DIM

    @pl.when(n == 0)
    def _():
        pk_ref[...] = jnp.zeros_like(pk_ref)
        pv_ref[...] = jnp.zeros_like(pv_ref)

    cos = cos_ref[...]
    sin = sin_ref[...]
    lane = lax.broadcasted_iota(jnp.int32, (W, V7X_LANES), 1)
    lo = lane < hd
    first_half = (lane % hd) < (hd // 2)

    def head_norm(t, gain):
        tt = t * t
        s_all = jnp.sum(tt, axis=-1, keepdims=True)
        s_lo = jnp.sum(jnp.where(lo, tt, 0.0), axis=-1, keepdims=True)
        ms = jnp.where(lo, s_lo, s_all - s_lo) * (1.0 / hd)
        return t * lax.rsqrt(ms + EPS) * gain

    def rope(t):
        sw = jnp.where(first_half, pltpu.roll(t, V7X_LANES - hd // 2, 1), pltpu.roll(t, hd // 2, 1))
        return t * cos + sw * sin

    kc = rope(head_norm(k_ref[...], kn_ref[...]))
    vc = v_ref[...]
    keys = jnp.concatenate([pk_ref[...], kc], axis=0)
    vals = jnp.concatenate([pv_ref[...], vc], axis=0)
    lane2 = lax.broadcasted_iota(jnp.int32, (2 * W, V7X_LANES), 1)
    lo2 = lane2 < hd

    def dup(x, h):
        sw = pltpu.roll(x, hd, 1)
        return (jnp.where(lo2, x, sw) if h == 0 else jnp.where(lo2, sw, x)).astype(BF16)

    kk = [dup(keys, h) for h in range(SWA_KV_HEADS)]
    vv = [dup(vals, h) for h in range(SWA_KV_HEADS)]
    qi = lax.broadcasted_iota(jnp.int32, (W, 2 * W), 0)
    ci = lax.broadcasted_iota(jnp.int32, (W, 2 * W), 1)
    prev_floor = qi + jnp.where(n > 0, 0, 2 * W)
    valid = jnp.logical_or(jnp.logical_and(ci < W, ci > prev_floor),
                           jnp.logical_and(ci >= W, (ci - W) <= qi))
    tiles = SWA_Q_HEADS * hd // V7X_LANES
    per_kv = tiles // SWA_KV_HEADS
    for m in range(tiles):
        sl = slice(m * V7X_LANES, (m + 1) * V7X_LANES)
        qt = rope(head_norm(q_ref[:, sl], qn_ref[:, sl]))
        h = m // per_kv
        outs = []
        for half in range(2):
            sel = lo if half == 0 else jnp.logical_not(lo)
            qm = jnp.where(sel, qt, 0.0).astype(BF16)
            s = lax.dot_general(qm, kk[h], (((1,), (1,)), ((), ())),
                                preferred_element_type=F32) * (hd ** -0.5)
            s = jnp.where(valid, s, -1e30)
            sink = sink_ref[2 * m + half]
            mx = jnp.maximum(jnp.max(s, axis=-1, keepdims=True), sink)
            p = jnp.exp(s - mx)
            den = jnp.sum(p, axis=-1, keepdims=True) + jnp.exp(sink - mx)
            p = p / den
            outs.append(jnp.dot(p.astype(BF16), vv[h], preferred_element_type=F32))
        o_ref[:, sl] = jnp.where(lo, outs[0], outs[1])
    pk_ref[...] = kc
    pv_ref[...] = vc


def _swa(proj, cosf, sinf, q_norm, k_norm, sinks, batch, seq):
    T = proj.shape[0]
    W = SWA_WINDOW
    nb = seq // W
    qw = SWA_Q_HEADS * SWA_HEAD_DIM
    qn = jnp.tile(q_norm, SWA_Q_HEADS).reshape(1, qw)
    kn = jnp.tile(k_norm, SWA_KV_HEADS).reshape(1, V7X_LANES)
    return pl.pallas_call(
        _swa_kernel,
        grid=(batch, nb),
        in_specs=[pl.BlockSpec(memory_space=pltpu.SMEM),
                  pl.BlockSpec((W, qw), lambda b, n: (b * nb + n, OFF_QA // qw)),
                  pl.BlockSpec((W, V7X_LANES), lambda b, n: (b * nb + n, OFF_KA // V7X_LANES)),
                  pl.BlockSpec((W, V7X_LANES), lambda b, n: (b * nb + n, OFF_VA // V7X_LANES)),
                  pl.BlockSpec((W, V7X_LANES), lambda b, n: (b * nb + n, 0)),
                  pl.BlockSpec((W, V7X_LANES), lambda b, n: (b * nb + n, 0)),
                  pl.BlockSpec((1, qw), lambda b, n: (0, 0)),
                  pl.BlockSpec((1, V7X_LANES), lambda b, n: (0, 0))],
        out_specs=pl.BlockSpec((W, qw), lambda b, n: (b * nb + n, 0)),
        out_shape=jax.ShapeDtypeStruct((T, qw), F32),
        scratch_shapes=[pltpu.VMEM((W, V7X_LANES), F32), pltpu.VMEM((W, V7X_LANES), F32)],
        compiler_params=_cparams(("arbitrary", "arbitrary")),
        name="swa",
    )(sinks, proj, proj, proj, cosf, sinf, qn, kn)


POOL_CHUNK = 256


def _pool_kernel(u0_ref, u1_ref, u2_ref, u3_ref, w_ref, sc_ref, o_ref, ext_ref):
    n = pl.program_id(1)
    Lc = POOL_CHUNK

    @pl.when(n == 0)
    def _():
        ext_ref[:, 0:POOL_MAXW, :] = jnp.zeros((len(POOL_WINDOWS), POOL_MAXW, POOL_GROUP), F32)

    t = n * Lc + lax.broadcasted_iota(jnp.int32, (Lc, POOL_GROUP), 0)
    for g, (u_ref, w) in enumerate(zip((u0_ref, u1_ref, u2_ref, u3_ref), POOL_WINDOWS)):
        u = u_ref[...]
        ext_ref[g, POOL_MAXW:POOL_MAXW + Lc, :] = u
        acc = u
        for k in range(1, w):
            acc = acc + ext_ref[g, POOL_MAXW - k:POOL_MAXW - k + Lc, :]
        cnt = jnp.minimum(t + 1, w).astype(F32)
        pooled = acc / cnt - u
        y = jnp.dot(pooled.astype(BF16), w_ref[g], preferred_element_type=F32)
        sl = slice(g * POOL_GROUP, (g + 1) * POOL_GROUP)
        o_ref[:, sl] = y * sc_ref[:, sl]
        ext_ref[g, 0:POOL_MAXW, :] = u[Lc - POOL_MAXW:, :]


def _pool(proj, pool_w, pool_scale, batch, seq):
    T = proj.shape[0]
    Lc = POOL_CHUNK
    nt = seq // Lc
    ng = len(POOL_WINDOWS)
    width = ng * POOL_GROUP
    uspecs = [pl.BlockSpec((Lc, POOL_GROUP), functools.partial(
        lambda b, n, g: (b * nt + n, OFF_UP // POOL_GROUP + g), g=g)) for g in range(ng)]
    return pl.pallas_call(
        _pool_kernel,
        grid=(batch, nt),
        in_specs=uspecs + [pl.BlockSpec((ng, POOL_GROUP, POOL_GROUP), lambda b, n: (0, 0, 0)),
                           pl.BlockSpec((1, width), lambda b, n: (0, 0))],
        out_specs=pl.BlockSpec((Lc, width), lambda b, n: (b * nt + n, 0)),
        out_shape=jax.ShapeDtypeStruct((T, width), F32),
        scratch_shapes=[pltpu.VMEM((ng, POOL_MAXW + Lc, POOL_GROUP), F32)],
        compiler_params=_cparams(("arbitrary", "arbitrary")),
        name="pool",
    )(proj, proj, proj, proj, pool_w.astype(BF16), pool_scale.reshape(1, width))


def _out_kernel(r_ref, s_ref, a_ref, p_ref, gain_ref, w_ref, x_ref, g1_ref, o_ref, mix_ref):
    j = pl.program_id(1)

    @pl.when(j == 0)
    def _():
        for idx, ref in enumerate((r_ref, s_ref, a_ref, p_ref)):
            v = ref[...]
            width = v.shape[1]
            sl = slice(idx * width, (idx + 1) * width)
            ms = jnp.mean(v * v, axis=-1, keepdims=True)
            mix_ref[:, sl] = (v * lax.rsqrt(ms + EPS) * gain_ref[:, sl]).astype(BF16)

    acc = jnp.dot(mix_ref[...], w_ref[...], preferred_element_type=F32)
    o_ref[...] = x_ref[...] + g1_ref[0] * acc


def _out_proj(branches, gain, w_out_bf, x2, g1, seq):
    T, D = x2.shape
    B = g1.shape[0]
    tm, tn = 512, 512
    per_b = seq // tm
    width = branches[0].shape[1]
    bspec = pl.BlockSpec((tm, width), lambda i, j: (i, 0))
    return pl.pallas_call(
        _out_kernel,
        grid=(T // tm, D // tn),
        in_specs=[bspec, bspec, bspec, bspec,
                  pl.BlockSpec((1, D), lambda i, j: (0, 0)),
                  pl.BlockSpec((D, tn), lambda i, j: (0, j)),
                  pl.BlockSpec((tm, tn), lambda i, j: (i, j)),
                  pl.BlockSpec((1, 1, tn), lambda i, j: (i // per_b, 0, j))],
        out_specs=pl.BlockSpec((tm, tn), lambda i, j: (i, j)),
        out_shape=jax.ShapeDtypeStruct((T, D), F32),
        scratch_shapes=[pltpu.VMEM((tm, D), BF16)],
        compiler_params=_cparams(("arbitrary", "arbitrary")),
        name="out_proj",
    )(*branches, gain.reshape(1, D), w_out_bf, x2, g1.reshape(B, 1, D))


PEER_TOPK_TB = 128


def _iter_topk(s, k):
    n = s.shape[0]
    rows = lax.broadcasted_iota(jnp.int32, s.shape, 0).astype(F32)
    vals, idxs = [], []
    for _ in range(k):
        m = jnp.max(s, axis=0, keepdims=True)
        idx = jnp.min(jnp.where(s == m, rows, float(n)), axis=0, keepdims=True)
        s = jnp.where(rows == idx, -jnp.inf, s)
        vals.append(m)
        idxs.append(idx)
    return jnp.concatenate(vals, axis=0), jnp.concatenate(idxs, axis=0)


def _topk_kernel(q_ref, keys_ref, e_ref, g_ref):
    K = PEER_TOPK
    half = PEER_N_KEYS
    e_all, g_all = [], []
    for h in range(PEER_HEADS):
        tops = []
        for p in range(2):
            c0 = h * 2 * half + p * half
            qhp = q_ref[:, c0:c0 + half].astype(BF16)
            s = lax.dot_general(keys_ref[p], qhp, (((1,), (1,)), ((), ())), preferred_element_type=F32)
            tops.append(_iter_topk(s, K))
        (s0, i0), (s1, i1) = tops
        cand_s = jnp.concatenate([s0[a:a + 1, :] + s1 for a in range(K)], axis=0)
        cand_e = jnp.concatenate([i0[a:a + 1, :] * PEER_N_KEYS + i1 for a in range(K)], axis=0)
        rows = lax.broadcasted_iota(jnp.int32, cand_s.shape, 0).astype(F32)
        bs, be = [], []
        for _ in range(K):
            m = jnp.max(cand_s, axis=0, keepdims=True)
            pos = jnp.min(jnp.where(cand_s == m, rows, float(K * K)), axis=0, keepdims=True)
            hit = rows == pos
            be.append(jnp.sum(jnp.where(hit, cand_e, 0.0), axis=0, keepdims=True))
            cand_s = jnp.where(hit, -jnp.inf, cand_s)
            bs.append(m)
        best = jnp.concatenate(bs, axis=0)
        pexp = jnp.exp(best - best[0:1, :])
        g_all.append(pexp / jnp.sum(pexp, axis=0, keepdims=True))
        e_all.append(jnp.concatenate(be, axis=0))
    e_ref[...] = jnp.concatenate(e_all, axis=0).T.astype(jnp.int32)
    g_ref[...] = jnp.concatenate(g_all, axis=0).T


def _peer_topk(q, sub_keys_bf):
    T, QW = q.shape
    tb = PEER_TOPK_TB
    return pl.pallas_call(
        _topk_kernel,
        grid=(T // tb,),
        in_specs=[pl.BlockSpec((tb, QW), lambda i: (i, 0)),
                  pl.BlockSpec((2, PEER_N_KEYS, PEER_N_KEYS), lambda i: (0, 0, 0))],
        out_specs=[pl.BlockSpec((tb, PEER_HK), lambda i: (i, 0)),
                   pl.BlockSpec((tb, PEER_HK), lambda i: (i, 0))],
        out_shape=[jax.ShapeDtypeStruct((T, PEER_HK), jnp.int32),
                   jax.ShapeDtypeStruct((T, PEER_HK), F32)],
        compiler_params=_cparams(("arbitrary",)),
        name="peer_topk",
    )(q, sub_keys_bf)


def _pack_tables(u_tab, v_tab):
    ub = lax.bitcast_convert_type(u_tab.astype(BF16), jnp.uint16).astype(jnp.uint32)
    vb = lax.bitcast_convert_type(v_tab.astype(BF16), jnp.uint16).astype(jnp.uint32)
    return lax.bitcast_convert_type(ub | (vb << 16), jnp.int32)


def _peer_kernel(ec_ref, en_ref, h_ref, gate_ref, x_ref, g2_ref, tab_hbm, o_ref,
                 buf, acc_ref, cmat_ref, sem):
    i = pl.program_id(0)
    for half_step in range(2):
        row0 = half_step * PEER_TB
        if half_step == 0:
            _peer_half(i, row0, 0, PEER_TB, ec_ref, PEER_TB, ec_ref, h_ref, gate_ref, tab_hbm, o_ref,
                       buf, acc_ref, cmat_ref, sem)
        else:
            _peer_half(i, row0, PEER_TB, 0, en_ref, 0, ec_ref, h_ref, gate_ref, tab_hbm, o_ref,
                       buf, acc_ref, cmat_ref, sem)
    o_ref[...] = x_ref[...] + g2_ref[0] * o_ref[...]


def _peer_half(i, row0, cur, nxt, nidx_ref, nrow0, ec_ref, h_ref, gate_ref, tab_hbm, o_ref,
               buf, acc_ref, cmat_ref, sem):
    D = h_ref.shape[1]
    nch = D // V7X_LANES
    half_ch = nch // 2
    G = PEER_GROUPS
    zero = jnp.zeros((V7X_SUBLANES, V7X_LANES), F32)

    def start_group(idx_ref, r, g, slot):
        for s in range(V7X_SUBLANES):
            e = idx_ref[r * PEER_HK + g * V7X_SUBLANES + s]
            pltpu.make_async_copy(tab_hbm.at[pl.ds(e, 1), :], buf.at[slot, g, pl.ds(s, 1), :],
                                  sem.at[slot]).start()

    def wait_slot(slot):
        pltpu.make_async_copy(buf.at[slot], buf.at[slot], sem.at[slot]).wait()

    if row0 == 0:
        @pl.when(i == 0)
        def _():
            for tt in range(PEER_TB):
                def body(g, carry, tt=tt):
                    start_group(ec_ref, tt, g, tt)
                    return carry
                lax.fori_loop(0, G, body, 0)

    for tt in range(PEER_TB):
        wait_slot(cur + tt)

        def s1_body(j, carry, tt=tt):
            parts = [[None] * 2 for _ in range(PEER_S1_GROUPS)]
            for c in range(nch):
                cs = slice(c * V7X_LANES, (c + 1) * V7X_LANES)
                hb = jnp.broadcast_to(h_ref[row0 + tt:row0 + tt + 1, cs], (V7X_SUBLANES, V7X_LANES))
                for q in range(PEER_S1_GROUPS):
                    w = buf[cur + tt, j * PEER_S1_GROUPS + q, :, cs]
                    t = lax.bitcast_convert_type(jnp.left_shift(w, 16), F32) * hb
                    p = parts[q][c % 2]
                    parts[q][c % 2] = t if p is None else p + t
            for q in range(PEER_S1_GROUPS):
                r0 = pl.multiple_of((tt * G + j * PEER_S1_GROUPS + q) * V7X_SUBLANES, V7X_SUBLANES)
                acc_ref[pl.ds(r0, V7X_SUBLANES), :] = parts[q][0] + parts[q][1]
            start_group(nidx_ref, nrow0 + tt, j, nxt + tt)
            return carry

        lax.fori_loop(0, G // PEER_S1_GROUPS, s1_body, 0)

    ones_row = jnp.ones((V7X_SUBLANES, V7X_LANES), F32)
    ones_sq = jnp.ones((V7X_LANES, V7X_LANES), F32)
    a8 = lax.dot_general(ones_row, acc_ref[...], (((1,), (1,)), ((), ())),
                         preferred_element_type=F32, precision=HIGHEST)
    a_tok = jnp.concatenate([a8[0:1, tt * PEER_HK:(tt + 1) * PEER_HK] for tt in range(PEER_TB)], axis=0)
    c_tok = gate_ref[row0:row0 + PEER_TB, :] * _gelu(a_tok)
    r_i = lax.broadcasted_iota(jnp.int32, (PEER_HK, V7X_LANES), 0)
    c_i = lax.broadcasted_iota(jnp.int32, (PEER_HK, V7X_LANES), 1)
    eye = r_i == c_i
    for tt in range(PEER_TB):
        c_diag = jnp.where(eye, jnp.broadcast_to(c_tok[tt:tt + 1, :], (PEER_HK, V7X_LANES)), 0.0)
        cmat_ref[tt * PEER_HK:(tt + 1) * PEER_HK, :] = jnp.dot(c_diag, ones_sq, preferred_element_type=F32,
                                                                precision=HIGHEST)

    for tt in range(PEER_TB):
        for half in range(2):
            def s2_body(j, accs, tt=tt, half=half):
                new = list(accs)
                for q in range(PEER_S2_GROUPS):
                    g = j * PEER_S2_GROUPS + q
                    r0 = pl.multiple_of((tt * G + g) * V7X_SUBLANES, V7X_SUBLANES)
                    cm = cmat_ref[pl.ds(r0, V7X_SUBLANES), :]
                    for cc in range(half_ch):
                        c = half * half_ch + cc
                        w = buf[cur + tt, g, :, c * V7X_LANES:(c + 1) * V7X_LANES]
                        v = lax.bitcast_convert_type(jnp.bitwise_and(w, jnp.int32(-65536)), F32)
                        new[cc] = new[cc] + cm * v
                start_group(nidx_ref, nrow0 + tt, G // 2 + half * (G // 4) + j, nxt + tt)
                return tuple(new)

            accs = lax.fori_loop(0, G // PEER_S2_GROUPS, s2_body, tuple(zero for _ in range(half_ch)))
            for cc in range(half_ch):
                c = half * half_ch + cc
                o_ref[row0 + tt:row0 + tt + 1, c * V7X_LANES:(c + 1) * V7X_LANES] = jnp.sum(
                    accs[cc], axis=0, keepdims=True)

    if row0 != 0:
        @pl.when(i == pl.num_programs(0) - 1)
        def _():
            for tt in range(PEER_TB):
                wait_slot(nxt + tt)


def _peer_gather(experts, gates, h32, x2, g2, uv_tab, seq):
    T, D = x2.shape
    B = g2.shape[0]
    tb = 2 * PEER_TB
    per_b = seq // tb
    nsteps = T // tb
    G = PEER_GROUPS
    assert G // PEER_S1_GROUPS == G // 2 and G // PEER_S2_GROUPS == G // 4
    row = pl.BlockSpec((tb, D), lambda i: (i, 0))
    eflat = experts.reshape(T * PEER_HK)
    return pl.pallas_call(
        _peer_kernel,
        grid=(nsteps,),
        in_specs=[pl.BlockSpec((tb * PEER_HK,), lambda i: (i,), memory_space=pltpu.SMEM),
                  pl.BlockSpec((tb * PEER_HK,), lambda i: (jnp.minimum(i + 1, nsteps - 1),),
                               memory_space=pltpu.SMEM),
                  row,
                  pl.BlockSpec((tb, PEER_HK), lambda i: (i, 0)),
                  row,
                  pl.BlockSpec((1, 1, D), lambda i: (i // per_b, 0, 0)),
                  pl.BlockSpec(memory_space=pl.ANY)],
        out_specs=row,
        out_shape=jax.ShapeDtypeStruct((T, D), F32),
        scratch_shapes=[pltpu.VMEM((tb, G, V7X_SUBLANES, D), jnp.int32),
                        pltpu.VMEM((PEER_TB * PEER_HK, V7X_LANES), F32),
                        pltpu.VMEM((PEER_TB * PEER_HK, V7X_LANES), F32),
                        pltpu.SemaphoreType.DMA((tb,))],
        compiler_params=_cparams(("arbitrary",)),
        name="peer_gather",
    )(eflat, eflat, h32, gates, x2, g2.reshape(B, 1, D), uv_tab)


def _layer(x2, mod, cos_r, sin_r, cos_a, sin_a, batch, seq, norm1_g, norm2_g, w_in, w_out, out_norm_g,
           ssm_a_re, ssm_a_im, ssm_log_dt, ssm_b_re, ssm_b_im, ssm_c_re, ssm_c_im,
           ssm_d, ssm_w_glu, ssm_b_glu, attn_q_norm, attn_k_norm, attn_sinks,
           pool_w, pool_scale, peer_w_query, peer_sub_keys, peer_u, peer_v):
    D = D_MODEL
    sh1, sc1, g1, sh2, sc2, g2 = [mod[:batch, k * D:(k + 1) * D] for k in range(6)]

    (h1,) = _norm_mod(x2, norm1_g, sc1, sh1, seq, (BF16,))
    proj = _matmul(h1, w_in.astype(BF16), tm=512, tn=640)

    ret = _retention(proj, cos_r, sin_r, batch, seq)
    ssm_params = _ssm_params(ssm_a_re, ssm_a_im, ssm_log_dt, ssm_b_re, ssm_b_im, ssm_c_re, ssm_c_im)
    ssm = _glu(_ssm_scan(proj, ssm_params, ssm_d, batch, seq), ssm_w_glu.astype(BF16), ssm_b_glu)
    swa = _swa(proj, cos_a, sin_a, attn_q_norm, attn_k_norm, attn_sinks, batch, seq)
    pool = _pool(proj, pool_w, pool_scale, batch, seq)
    x2 = _out_proj((ret, ssm, swa, pool), out_norm_g, w_out.astype(BF16), x2, g1, seq)

    h2b, h2f = _norm_mod(x2, norm2_g, sc2, sh2, seq, (BF16, F32))
    q = _matmul(h2b, peer_w_query.astype(BF16), tm=512, tn=512)
    experts, gates = _peer_topk(q, peer_sub_keys.astype(BF16))
    return _peer_gather(experts, gates, h2f, x2, g2, _pack_tables(peer_u, peer_v), seq)


def kernel(x, c, positions, ada_w, ada_b, norm1_g, norm2_g, w_in, w_out, out_norm_g, ssm_a_re, ssm_a_im, ssm_log_dt, ssm_b_re, ssm_b_im, ssm_c_re, ssm_c_im, ssm_d, ssm_w_glu, ssm_b_glu, attn_q_norm, attn_k_norm, attn_sinks, pool_w, pool_scale, peer_w_query, peer_sub_keys, peer_u, peer_v):
    B, S, D = x.shape
    depth = ada_w.shape[0]
    c8 = jnp.concatenate([c, jnp.zeros((V7X_SUBLANES - B, D), c.dtype)], axis=0)
    mod = _ada_mod(c8, ada_w, ada_b)
    cos_r, sin_r = _rope_tables(positions, RET_DK)
    cos_a, sin_a = _rope_tables(positions, SWA_HEAD_DIM)
    x2 = x.reshape(B * S, D)
    for i in range(depth):
        x2 = _layer(x2, mod[i], cos_r, sin_r, cos_a, sin_a, B, S, norm1_g[i], norm2_g[i], w_in[i], w_out[i],
                    out_norm_g[i], ssm_a_re[i], ssm_a_im[i], ssm_log_dt[i], ssm_b_re[i], ssm_b_im[i],
                    ssm_c_re[i], ssm_c_im[i], ssm_d[i], ssm_w_glu[i], ssm_b_glu[i],
                    attn_q_norm[i], attn_k_norm[i], attn_sinks[i], pool_w[i], pool_scale[i],
                    peer_w_query[i], peer_sub_keys[i], peer_u[i], peer_v[i])
    return x2.reshape(B, S, D)
```

```python
import functools
import math

import numpy as np
import jax
import jax.numpy as jnp
from jax import lax
from jax.experimental import pallas as pl
from jax.experimental.pallas import tpu as pltpu

F32 = jnp.float32
BF16 = jnp.bfloat16
HIGHEST = lax.Precision.HIGHEST

V7X_LANES = 128
V7X_SUBLANES = 8
V7X_VMEM_BYTES = 64 * 1024 * 1024
VMEM_LIMIT = 52 * 1024 * 1024

D_MODEL = 4096
EPS = 1e-6
ROPE_THETA = 10000.0

RET_HEADS = 4
RET_DV = 256
RET_DK = 128
RET_CHUNK = 128

SSM_WIDTH = 1024
SSM_GROUP = 16
SSM_GROUPS = 64
SSM_STATE = 64
SSM_COLS = 128
SSM_CH = SSM_COLS // SSM_GROUP * SSM_STATE
SSM_CHUNK = 256

SWA_HEAD_DIM = 64
SWA_Q_HEADS = 16
SWA_KV_HEADS = 2
SWA_WINDOW = 128

POOL_WINDOWS = (2, 4, 8, 16)
POOL_GROUP = 256
POOL_MAXW = 16

OFF_QR, OFF_KR, OFF_VR, OFF_GR, OFF_US, OFF_QA, OFF_KA, OFF_VA, OFF_UP = (
    0, 512, 1024, 2048, 3072, 4096, 5120, 5248, 5376)
IN_COLS = 6400

PEER_N_KEYS = 128
PEER_HEADS = 8
PEER_TOPK = 16
PEER_HK = PEER_HEADS * PEER_TOPK
PEER_TB = 8
PEER_GROUPS = PEER_HK // V7X_SUBLANES
PEER_S1_GROUPS = 2
PEER_S1_CHAINS = 4
PEER_S2_GROUPS = 4


def _cparams(sem):
    return pltpu.CompilerParams(dimension_semantics=sem, vmem_limit_bytes=VMEM_LIMIT)


def _gelu(x):
    return 0.5 * x * (1.0 + lax.erf(x * math.sqrt(0.5)))


def _sigmoid(x):
    return 1.0 / (1.0 + jnp.exp(-x))


def _ada_kernel(c_ref, w_ref, b_ref, o_ref):
    c = c_ref[...]
    a = c * _sigmoid(c)
    o_ref[0] = jnp.dot(a, w_ref[0], preferred_element_type=F32, precision=HIGHEST) + b_ref[0]


def _ada_mod(c8, ada_w, ada_b):
    L, D, N = ada_w.shape
    tn = 512
    return pl.pallas_call(
        _ada_kernel,
        grid=(L, N // tn),
        in_specs=[pl.BlockSpec((8, D), lambda l, j: (0, 0)),
                  pl.BlockSpec((1, D, tn), lambda l, j: (l, 0, j)),
                  pl.BlockSpec((1, 1, tn), lambda l, j: (l, 0, j))],
        out_specs=pl.BlockSpec((1, 8, tn), lambda l, j: (l, 0, j)),
        out_shape=jax.ShapeDtypeStruct((L, 8, N), F32),
        compiler_params=_cparams(("arbitrary", "arbitrary")),
        name="ada_mod",
    )(c8, ada_w, ada_b.reshape(L, 1, N))


def _norm_mod_kernel(x_ref, g_ref, sc_ref, sh_ref, *o_refs):
    x = x_ref[...]
    ms = jnp.mean(x * x, axis=-1, keepdims=True)
    y = x * lax.rsqrt(ms + EPS) * g_ref[...]
    h = y * (1.0 + sc_ref[0]) + sh_ref[0]
    for o in o_refs:
        o[...] = h.astype(o.dtype)


def _norm_mod(x2, gain, sc, sh, seq, out_dtypes):
    T, D = x2.shape
    B = sc.shape[0]
    tm = 256
    per_b = seq // tm
    row = pl.BlockSpec((tm, D), lambda i: (i, 0))
    bspec = pl.BlockSpec((1, 1, D), lambda i: (i // per_b, 0, 0))
    outs = pl.pallas_call(
        _norm_mod_kernel,
        grid=(T // tm,),
        in_specs=[row, pl.BlockSpec((1, D), lambda i: (0, 0)), bspec, bspec],
        out_specs=[row for _ in out_dtypes],
        out_shape=[jax.ShapeDtypeStruct((T, D), dt) for dt in out_dtypes],
        compiler_params=_cparams(("arbitrary",)),
        name="norm_mod",
    )(x2, gain.reshape(1, D), sc.reshape(B, 1, D), sh.reshape(B, 1, D))
    return outs


def _mm_kernel(a_ref, w_ref, o_ref):
    o_ref[...] = jnp.dot(a_ref[...], w_ref[...], preferred_element_type=F32)


def _matmul(a, w, tm=512, tn=512):
    M, K = a.shape
    N = w.shape[1]
    return pl.pallas_call(
        _mm_kernel,
        grid=(M // tm, N // tn),
        in_specs=[pl.BlockSpec((tm, K), lambda i, j: (i, 0)),
                  pl.BlockSpec((K, tn), lambda i, j: (0, j))],
        out_specs=pl.BlockSpec((tm, tn), lambda i, j: (i, j)),
        out_shape=jax.ShapeDtypeStruct((M, N), F32),
        compiler_params=_cparams(("arbitrary", "arbitrary")),
        name="matmul",
    )(a, w)


def _rope_tables(positions, d):
    half = d // 2
    inv = ROPE_THETA ** (-jnp.arange(half, dtype=F32) * 2.0 / d)
    ang = positions.astype(F32).reshape(-1, 1) * inv
    cos, sin = jnp.cos(ang), jnp.sin(ang)
    reps = V7X_LANES // d
    cosf = jnp.concatenate([cos, cos] * reps, axis=-1)
    sinf = jnp.concatenate([-sin, sin] * reps, axis=-1)
    return cosf, sinf


def _ret_consts():
    L, H = RET_CHUNK, RET_HEADS
    log_g = np.log(1.0 - 2.0 ** (-5.0 - np.arange(H, dtype=np.float64)))
    idx = np.arange(L, dtype=np.float64)
    diff = idx[:, None] - idx[None, :]
    decay = np.where(diff >= 0, np.exp(np.maximum(diff, 0.0)[None] * log_g[:, None, None]), 0.0)
    w_k = np.exp((L - 1 - idx)[None, :] * log_g[:, None])
    w_q = np.exp((idx + 1)[None, :] * log_g[:, None])
    gam = np.exp(L * log_g)
    wk_full = np.broadcast_to(w_k[:, :, None], (H, L, RET_DK))
    wq_full = np.broadcast_to(w_q[:, :, None], (H, L, RET_DK))
    gam_full = np.broadcast_to(gam[:, None, None], (H, 1, RET_DV))
    return (jnp.asarray(decay, F32), jnp.asarray(wq_full, F32), jnp.asarray(wk_full, F32),
            jnp.asarray(gam_full, F32))


def _ret_kernel(q_ref, k_ref, v_ref, g_ref, cos_ref, sin_ref, dec_ref, wq_ref, wk_ref, gam_ref,
                o_ref, r_ref):
    n = pl.program_id(2)

    @pl.when(n == 0)
    def _():
        r_ref[...] = jnp.zeros_like(r_ref)

    cos = cos_ref[...]
    sin = sin_ref[...]
    q = q_ref[...]
    k = k_ref[...]
    q = q * cos + pltpu.roll(q, RET_DK // 2, 1) * sin
    k = (k * cos + pltpu.roll(k, RET_DK // 2, 1) * sin) * (RET_DK ** -0.5)
    vb = v_ref[...].astype(BF16)
    s = lax.dot_general(q.astype(BF16), k.astype(BF16), (((1,), (1,)), ((), ())),
                        preferred_element_type=F32) * dec_ref[0]
    o = jnp.dot(s.astype(BF16), vb, preferred_element_type=F32)
    r = r_ref[...]
    o = o + jnp.dot((q * wq_ref[0]).astype(BF16), r.astype(BF16), preferred_element_type=F32)
    kv = lax.dot_general((k * wk_ref[0]).astype(BF16), vb, (((0,), (0,)), ((), ())),
                         preferred_element_type=F32)
    r_ref[...] = gam_ref[0] * r + kv
    mu = jnp.mean(o, axis=-1, keepdims=True)
    oc = o - mu
    var = jnp.mean(oc * oc, axis=-1, keepdims=True)
    g = g_ref[...]
    o_ref[...] = g * _sigmoid(g) * (oc * lax.rsqrt(var + EPS))


def _retention(proj, cosf, sinf, batch, seq):
    T = proj.shape[0]
    L = RET_CHUNK
    n = seq // L
    dec, wq, wk, gam = _ret_consts()

    def rows(b, h, c):
        return b * n + c

    return pl.pallas_call(
        _ret_kernel,
        grid=(batch, RET_HEADS, n),
        in_specs=[
            pl.BlockSpec((L, RET_DK), lambda b, h, c: (rows(b, h, c), OFF_QR // RET_DK + h)),
            pl.BlockSpec((L, RET_DK), lambda b, h, c: (rows(b, h, c), OFF_KR // RET_DK + h)),
            pl.BlockSpec((L, RET_DV), lambda b, h, c: (rows(b, h, c), OFF_VR // RET_DV + h)),
            pl.BlockSpec((L, RET_DV), lambda b, h, c: (rows(b, h, c), OFF_GR // RET_DV + h)),
            pl.BlockSpec((L, RET_DK), lambda b, h, c: (rows(b, h, c), 0)),
            pl.BlockSpec((L, RET_DK), lambda b, h, c: (rows(b, h, c), 0)),
            pl.BlockSpec((1, L, L), lambda b, h, c: (h, 0, 0)),
            pl.BlockSpec((1, L, RET_DK), lambda b, h, c: (h, 0, 0)),
            pl.BlockSpec((1, L, RET_DK), lambda b, h, c: (h, 0, 0)),
            pl.BlockSpec((1, 1, RET_DV), lambda b, h, c: (h, 0, 0)),
        ],
        out_specs=pl.BlockSpec((L, RET_DV), lambda b, h, c: (rows(b, h, c), h)),
        out_shape=jax.ShapeDtypeStruct((T, RET_HEADS * RET_DV), F32),
        scratch_shapes=[pltpu.VMEM((RET_DK, RET_DV), F32)],
        compiler_params=_cparams(("arbitrary", "arbitrary", "arbitrary")),
        name="retention",
    )(proj, proj, proj, proj, cosf, sinf, dec, wq, wk, gam)


def _ssm_params(a_re, a_im, log_dt, b_re, b_im, c_re, c_im):
    dt = jnp.exp(log_dt)[:, None]
    mag = jnp.exp(a_re * dt)
    ab_re = mag * jnp.cos(a_im * dt)
    ab_im = mag * jnp.sin(a_im * dt)
    den = a_re * a_re + a_im * a_im
    nr = ab_re - 1.0
    f_re = (nr * a_re + ab_im * a_im) / den
    f_im = (ab_im * a_re - nr * a_im) / den
    bb_re = f_re[..., None] * b_re - f_im[..., None] * b_im
    bb_im = f_re[..., None] * b_im + f_im[..., None] * b_re
    nblk = SSM_WIDTH // SSM_COLS
    gpb = SSM_COLS // SSM_GROUP
    eye = jnp.eye(gpb, dtype=F32)

    def blockdiag_in(bb):
        t = bb.reshape(nblk, gpb, SSM_STATE, SSM_GROUP)
        m = jnp.einsum('ngpc,gh->ngchp', t, eye)
        return m.reshape(nblk, SSM_COLS, SSM_CH)

    def blockdiag_out(cc):
        t = cc.reshape(nblk, gpb, SSM_GROUP, SSM_STATE)
        m = jnp.einsum('ngcp,gh->ngphc', t, eye)
        return m.reshape(nblk, SSM_CH, SSM_COLS)

    rounds = int(math.log2(SSM_CHUNK))
    pr, pi = [ab_re], [ab_im]
    for _ in range(rounds - 1):
        r, i = pr[-1], pi[-1]
        pr.append(r * r - i * i)
        pi.append(2.0 * r * i)
    apr = jnp.stack(pr, 0).reshape(rounds, nblk, SSM_CH).transpose(1, 0, 2)
    api = jnp.stack(pi, 0).reshape(rounds, nblk, SSM_CH).transpose(1, 0, 2)
    qr, qi = [ab_re], [ab_im]
    for _ in range(V7X_SUBLANES - 1):
        r, i = qr[-1], qi[-1]
        qr.append(r * ab_re - i * ab_im)
        qi.append(r * ab_im + i * ab_re)
    ppr = jnp.stack(qr, 0).reshape(V7X_SUBLANES, nblk, SSM_CH).transpose(1, 0, 2)
    ppi = jnp.stack(qi, 0).reshape(V7X_SUBLANES, nblk, SSM_CH).transpose(1, 0, 2)
    return (blockdiag_in(bb_re).astype(BF16), blockdiag_in(bb_im).astype(BF16),
            blockdiag_out(c_re).astype(BF16), blockdiag_out(c_im).astype(BF16), apr, api, ppr, ppi)


def _cmul_add(xr, xi, p_r, p_i, sr, si):
    return xr + p_r * sr - p_i * si, xi + p_r * si + p_i * sr


def _ssm_kernel(u_ref, bre_ref, bim_ref, cre_ref, cim_ref, apr_ref, api_ref, ppr_ref, ppi_ref, d_ref, o_ref,
                cr_ref, ci_ref, xr_s, xi_s, hr_s, hi_s):
    n = pl.program_id(2)
    hs = V7X_SUBLANES
    ng = SSM_CHUNK // hs

    @pl.when(n == 0)
    def _():
        cr_ref[...] = jnp.zeros_like(cr_ref)
        ci_ref[...] = jnp.zeros_like(ci_ref)

    u = u_ref[...]
    ub = u.astype(BF16)
    xr = jnp.dot(ub, bre_ref[0], preferred_element_type=F32)
    xi = jnp.dot(ub, bim_ref[0], preferred_element_type=F32)
    sub = jnp.bitwise_and(lax.broadcasted_iota(jnp.int32, xr.shape, 0), hs - 1)
    in_rounds = int(math.log2(hs))
    for k in range(in_rounds):
        s = 1 << k
        keep = sub >= s
        sr = jnp.where(keep, pltpu.roll(xr, s, 0), 0.0)
        si = jnp.where(keep, pltpu.roll(xi, s, 0), 0.0)
        xr, xi = _cmul_add(xr, xi, apr_ref[0, k:k + 1, :], api_ref[0, k:k + 1, :], sr, si)
    nlt = SSM_CH // V7X_LANES
    for q in range(nlt):
        xr_s[q] = xr[:, q * V7X_LANES:(q + 1) * V7X_LANES]
        xi_s[q] = xi[:, q * V7X_LANES:(q + 1) * V7X_LANES]

    er = jnp.concatenate([xr_s[q, pl.ds(hs - 1, ng, stride=hs), :] for q in range(nlt)], axis=1)
    ei = jnp.concatenate([xi_s[q, pl.ds(hs - 1, ng, stride=hs), :] for q in range(nlt)], axis=1)
    c_r = cr_ref[...]
    c_i = ci_ref[...]
    grow = lax.broadcasted_iota(jnp.int32, er.shape, 0)
    first = grow == 0
    a8r = apr_ref[0, in_rounds:in_rounds + 1, :]
    a8i = api_ref[0, in_rounds:in_rounds + 1, :]
    er = er + jnp.where(first, a8r * c_r - a8i * c_i, 0.0)
    ei = ei + jnp.where(first, a8r * c_i + a8i * c_r, 0.0)
    for k in range(int(math.log2(ng))):
        s = 1 << k
        keep = grow >= s
        sr = jnp.where(keep, pltpu.roll(er, s, 0), 0.0)
        si = jnp.where(keep, pltpu.roll(ei, s, 0), 0.0)
        kk = in_rounds + k
        er, ei = _cmul_add(er, ei, apr_ref[0, kk:kk + 1, :], api_ref[0, kk:kk + 1, :], sr, si)
    cr_ref[...] = er[ng - 1:ng, :]
    ci_ref[...] = ei[ng - 1:ng, :]
    hr_s[...] = jnp.where(first, c_r, pltpu.roll(er, 1, 0))
    hi_s[...] = jnp.where(first, c_i, pltpu.roll(ei, 1, 0))
    p_r = ppr_ref[0]
    p_i = ppi_ref[0]
    for g in range(ng):
        rows = slice(g * hs, (g + 1) * hs)
        for q in range(nlt):
            ls = slice(q * V7X_LANES, (q + 1) * V7X_LANES)
            hb_r = jnp.broadcast_to(hr_s[g:g + 1, ls], (hs, V7X_LANES))
            hb_i = jnp.broadcast_to(hi_s[g:g + 1, ls], (hs, V7X_LANES))
            yr, yi = _cmul_add(xr_s[q, rows, :], xi_s[q, rows, :], p_r[:, ls], p_i[:, ls], hb_r, hb_i)
            xr_s[q, rows, :] = yr
            xi_s[q, rows, :] = yi
    hr = jnp.concatenate([xr_s[q] for q in range(nlt)], axis=1)
    hi = jnp.concatenate([xi_s[q] for q in range(nlt)], axis=1)
    y = (jnp.dot(hr.astype(BF16), cre_ref[0], preferred_element_type=F32)
         - jnp.dot(hi.astype(BF16), cim_ref[0], preferred_element_type=F32))
    o_ref[...] = _gelu(y + d_ref[...] * u)


def _ssm_scan(proj, params, d, batch, seq):
    T = proj.shape[0]
    bre, bim, cre, cim, apr, api, ppr, ppi = params
    nblk = SSM_WIDTH // SSM_COLS
    nt = seq // SSM_CHUNK
    rounds = apr.shape[1]
    ngroups = SSM_CHUNK // V7X_SUBLANES
    wspec_in = pl.BlockSpec((1, SSM_COLS, SSM_CH), lambda b, c, n: (c, 0, 0))
    wspec_out = pl.BlockSpec((1, SSM_CH, SSM_COLS), lambda b, c, n: (c, 0, 0))
    pspec = pl.BlockSpec((1, rounds, SSM_CH), lambda b, c, n: (c, 0, 0))
    qspec = pl.BlockSpec((1, V7X_SUBLANES, SSM_CH), lambda b, c, n: (c, 0, 0))
    return pl.pallas_call(
        _ssm_kernel,
        grid=(batch, nblk, nt),
        in_specs=[pl.BlockSpec((SSM_CHUNK, SSM_COLS), lambda b, c, n: (b * nt + n, OFF_US // SSM_COLS + c)),
                  wspec_in, wspec_in, wspec_out, wspec_out, pspec, pspec, qspec, qspec,
                  pl.BlockSpec((1, SSM_COLS), lambda b, c, n: (0, c))],
        out_specs=pl.BlockSpec((SSM_CHUNK, SSM_COLS), lambda b, c, n: (b * nt + n, c)),
        out_shape=jax.ShapeDtypeStruct((T, SSM_WIDTH), F32),
        scratch_shapes=[pltpu.VMEM((1, SSM_CH), F32), pltpu.VMEM((1, SSM_CH), F32),
                        pltpu.VMEM((SSM_CH // V7X_LANES, SSM_CHUNK, V7X_LANES), F32),
                        pltpu.VMEM((SSM_CH // V7X_LANES, SSM_CHUNK, V7X_LANES), F32),
                        pltpu.VMEM((ngroups, SSM_CH), F32), pltpu.VMEM((ngroups, SSM_CH), F32)],
        compiler_params=_cparams(("arbitrary", "arbitrary", "arbitrary")),
        name="ssm_scan",
    )(proj, bre, bim, cre, cim, apr, api, ppr, ppi, d.reshape(1, SSM_WIDTH))


def _glu_kernel(y_ref, w_ref, b_ref, o_ref):
    y = y_ref[...]
    z = jnp.dot(y.astype(BF16), w_ref[...], preferred_element_type=F32) + b_ref[...]
    o_ref[...] = y * _sigmoid(z)


def _glu(y, w, b):
    T, W = y.shape
    tm = 512
    return pl.pallas_call(
        _glu_kernel,
        grid=(T // tm,),
        in_specs=[pl.BlockSpec((tm, W), lambda i: (i, 0)),
                  pl.BlockSpec((W, W), lambda i: (0, 0)),
                  pl.BlockSpec((1, W), lambda i: (0, 0))],
        out_specs=pl.BlockSpec((tm, W), lambda i: (i, 0)),
        out_shape=jax.ShapeDtypeStruct((T, W), F32),
        compiler_params=_cparams(("arbitrary",)),
        name="ssm_glu",
    )(y, w, b.reshape(1, W))


def _swa_kernel(sink_ref, q_ref, k_ref, v_ref, cos_ref, sin_ref, qn_ref, kn_ref, o_ref, pk_ref, pv_ref):
    n = pl.program_id(1)
    W = SWA_WINDOW
    hd = SWA_HEAD_DIM

    @pl.when(n == 0)
    def _():
        pk_ref[...] = jnp.zeros_like(pk_ref)
        pv_ref[...] = jnp.zeros_like(pv_ref)

    cos = cos_ref[...]
    sin = sin_ref[...]
    lane = lax.broadcasted_iota(jnp.int32, (W, V7X_LANES), 1)
    lo = lane < hd
    first_half = (lane % hd) < (hd // 2)

    def head_norm(t, gain):
        tt = t * t
        s_all = jnp.sum(tt, axis=-1, keepdims=True)
        s_lo = jnp.sum(jnp.where(lo, tt, 0.0), axis=-1, keepdims=True)
        ms = jnp.where(lo, s_lo, s_all - s_lo) * (1.0 / hd)
        return t * lax.rsqrt(ms + EPS) * gain

    def rope(t):
        sw = jnp.where(first_half, pltpu.roll(t, V7X_LANES - hd // 2, 1), pltpu.roll(t, hd // 2, 1))
        return t * cos + sw * sin

    kc = rope(head_norm(k_ref[...], kn_ref[...]))
    vc = v_ref[...]
    keys = jnp.concatenate([pk_ref[...], kc], axis=0)
    vals = jnp.concatenate([pv_ref[...], vc], axis=0)
    lane2 = lax.broadcasted_iota(jnp.int32, (2 * W, V7X_LANES), 1)
    lo2 = lane2 < hd

    def dup(x, h):
        sw = pltpu.roll(x, hd, 1)
        return (jnp.where(lo2, x, sw) if h == 0 else jnp.where(lo2, sw, x)).astype(BF16)

    kk = [dup(keys, h) for h in range(SWA_KV_HEADS)]
    vv = [dup(vals, h) for h in range(SWA_KV_HEADS)]
    qi = lax.broadcasted_iota(jnp.int32, (W, 2 * W), 0)
    ci = lax.broadcasted_iota(jnp.int32, (W, 2 * W), 1)
    prev_floor = qi + jnp.where(n > 0, 0, 2 * W)
    valid = jnp.logical_or(jnp.logical_and(ci < W, ci > prev_floor),
                           jnp.logical_and(ci >= W, (ci - W) <= qi))
    tiles = SWA_Q_HEADS * hd // V7X_LANES
    per_kv = tiles // SWA_KV_HEADS
    for m in range(tiles):
        sl = slice(m * V7X_LANES, (m + 1) * V7X_LANES)
        qt = rope(head_norm(q_ref[:, sl], qn_ref[:, sl]))
        h = m // per_kv
        outs = []
        for half in range(2):
            sel = lo if half == 0 else jnp.logical_not(lo)
            qm = jnp.where(sel, qt, 0.0).astype(BF16)
            s = lax.dot_general(qm, kk[h], (((1,), (1,)), ((), ())),
                                preferred_element_type=F32) * (hd ** -0.5)
            s = jnp.where(valid, s, -1e30)
            sink = sink_ref[2 * m + half]
            mx = jnp.maximum(jnp.max(s, axis=-1, keepdims=True), sink)
            p = jnp.exp(s - mx)
            den = jnp.sum(p, axis=-1, keepdims=True) + jnp.exp(sink - mx)
            p = p / den
            outs.append(jnp.dot(p.astype(BF16), vv[h], preferred_element_type=F32))
        o_ref[:, sl] = jnp.where(lo, outs[0], outs[1])
    pk_ref[...] = kc
    pv_ref[...] = vc


def _swa(proj, cosf, sinf, q_norm, k_norm, sinks, batch, seq):
    T = proj.shape[0]
    W = SWA_WINDOW
    nb = seq // W
    qw = SWA_Q_HEADS * SWA_HEAD_DIM
    qn = jnp.tile(q_norm, SWA_Q_HEADS).reshape(1, qw)
    kn = jnp.tile(k_norm, SWA_KV_HEADS).reshape(1, V7X_LANES)
    return pl.pallas_call(
        _swa_kernel,
        grid=(batch, nb),
        in_specs=[pl.BlockSpec(memory_space=pltpu.SMEM),
                  pl.BlockSpec((W, qw), lambda b, n: (b * nb + n, OFF_QA // qw)),
                  pl.BlockSpec((W, V7X_LANES), lambda b, n: (b * nb + n, OFF_KA // V7X_LANES)),
                  pl.BlockSpec((W, V7X_LANES), lambda b, n: (b * nb + n, OFF_VA // V7X_LANES)),
                  pl.BlockSpec((W, V7X_LANES), lambda b, n: (b * nb + n, 0)),
                  pl.BlockSpec((W, V7X_LANES), lambda b, n: (b * nb + n, 0)),
                  pl.BlockSpec((1, qw), lambda b, n: (0, 0)),
                  pl.BlockSpec((1, V7X_LANES), lambda b, n: (0, 0))],
        out_specs=pl.BlockSpec((W, qw), lambda b, n: (b * nb + n, 0)),
        out_shape=jax.ShapeDtypeStruct((T, qw), F32),
        scratch_shapes=[pltpu.VMEM((W, V7X_LANES), F32), pltpu.VMEM((W, V7X_LANES), F32)],
        compiler_params=_cparams(("arbitrary", "arbitrary")),
        name="swa",
    )(sinks, proj, proj, proj, cosf, sinf, qn, kn)


POOL_CHUNK = 256


def _pool_kernel(u0_ref, u1_ref, u2_ref, u3_ref, w_ref, sc_ref, o_ref, ext_ref):
    n = pl.program_id(1)
    Lc = POOL_CHUNK

    @pl.when(n == 0)
    def _():
        ext_ref[:, 0:POOL_MAXW, :] = jnp.zeros((len(POOL_WINDOWS), POOL_MAXW, POOL_GROUP), F32)

    t = n * Lc + lax.broadcasted_iota(jnp.int32, (Lc, POOL_GROUP), 0)
    for g, (u_ref, w) in enumerate(zip((u0_ref, u1_ref, u2_ref, u3_ref), POOL_WINDOWS)):
        u = u_ref[...]
        ext_ref[g, POOL_MAXW:POOL_MAXW + Lc, :] = u
        acc = u
        for k in range(1, w):
            acc = acc + ext_ref[g, POOL_MAXW - k:POOL_MAXW - k + Lc, :]
        cnt = jnp.minimum(t + 1, w).astype(F32)
        pooled = acc / cnt - u
        y = jnp.dot(pooled.astype(BF16), w_ref[g], preferred_element_type=F32)
        sl = slice(g * POOL_GROUP, (g + 1) * POOL_GROUP)
        o_ref[:, sl] = y * sc_ref[:, sl]
        ext_ref[g, 0:POOL_MAXW, :] = u[Lc - POOL_MAXW:, :]


def _pool(proj, pool_w, pool_scale, batch, seq):
    T = proj.shape[0]
    Lc = POOL_CHUNK
    nt = seq // Lc
    ng = len(POOL_WINDOWS)
    width = ng * POOL_GROUP
    uspecs = [pl.BlockSpec((Lc, POOL_GROUP), functools.partial(
        lambda b, n, g: (b * nt + n, OFF_UP // POOL_GROUP + g), g=g)) for g in range(ng)]
    return pl.pallas_call(
        _pool_kernel,
        grid=(batch, nt),
        in_specs=uspecs + [pl.BlockSpec((ng, POOL_GROUP, POOL_GROUP), lambda b, n: (0, 0, 0)),
                           pl.BlockSpec((1, width), lambda b, n: (0, 0))],
        out_specs=pl.BlockSpec((Lc, width), lambda b, n: (b * nt + n, 0)),
        out_shape=jax.ShapeDtypeStruct((T, width), F32),
        scratch_shapes=[pltpu.VMEM((ng, POOL_MAXW + Lc, POOL_GROUP), F32)],
        compiler_params=_cparams(("arbitrary", "arbitrary")),
        name="pool",
    )(proj, proj, proj, proj, pool_w.astype(BF16), pool_scale.reshape(1, width))


def _out_kernel(r_ref, s_ref, a_ref, p_ref, gain_ref, w_ref, x_ref, g1_ref, o_ref, mix_ref):
    j = pl.program_id(1)

    @pl.when(j == 0)
    def _():
        for idx, ref in enumerate((r_ref, s_ref, a_ref, p_ref)):
            v = ref[...]
            width = v.shape[1]
            sl = slice(idx * width, (idx + 1) * width)
            ms = jnp.mean(v * v, axis=-1, keepdims=True)
            mix_ref[:, sl] = (v * lax.rsqrt(ms + EPS) * gain_ref[:, sl]).astype(BF16)

    acc = jnp.dot(mix_ref[...], w_ref[...], preferred_element_type=F32)
    o_ref[...] = x_ref[...] + g1_ref[0] * acc


def _out_proj(branches, gain, w_out_bf, x2, g1, seq):
    T, D = x2.shape
    B = g1.shape[0]
    tm, tn = 512, 512
    per_b = seq // tm
    width = branches[0].shape[1]
    bspec = pl.BlockSpec((tm, width), lambda i, j: (i, 0))
    return pl.pallas_call(
        _out_kernel,
        grid=(T // tm, D // tn),
        in_specs=[bspec, bspec, bspec, bspec,
                  pl.BlockSpec((1, D), lambda i, j: (0, 0)),
                  pl.BlockSpec((D, tn), lambda i, j: (0, j)),
                  pl.BlockSpec((tm, tn), lambda i, j: (i, j)),
                  pl.BlockSpec((1, 1, tn), lambda i, j: (i // per_b, 0, j))],
        out_specs=pl.BlockSpec((tm, tn), lambda i, j: (i, j)),
        out_shape=jax.ShapeDtypeStruct((T, D), F32),
        scratch_shapes=[pltpu.VMEM((tm, D), BF16)],
        compiler_params=_cparams(("arbitrary", "arbitrary")),
        name="out_proj",
    )(*branches, gain.reshape(1, D), w_out_bf, x2, g1.reshape(B, 1, D))


PEER_TOPK_TB = 128


def _iter_topk(s, k):
    n = s.shape[0]
    rows = lax.broadcasted_iota(jnp.int32, s.shape, 0).astype(F32)
    vals, idxs = [], []
    for _ in range(k):
        m = jnp.max(s, axis=0, keepdims=True)
        idx = jnp.min(jnp.where(s == m, rows, float(n)), axis=0, keepdims=True)
        s = jnp.where(rows == idx, -jnp.inf, s)
        vals.append(m)
        idxs.append(idx)
    return jnp.concatenate(vals, axis=0), jnp.concatenate(idxs, axis=0)


def _topk_kernel(q_ref, keys_ref, e_ref, g_ref):
    K = PEER_TOPK
    half = PEER_N_KEYS
    e_all, g_all = [], []
    for h in range(PEER_HEADS):
        tops = []
        for p in range(2):
            c0 = h * 2 * half + p * half
            qhp = q_ref[:, c0:c0 + half].astype(BF16)
            s = lax.dot_general(keys_ref[p], qhp, (((1,), (1,)), ((), ())), preferred_element_type=F32)
            tops.append(_iter_topk(s, K))
        (s0, i0), (s1, i1) = tops
        hs = V7X_SUBLANES
        io8 = lax.broadcasted_iota(jnp.int32, (hs, s0.shape[1]), 0).astype(F32)
        ps, pe, pp = [], [], []
        for a, b0 in [(0, 0), (0, hs)] + [(a, 0) for a in range(1, hs)]:
            ps.append(s0[a:a + 1, :] + s1[b0:b0 + hs, :])
            pe.append(i0[a:a + 1, :] * PEER_N_KEYS + i1[b0:b0 + hs, :])
            pp.append(io8 + float(a * K + b0))
        ps.append(s0[hs:K, :] + s1[0:1, :])
        pe.append(i0[hs:K, :] * PEER_N_KEYS + i1[0:1, :])
        pp.append((io8 + float(hs)) * float(K))
        cand_s = jnp.concatenate(ps, axis=0)
        cand_e = jnp.concatenate(pe, axis=0)
        rows = jnp.concatenate(pp, axis=0)
        bs, be = [], []
        for _ in range(K):
            m = jnp.max(cand_s, axis=0, keepdims=True)
            pos = jnp.min(jnp.where(cand_s == m, rows, float(K * K)), axis=0, keepdims=True)
            hit = rows == pos
            be.append(jnp.sum(jnp.where(hit, cand_e, 0.0), axis=0, keepdims=True))
            cand_s = jnp.where(hit, -jnp.inf, cand_s)
            bs.append(m)
        best = jnp.concatenate(bs, axis=0)
        pexp = jnp.exp(best - best[0:1, :])
        g_all.append(pexp / jnp.sum(pexp, axis=0, keepdims=True))
        e_all.append(jnp.concatenate(be, axis=0))
    e_ref[...] = jnp.concatenate(e_all, axis=0).T.astype(jnp.int32)
    g_ref[...] = jnp.concatenate(g_all, axis=0).T


def _peer_topk(q, sub_keys_bf):
    T, QW = q.shape
    tb = PEER_TOPK_TB
    return pl.pallas_call(
        _topk_kernel,
        grid=(T // tb,),
        in_specs=[pl.BlockSpec((tb, QW), lambda i: (i, 0)),
                  pl.BlockSpec((2, PEER_N_KEYS, PEER_N_KEYS), lambda i: (0, 0, 0))],
        out_specs=[pl.BlockSpec((tb, PEER_HK), lambda i: (i, 0)),
                   pl.BlockSpec((tb, PEER_HK), lambda i: (i, 0))],
        out_shape=[jax.ShapeDtypeStruct((T, PEER_HK), jnp.int32),
                   jax.ShapeDtypeStruct((T, PEER_HK), F32)],
        compiler_params=_cparams(("arbitrary",)),
        name="peer_topk",
    )(q, sub_keys_bf)


def _pack_tables(u_tab, v_tab):
    ub = lax.bitcast_convert_type(u_tab.astype(BF16), jnp.uint16).astype(jnp.uint32)
    vb = lax.bitcast_convert_type(v_tab.astype(BF16), jnp.uint16).astype(jnp.uint32)
    packed = lax.bitcast_convert_type(ub | (vb << 16), jnp.int32)
    return jnp.broadcast_to(packed[:, None, :], (packed.shape[0], V7X_SUBLANES, packed.shape[1]))


def _peer_kernel(ec_ref, en_ref, h_ref, gate_ref, x_ref, g2_ref, tab_hbm, o_ref,
                 buf0, buf1, acc_ref, cmat_ref, sem):
    i = pl.program_id(0)
    _peer_half(i, 0, buf0, sem.at[0], buf1, sem.at[1], ec_ref, PEER_TB, ec_ref, h_ref, gate_ref,
               tab_hbm, o_ref, acc_ref, cmat_ref)
    _peer_half(i, PEER_TB, buf1, sem.at[1], buf0, sem.at[0], en_ref, 0, ec_ref, h_ref, gate_ref,
               tab_hbm, o_ref, acc_ref, cmat_ref)
    o_ref[...] = x_ref[...] + g2_ref[0] * o_ref[...]


def _peer_half(i, row0, buf, csem, nbuf, nsem, nidx_ref, nrow0, ec_ref, h_ref, gate_ref, tab_hbm, o_ref,
               acc_ref, cmat_ref):
    D = h_ref.shape[1]
    nch = D // V7X_LANES
    half_ch = nch // 2
    G = PEER_GROUPS
    cur = 0
    nxt = 0
    zero = jnp.zeros((V7X_SUBLANES, V7X_LANES), F32)

    def start_group(idx_ref, r, g, slot, dst=nbuf, dsem=nsem):
        for s in range(V7X_SUBLANES):
            e = idx_ref[r * PEER_HK + g * V7X_SUBLANES + s]
            pltpu.make_async_copy(tab_hbm.at[e, pl.ds(0, 1), :], dst.at[slot, g, pl.ds(s, 1), :],
                                  dsem.at[slot]).start()

    def wait_slot(slot, b=buf, bsem=csem):
        pltpu.make_async_copy(b.at[slot], b.at[slot], bsem.at[slot]).wait()

    if row0 == 0:
        @pl.when(i == 0)
        def _():
            for tt in range(PEER_TB):
                def body(g, carry, tt=tt):
                    start_group(ec_ref, tt, g, tt, dst=buf, dsem=csem)
                    return carry
                lax.fori_loop(0, G, body, 0)

    for tt in range(PEER_TB):
        wait_slot(cur + tt)

        def s1_body(j, carry, tt=tt):
            parts = [[None] * PEER_S1_CHAINS for _ in range(PEER_S1_GROUPS)]
            for c in range(nch):
                cs = slice(c * V7X_LANES, (c + 1) * V7X_LANES)
                hb = jnp.broadcast_to(h_ref[row0 + tt:row0 + tt + 1, cs], (V7X_SUBLANES, V7X_LANES))
                for q in range(PEER_S1_GROUPS):
                    w = buf[cur + tt, j * PEER_S1_GROUPS + q, :, cs]
                    t = lax.bitcast_convert_type(jnp.left_shift(w, 16), F32) * hb
                    p = parts[q][c % PEER_S1_CHAINS]
                    parts[q][c % PEER_S1_CHAINS] = t if p is None else p + t
            for q in range(PEER_S1_GROUPS):
                r0 = pl.multiple_of((tt * G + j * PEER_S1_GROUPS + q) * V7X_SUBLANES, V7X_SUBLANES)
                acc_ref[pl.ds(r0, V7X_SUBLANES), :] = (parts[q][0] + parts[q][1]) + (parts[q][2] + parts[q][3])
            start_group(nidx_ref, nrow0 + tt, j, nxt + tt)
            return carry

        lax.fori_loop(0, G // PEER_S1_GROUPS, s1_body, 0)

    ones_row = jnp.ones((V7X_SUBLANES, V7X_LANES), BF16)
    ones_sq = jnp.ones((V7X_LANES, V7X_LANES), BF16)
    nt_dims = (((1,), (1,)), ((), ()))
    acc = acc_ref[...]
    acc_hi = acc.astype(BF16)
    acc_lo = (acc - acc_hi.astype(F32)).astype(BF16)
    a8 = (lax.dot_general(ones_row, acc_hi, nt_dims, preferred_element_type=F32)
          + lax.dot_general(ones_row, acc_lo, nt_dims, preferred_element_type=F32))
    a_tok = jnp.concatenate([a8[0:1, tt * PEER_HK:(tt + 1) * PEER_HK] for tt in range(PEER_TB)], axis=0)
    c_tok = gate_ref[row0:row0 + PEER_TB, :] * _gelu(a_tok)
    r_i = lax.broadcasted_iota(jnp.int32, (PEER_HK, V7X_LANES), 0)
    c_i = lax.broadcasted_iota(jnp.int32, (PEER_HK, V7X_LANES), 1)
    eye = r_i == c_i
    for tt in range(PEER_TB):
        c_diag = jnp.where(eye, jnp.broadcast_to(c_tok[tt:tt + 1, :], (PEER_HK, V7X_LANES)), 0.0)
        d_hi = c_diag.astype(BF16)
        d_lo = (c_diag - d_hi.astype(F32)).astype(BF16)
        cmat_ref[tt * PEER_HK:(tt + 1) * PEER_HK, :] = (
            jnp.dot(d_hi, ones_sq, preferred_element_type=F32) + jnp.dot(d_lo, ones_sq, preferred_element_type=F32))

    for tt in range(PEER_TB):
        for half in range(2):
            def s2_body(j, accs, tt=tt, half=half):
                new = list(accs)
                for q in range(PEER_S2_GROUPS):
                    g = j * PEER_S2_GROUPS + q
                    r0 = pl.multiple_of((tt * G + g) * V7X_SUBLANES, V7X_SUBLANES)
                    cm = cmat_ref[pl.ds(r0, V7X_SUBLANES), :]
                    for cc in range(half_ch):
                        c = half * half_ch + cc
                        w = buf[cur + tt, g, :, c * V7X_LANES:(c + 1) * V7X_LANES]
                        v = lax.bitcast_convert_type(jnp.bitwise_and(w, jnp.int32(-65536)), F32)
                        new[cc] = new[cc] + cm * v
                start_group(nidx_ref, nrow0 + tt, G // 2 + half * (G // 4) + j, nxt + tt)
                return tuple(new)

            accs = lax.fori_loop(0, G // PEER_S2_GROUPS, s2_body, tuple(zero for _ in range(half_ch)))
            for cc in range(half_ch):
                c = half * half_ch + cc
                o_ref[row0 + tt:row0 + tt + 1, c * V7X_LANES:(c + 1) * V7X_LANES] = jnp.sum(
                    accs[cc], axis=0, keepdims=True)

    if row0 != 0:
        @pl.when(i == pl.num_programs(0) - 1)
        def _():
            for tt in range(PEER_TB):
                wait_slot(tt, b=nbuf, bsem=nsem)


def _peer_gather(experts, gates, h32, x2, g2, uv_tab, seq):
    T, D = x2.shape
    B = g2.shape[0]
    tb = 2 * PEER_TB
    per_b = seq // tb
    nsteps = T // tb
    G = PEER_GROUPS
    assert G // PEER_S1_GROUPS == G // 2 and G // PEER_S2_GROUPS == G // 4
    row = pl.BlockSpec((tb, D), lambda i: (i, 0))
    eflat = experts.reshape(T * PEER_HK)
    return pl.pallas_call(
        _peer_kernel,
        grid=(nsteps,),
        in_specs=[pl.BlockSpec((tb * PEER_HK,), lambda i: (i,), memory_space=pltpu.SMEM),
                  pl.BlockSpec((tb * PEER_HK,), lambda i: (jnp.minimum(i + 1, nsteps - 1),),
                               memory_space=pltpu.SMEM),
                  row,
                  pl.BlockSpec((tb, PEER_HK), lambda i: (i, 0)),
                  row,
                  pl.BlockSpec((1, 1, D), lambda i: (i // per_b, 0, 0)),
                  pl.BlockSpec(memory_space=pl.ANY)],
        out_specs=row,
        out_shape=jax.ShapeDtypeStruct((T, D), F32),
        scratch_shapes=[pltpu.VMEM((PEER_TB, G, V7X_SUBLANES, D), jnp.int32),
                        pltpu.VMEM((PEER_TB, G, V7X_SUBLANES, D), jnp.int32),
                        pltpu.VMEM((PEER_TB * PEER_HK, V7X_LANES), F32),
                        pltpu.VMEM((PEER_TB * PEER_HK, V7X_LANES), F32),
                        pltpu.SemaphoreType.DMA((2, PEER_TB))],
        compiler_params=_cparams(("arbitrary",)),
        name="peer_gather",
    )(eflat, eflat, h32, gates, x2, g2.reshape(B, 1, D), uv_tab)


def _layer(x2, mod, cos_r, sin_r, cos_a, sin_a, batch, seq, norm1_g, norm2_g, w_in, w_out, out_norm_g,
           ssm_a_re, ssm_a_im, ssm_log_dt, ssm_b_re, ssm_b_im, ssm_c_re, ssm_c_im,
           ssm_d, ssm_w_glu, ssm_b_glu, attn_q_norm, attn_k_norm, attn_sinks,
           pool_w, pool_scale, peer_w_query, peer_sub_keys, peer_u, peer_v):
    D = D_MODEL
    sh1, sc1, g1, sh2, sc2, g2 = [mod[:batch, k * D:(k + 1) * D] for k in range(6)]

    (h1,) = _norm_mod(x2, norm1_g, sc1, sh1, seq, (BF16,))
    proj = _matmul(h1, w_in.astype(BF16), tm=1024, tn=640)

    ret = _retention(proj, cos_r, sin_r, batch, seq)
    ssm_params = _ssm_params(ssm_a_re, ssm_a_im, ssm_log_dt, ssm_b_re, ssm_b_im, ssm_c_re, ssm_c_im)
    ssm = _glu(_ssm_scan(proj, ssm_params, ssm_d, batch, seq), ssm_w_glu.astype(BF16), ssm_b_glu)
    swa = _swa(proj, cos_a, sin_a, attn_q_norm, attn_k_norm, attn_sinks, batch, seq)
    pool = _pool(proj, pool_w, pool_scale, batch, seq)
    x2 = _out_proj((ret, ssm, swa, pool), out_norm_g, w_out.astype(BF16), x2, g1, seq)

    h2b, h2f = _norm_mod(x2, norm2_g, sc2, sh2, seq, (BF16, F32))
    q = _matmul(h2b, peer_w_query.astype(BF16), tm=1024, tn=512)
    experts, gates = _peer_topk(q, peer_sub_keys.astype(BF16))
    return _peer_gather(experts, gates, h2f, x2, g2, _pack_tables(peer_u, peer_v), seq)


def kernel(x, c, positions, ada_w, ada_b, norm1_g, norm2_g, w_in, w_out, out_norm_g, ssm_a_re, ssm_a_im, ssm_log_dt, ssm_b_re, ssm_b_im, ssm_c_re, ssm_c_im, ssm_d, ssm_w_glu, ssm_b_glu, attn_q_norm, attn_k_norm, attn_sinks, pool_w, pool_scale, peer_w_query, peer_sub_keys, peer_u, peer_v):
    B, S, D = x.shape
    depth = ada_w.shape[0]
    c8 = jnp.concatenate([c, jnp.zeros((V7X_SUBLANES - B, D), c.dtype)], axis=0)
    mod = _ada_mod(c8, ada_w, ada_b)
    cos_r, sin_r = _rope_tables(positions, RET_DK)
    cos_a, sin_a = _rope_tables(positions, SWA_HEAD_DIM)
    x2 = x.reshape(B * S, D)
    for i in range(depth):
        x2 = _layer(x2, mod[i], cos_r, sin_r, cos_a, sin_a, B, S, norm1_g[i], norm2_g[i], w_in[i], w_out[i],
                    out_norm_g[i], ssm_a_re[i], ssm_a_im[i], ssm_log_dt[i], ssm_b_re[i], ssm_b_im[i],
                    ssm_c_re[i], ssm_c_im[i], ssm_d[i], ssm_w_glu[i], ssm_b_glu[i],
                    attn_q_norm[i], attn_k_norm[i], attn_sinks[i], pool_w[i], pool_scale[i],
                    peer_w_query[i], peer_sub_keys[i], peer_u[i], peer_v[i])
    return x2.reshape(B, S, D)
```

```python
import functools
import math

import numpy as np
import jax
import jax.numpy as jnp
from jax import lax
from jax.experimental import pallas as pl
from jax.experimental.pallas import tpu as pltpu

F32 = jnp.float32
BF16 = jnp.bfloat16
HIGHEST = lax.Precision.HIGHEST

V7X_LANES = 128
V7X_SUBLANES = 8
V7X_VMEM_BYTES = 64 * 1024 * 1024
VMEM_LIMIT = 52 * 1024 * 1024

D_MODEL = 4096
EPS = 1e-6
ROPE_THETA = 10000.0

RET_HEADS = 4
RET_DV = 256
RET_DK = 128
RET_CHUNK = 128

SSM_WIDTH = 1024
SSM_GROUP = 16
SSM_GROUPS = 64
SSM_STATE = 64
SSM_COLS = 128
SSM_CH = SSM_COLS // SSM_GROUP * SSM_STATE
SSM_CHUNK = 256

SWA_HEAD_DIM = 64
SWA_Q_HEADS = 16
SWA_KV_HEADS = 2
SWA_WINDOW = 128

POOL_WINDOWS = (2, 4, 8, 16)
POOL_GROUP = 256
POOL_MAXW = 16

OFF_QR, OFF_KR, OFF_VR, OFF_GR, OFF_US, OFF_QA, OFF_KA, OFF_VA, OFF_UP = (
    0, 512, 1024, 2048, 3072, 4096, 5120, 5248, 5376)
IN_COLS = 6400

PEER_N_KEYS = 128
PEER_HEADS = 8
PEER_TOPK = 16
PEER_HK = PEER_HEADS * PEER_TOPK
PEER_TB = 8
PEER_GROUPS = PEER_HK // V7X_SUBLANES
PEER_S1_GROUPS = 2
PEER_S1_CHAINS = 4
PEER_S2_GROUPS = 4


def _cparams(sem):
    return pltpu.CompilerParams(dimension_semantics=sem, vmem_limit_bytes=VMEM_LIMIT)


def _gelu(x):
    return 0.5 * x * (1.0 + lax.erf(x * math.sqrt(0.5)))


def _sigmoid(x):
    return 1.0 / (1.0 + jnp.exp(-x))


def _ada_kernel(c_ref, w_ref, b_ref, o_ref):
    c = c_ref[...]
    a = c * _sigmoid(c)
    o_ref[0] = jnp.dot(a, w_ref[0], preferred_element_type=F32, precision=HIGHEST) + b_ref[0]


def _ada_mod(c8, ada_w, ada_b):
    L, D, N = ada_w.shape
    tn = 1024
    return pl.pallas_call(
        _ada_kernel,
        grid=(L, N // tn),
        in_specs=[pl.BlockSpec((8, D), lambda l, j: (0, 0)),
                  pl.BlockSpec((1, D, tn), lambda l, j: (l, 0, j)),
                  pl.BlockSpec((1, 1, tn), lambda l, j: (l, 0, j))],
        out_specs=pl.BlockSpec((1, 8, tn), lambda l, j: (l, 0, j)),
        out_shape=jax.ShapeDtypeStruct((L, 8, N), F32),
        compiler_params=_cparams(("arbitrary", "arbitrary")),
        name="ada_mod",
    )(c8, ada_w, ada_b.reshape(L, 1, N))


def _norm_mod_kernel(x_ref, g_ref, sc_ref, sh_ref, *o_refs):
    x = x_ref[...]
    ms = jnp.mean(x * x, axis=-1, keepdims=True)
    y = x * lax.rsqrt(ms + EPS) * g_ref[...]
    h = y * (1.0 + sc_ref[0]) + sh_ref[0]
    for o in o_refs:
        o[...] = h.astype(o.dtype)


def _norm_mod(x2, gain, sc, sh, seq, out_dtypes):
    T, D = x2.shape
    B = sc.shape[0]
    tm = 256
    per_b = seq // tm
    row = pl.BlockSpec((tm, D), lambda i: (i, 0))
    bspec = pl.BlockSpec((1, 1, D), lambda i: (i // per_b, 0, 0))
    outs = pl.pallas_call(
        _norm_mod_kernel,
        grid=(T // tm,),
        in_specs=[row, pl.BlockSpec((1, D), lambda i: (0, 0)), bspec, bspec],
        out_specs=[row for _ in out_dtypes],
        out_shape=[jax.ShapeDtypeStruct((T, D), dt) for dt in out_dtypes],
        compiler_params=_cparams(("arbitrary",)),
        name="norm_mod",
    )(x2, gain.reshape(1, D), sc.reshape(B, 1, D), sh.reshape(B, 1, D))
    return outs


def _mm_kernel(a_ref, w_ref, o_ref):
    o_ref[...] = jnp.dot(a_ref[...], w_ref[...], preferred_element_type=F32)


def _matmul(a, w, tm=512, tn=512):
    M, K = a.shape
    N = w.shape[1]
    return pl.pallas_call(
        _mm_kernel,
        grid=(M // tm, N // tn),
        in_specs=[pl.BlockSpec((tm, K), lambda i, j: (i, 0)),
                  pl.BlockSpec((K, tn), lambda i, j: (0, j))],
        out_specs=pl.BlockSpec((tm, tn), lambda i, j: (i, j)),
        out_shape=jax.ShapeDtypeStruct((M, N), F32),
        compiler_params=_cparams(("arbitrary", "arbitrary")),
        name="matmul",
    )(a, w)


def _cast_kernel(x_ref, o_ref):
    o_ref[...] = x_ref[...].astype(o_ref.dtype)


def _to_bf16(w, rows=256):
    K, N = w.shape
    rows = min(rows, K)
    return pl.pallas_call(
        _cast_kernel,
        grid=(K // rows,),
        in_specs=[pl.BlockSpec((rows, N), lambda i: (i, 0))],
        out_specs=pl.BlockSpec((rows, N), lambda i: (i, 0)),
        out_shape=jax.ShapeDtypeStruct((K, N), BF16),
        compiler_params=_cparams(("arbitrary",)),
        name="cast_bf16",
    )(w)


def _rope_tables(positions, d):
    half = d // 2
    inv = ROPE_THETA ** (-jnp.arange(half, dtype=F32) * 2.0 / d)
    ang = positions.astype(F32).reshape(-1, 1) * inv
    cos, sin = jnp.cos(ang), jnp.sin(ang)
    reps = V7X_LANES // d
    cosf = jnp.concatenate([cos, cos] * reps, axis=-1)
    sinf = jnp.concatenate([-sin, sin] * reps, axis=-1)
    return cosf, sinf


def _ret_consts():
    L, H = RET_CHUNK, RET_HEADS
    log_g = np.log(1.0 - 2.0 ** (-5.0 - np.arange(H, dtype=np.float64)))
    idx = np.arange(L, dtype=np.float64)
    diff = idx[:, None] - idx[None, :]
    decay = np.where(diff >= 0, np.exp(np.maximum(diff, 0.0)[None] * log_g[:, None, None]), 0.0)
    w_k = np.exp((L - 1 - idx)[None, :] * log_g[:, None])
    w_q = np.exp((idx + 1)[None, :] * log_g[:, None])
    gam = np.exp(L * log_g)
    wk_full = np.broadcast_to(w_k[:, :, None], (H, L, RET_DK))
    wq_full = np.broadcast_to(w_q[:, :, None], (H, L, RET_DK))
    gam_full = np.broadcast_to(gam[:, None, None], (H, 1, RET_DV))
    return (jnp.asarray(decay, F32), jnp.asarray(wq_full, F32), jnp.asarray(wk_full, F32),
            jnp.asarray(gam_full, F32))


def _ret_kernel(q_ref, k_ref, v_ref, g_ref, cos_ref, sin_ref, dec_ref, wq_ref, wk_ref, gam_ref,
                o_ref, r_ref):
    n = pl.program_id(1)

    @pl.when(n == 0)
    def _():
        r_ref[...] = jnp.zeros_like(r_ref)

    cos = cos_ref[...]
    sin = sin_ref[...]
    for h in range(RET_HEADS):
        ks = slice(h * RET_DK, (h + 1) * RET_DK)
        vs = slice(h * RET_DV, (h + 1) * RET_DV)
        q = q_ref[:, ks]
        k = k_ref[:, ks]
        q = q * cos + pltpu.roll(q, RET_DK // 2, 1) * sin
        k = (k * cos + pltpu.roll(k, RET_DK // 2, 1) * sin) * (RET_DK ** -0.5)
        vb = v_ref[:, vs].astype(BF16)
        s = lax.dot_general(q.astype(BF16), k.astype(BF16), (((1,), (1,)), ((), ())),
                            preferred_element_type=F32) * dec_ref[h]
        o = jnp.dot(s.astype(BF16), vb, preferred_element_type=F32)
        r = r_ref[h]
        o = o + jnp.dot((q * wq_ref[h]).astype(BF16), r.astype(BF16), preferred_element_type=F32)
        kv = lax.dot_general((k * wk_ref[h]).astype(BF16), vb, (((0,), (0,)), ((), ())),
                             preferred_element_type=F32)
        r_ref[h] = gam_ref[h] * r + kv
        mu = jnp.mean(o, axis=-1, keepdims=True)
        oc = o - mu
        var = jnp.mean(oc * oc, axis=-1, keepdims=True)
        g = g_ref[:, vs]
        o_ref[:, vs] = g * _sigmoid(g) * (oc * lax.rsqrt(var + EPS))


def _retention(proj, cosf, sinf, batch, seq):
    T = proj.shape[0]
    L = RET_CHUNK
    n = seq // L
    H = RET_HEADS
    dec, wq, wk, gam = _ret_consts()
    kw, vw = H * RET_DK, H * RET_DV
    const3 = lambda b, c: (0, 0, 0)
    return pl.pallas_call(
        _ret_kernel,
        grid=(batch, n),
        in_specs=[
            pl.BlockSpec((L, kw), lambda b, c: (b * n + c, OFF_QR // kw)),
            pl.BlockSpec((L, kw), lambda b, c: (b * n + c, OFF_KR // kw)),
            pl.BlockSpec((L, vw), lambda b, c: (b * n + c, OFF_VR // vw)),
            pl.BlockSpec((L, vw), lambda b, c: (b * n + c, OFF_GR // vw)),
            pl.BlockSpec((L, RET_DK), lambda b, c: (b * n + c, 0)),
            pl.BlockSpec((L, RET_DK), lambda b, c: (b * n + c, 0)),
            pl.BlockSpec((H, L, L), const3),
            pl.BlockSpec((H, L, RET_DK), const3),
            pl.BlockSpec((H, L, RET_DK), const3),
            pl.BlockSpec((H, 1, RET_DV), const3),
        ],
        out_specs=pl.BlockSpec((L, vw), lambda b, c: (b * n + c, 0)),
        out_shape=jax.ShapeDtypeStruct((T, vw), F32),
        scratch_shapes=[pltpu.VMEM((H, RET_DK, RET_DV), F32)],
        compiler_params=_cparams(("arbitrary", "arbitrary")),
        name="retention",
    )(proj, proj, proj, proj, cosf, sinf, dec, wq, wk, gam)


def _ssm_params(a_re, a_im, log_dt, b_re, b_im, c_re, c_im):
    dt = jnp.exp(log_dt)[:, None]
    mag = jnp.exp(a_re * dt)
    ab_re = mag * jnp.cos(a_im * dt)
    ab_im = mag * jnp.sin(a_im * dt)
    den = a_re * a_re + a_im * a_im
    nr = ab_re - 1.0
    f_re = (nr * a_re + ab_im * a_im) / den
    f_im = (ab_im * a_re - nr * a_im) / den
    bb_re = f_re[..., None] * b_re - f_im[..., None] * b_im
    bb_im = f_re[..., None] * b_im + f_im[..., None] * b_re
    nblk = SSM_WIDTH // SSM_COLS
    gpb = SSM_COLS // SSM_GROUP
    eye = jnp.eye(gpb, dtype=F32)

    def blockdiag_in(bb):
        t = bb.reshape(nblk, gpb, SSM_STATE, SSM_GROUP)
        m = jnp.einsum('ngpc,gh->ngchp', t, eye)
        return m.reshape(nblk, SSM_COLS, SSM_CH)

    def blockdiag_out(cc):
        t = cc.reshape(nblk, gpb, SSM_GROUP, SSM_STATE)
        m = jnp.einsum('ngcp,gh->ngphc', t, eye)
        return m.reshape(nblk, SSM_CH, SSM_COLS)

    rounds = int(math.log2(SSM_CHUNK))
    pr, pi = [ab_re], [ab_im]
    for _ in range(rounds - 1):
        r, i = pr[-1], pi[-1]
        pr.append(r * r - i * i)
        pi.append(2.0 * r * i)
    apr = jnp.stack(pr, 0).reshape(rounds, nblk, SSM_CH).transpose(1, 0, 2)
    api = jnp.stack(pi, 0).reshape(rounds, nblk, SSM_CH).transpose(1, 0, 2)
    qr, qi = [ab_re], [ab_im]
    for _ in range(V7X_SUBLANES - 1):
        r, i = qr[-1], qi[-1]
        qr.append(r * ab_re - i * ab_im)
        qi.append(r * ab_im + i * ab_re)
    ppr = jnp.stack(qr, 0).reshape(V7X_SUBLANES, nblk, SSM_CH).transpose(1, 0, 2)
    ppi = jnp.stack(qi, 0).reshape(V7X_SUBLANES, nblk, SSM_CH).transpose(1, 0, 2)
    return (blockdiag_in(bb_re).astype(BF16), blockdiag_in(bb_im).astype(BF16),
            blockdiag_out(c_re).astype(BF16), blockdiag_out(c_im).astype(BF16), apr, api, ppr, ppi)


def _cmul_add(xr, xi, p_r, p_i, sr, si):
    return xr + p_r * sr - p_i * si, xi + p_r * si + p_i * sr


def _ssm_kernel(u_ref, bre_ref, bim_ref, cre_ref, cim_ref, apr_ref, api_ref, ppr_ref, ppi_ref, d_ref, o_ref,
                cr_ref, ci_ref, xr_s, xi_s, hr_s, hi_s):
    n = pl.program_id(2)
    hs = V7X_SUBLANES
    ng = SSM_CHUNK // hs

    @pl.when(n == 0)
    def _():
        cr_ref[...] = jnp.zeros_like(cr_ref)
        ci_ref[...] = jnp.zeros_like(ci_ref)

    u = u_ref[...]
    ub = u.astype(BF16)
    xr = jnp.dot(ub, bre_ref[0], preferred_element_type=F32)
    xi = jnp.dot(ub, bim_ref[0], preferred_element_type=F32)
    sub = jnp.bitwise_and(lax.broadcasted_iota(jnp.int32, xr.shape, 0), hs - 1)
    in_rounds = int(math.log2(hs))
    for k in range(in_rounds):
        s = 1 << k
        keep = sub >= s
        sr = jnp.where(keep, pltpu.roll(xr, s, 0), 0.0)
        si = jnp.where(keep, pltpu.roll(xi, s, 0), 0.0)
        xr, xi = _cmul_add(xr, xi, apr_ref[0, k:k + 1, :], api_ref[0, k:k + 1, :], sr, si)
    nlt = SSM_CH // V7X_LANES
    for q in range(nlt):
        xr_s[q] = xr[:, q * V7X_LANES:(q + 1) * V7X_LANES]
        xi_s[q] = xi[:, q * V7X_LANES:(q + 1) * V7X_LANES]

    er = jnp.concatenate([xr_s[q, pl.ds(hs - 1, ng, stride=hs), :] for q in range(nlt)], axis=1)
    ei = jnp.concatenate([xi_s[q, pl.ds(hs - 1, ng, stride=hs), :] for q in range(nlt)], axis=1)
    c_r = cr_ref[...]
    c_i = ci_ref[...]
    grow = lax.broadcasted_iota(jnp.int32, er.shape, 0)
    first = grow == 0
    a8r = apr_ref[0, in_rounds:in_rounds + 1, :]
    a8i = api_ref[0, in_rounds:in_rounds + 1, :]
    er = er + jnp.where(first, a8r * c_r - a8i * c_i, 0.0)
    ei = ei + jnp.where(first, a8r * c_i + a8i * c_r, 0.0)
    for k in range(int(math.log2(ng))):
        s = 1 << k
        keep = grow >= s
        sr = jnp.where(keep, pltpu.roll(er, s, 0), 0.0)
        si = jnp.where(keep, pltpu.roll(ei, s, 0), 0.0)
        kk = in_rounds + k
        er, ei = _cmul_add(er, ei, apr_ref[0, kk:kk + 1, :], api_ref[0, kk:kk + 1, :], sr, si)
    cr_ref[...] = er[ng - 1:ng, :]
    ci_ref[...] = ei[ng - 1:ng, :]
    hr_s[...] = jnp.where(first, c_r, pltpu.roll(er, 1, 0))
    hi_s[...] = jnp.where(first, c_i, pltpu.roll(ei, 1, 0))
    p_r = ppr_ref[0]
    p_i = ppi_ref[0]
    for g in range(ng):
        rows = slice(g * hs, (g + 1) * hs)
        for q in range(nlt):
            ls = slice(q * V7X_LANES, (q + 1) * V7X_LANES)
            hb_r = jnp.broadcast_to(hr_s[g:g + 1, ls], (hs, V7X_LANES))
            hb_i = jnp.broadcast_to(hi_s[g:g + 1, ls], (hs, V7X_LANES))
            yr, yi = _cmul_add(xr_s[q, rows, :], xi_s[q, rows, :], p_r[:, ls], p_i[:, ls], hb_r, hb_i)
            xr_s[q, rows, :] = yr
            xi_s[q, rows, :] = yi
    hr = jnp.concatenate([xr_s[q] for q in range(nlt)], axis=1)
    hi = jnp.concatenate([xi_s[q] for q in range(nlt)], axis=1)
    y = (jnp.dot(hr.astype(BF16), cre_ref[0], preferred_element_type=F32)
         - jnp.dot(hi.astype(BF16), cim_ref[0], preferred_element_type=F32))
    o_ref[...] = _gelu(y + d_ref[...] * u)


def _ssm_scan(proj, params, d, batch, seq):
    T = proj.shape[0]
    bre, bim, cre, cim, apr, api, ppr, ppi = params
    nblk = SSM_WIDTH // SSM_COLS
    nt = seq // SSM_CHUNK
    rounds = apr.shape[1]
    ngroups = SSM_CHUNK // V7X_SUBLANES
    wspec_in = pl.BlockSpec((1, SSM_COLS, SSM_CH), lambda b, c, n: (c, 0, 0))
    wspec_out = pl.BlockSpec((1, SSM_CH, SSM_COLS), lambda b, c, n: (c, 0, 0))
    pspec = pl.BlockSpec((1, rounds, SSM_CH), lambda b, c, n: (c, 0, 0))
    qspec = pl.BlockSpec((1, V7X_SUBLANES, SSM_CH), lambda b, c, n: (c, 0, 0))
    return pl.pallas_call(
        _ssm_kernel,
        grid=(batch, nblk, nt),
        in_specs=[pl.BlockSpec((SSM_CHUNK, SSM_COLS), lambda b, c, n: (b * nt + n, OFF_US // SSM_COLS + c)),
                  wspec_in, wspec_in, wspec_out, wspec_out, pspec, pspec, qspec, qspec,
                  pl.BlockSpec((1, SSM_COLS), lambda b, c, n: (0, c))],
        out_specs=pl.BlockSpec((SSM_CHUNK, SSM_COLS), lambda b, c, n: (b * nt + n, c)),
        out_shape=jax.ShapeDtypeStruct((T, SSM_WIDTH), F32),
        scratch_shapes=[pltpu.VMEM((1, SSM_CH), F32), pltpu.VMEM((1, SSM_CH), F32),
                        pltpu.VMEM((SSM_CH // V7X_LANES, SSM_CHUNK, V7X_LANES), F32),
                        pltpu.VMEM((SSM_CH // V7X_LANES, SSM_CHUNK, V7X_LANES), F32),
                        pltpu.VMEM((ngroups, SSM_CH), F32), pltpu.VMEM((ngroups, SSM_CH), F32)],
        compiler_params=_cparams(("arbitrary", "arbitrary", "arbitrary")),
        name="ssm_scan",
    )(proj, bre, bim, cre, cim, apr, api, ppr, ppi, d.reshape(1, SSM_WIDTH))


def _glu_kernel(y_ref, w_ref, b_ref, o_ref):
    y = y_ref[...]
    z = jnp.dot(y.astype(BF16), w_ref[...], preferred_element_type=F32) + b_ref[...]
    o_ref[...] = y * _sigmoid(z)


def _glu(y, w, b):
    T, W = y.shape
    tm = 512
    return pl.pallas_call(
        _glu_kernel,
        grid=(T // tm,),
        in_specs=[pl.BlockSpec((tm, W), lambda i: (i, 0)),
                  pl.BlockSpec((W, W), lambda i: (0, 0)),
                  pl.BlockSpec((1, W), lambda i: (0, 0))],
        out_specs=pl.BlockSpec((tm, W), lambda i: (i, 0)),
        out_shape=jax.ShapeDtypeStruct((T, W), F32),
        compiler_params=_cparams(("arbitrary",)),
        name="ssm_glu",
    )(y, w, b.reshape(1, W))


def _swa_kernel(sink_ref, q_ref, k_ref, v_ref, cos_ref, sin_ref, qn_ref, kn_ref, o_ref, pk_ref, pv_ref):
    n = pl.program_id(1)
    W = SWA_WINDOW
    hd = SWA_HEAD_DIM

    @pl.when(n == 0)
    def _():
        pk_ref[...] = jnp.zeros_like(pk_ref)
        pv_ref[...] = jnp.zeros_like(pv_ref)

    cos = cos_ref[...]
    sin = sin_ref[...]
    lane = lax.broadcasted_iota(jnp.int32, (W, V7X_LANES), 1)
    lo = lane < hd
    first_half = (lane % hd) < (hd // 2)

    def head_norm(t, gain):
        tt = t * t
        s_all = jnp.sum(tt, axis=-1, keepdims=True)
        s_lo = jnp.sum(jnp.where(lo, tt, 0.0), axis=-1, keepdims=True)
        ms = jnp.where(lo, s_lo, s_all - s_lo) * (1.0 / hd)
        return t * lax.rsqrt(ms + EPS) * gain

    def rope(t):
        sw = jnp.where(first_half, pltpu.roll(t, V7X_LANES - hd // 2, 1), pltpu.roll(t, hd // 2, 1))
        return t * cos + sw * sin

    kc = rope(head_norm(k_ref[...], kn_ref[...]))
    vc = v_ref[...]
    keys = jnp.concatenate([pk_ref[...], kc], axis=0)
    vals = jnp.concatenate([pv_ref[...], vc], axis=0)
    lane2 = lax.broadcasted_iota(jnp.int32, (2 * W, V7X_LANES), 1)
    lo2 = lane2 < hd

    def dup(x, h):
        sw = pltpu.roll(x, hd, 1)
        return (jnp.where(lo2, x, sw) if h == 0 else jnp.where(lo2, sw, x)).astype(BF16)

    kk = [dup(keys, h) for h in range(SWA_KV_HEADS)]
    vv = [dup(vals, h) for h in range(SWA_KV_HEADS)]
    qi = lax.broadcasted_iota(jnp.int32, (W, 2 * W), 0)
    ci = lax.broadcasted_iota(jnp.int32, (W, 2 * W), 1)
    prev_floor = qi + jnp.where(n > 0, 0, 2 * W)
    valid = jnp.logical_or(jnp.logical_and(ci < W, ci > prev_floor),
                           jnp.logical_and(ci >= W, (ci - W) <= qi))
    tiles = SWA_Q_HEADS * hd // V7X_LANES
    per_kv = tiles // SWA_KV_HEADS
    for m in range(tiles):
        sl = slice(m * V7X_LANES, (m + 1) * V7X_LANES)
        qt = rope(head_norm(q_ref[:, sl], qn_ref[:, sl]))
        h = m // per_kv
        outs = []
        for half in range(2):
            sel = lo if half == 0 else jnp.logical_not(lo)
            qm = jnp.where(sel, qt, 0.0).astype(BF16)
            s = lax.dot_general(qm, kk[h], (((1,), (1,)), ((), ())),
                                preferred_element_type=F32) * (hd ** -0.5)
            s = jnp.where(valid, s, -1e30)
            sink = sink_ref[2 * m + half]
            mx = jnp.maximum(jnp.max(s, axis=-1, keepdims=True), sink)
            p = jnp.exp(s - mx)
            den = jnp.sum(p, axis=-1, keepdims=True) + jnp.exp(sink - mx)
            p = p / den
            outs.append(jnp.dot(p.astype(BF16), vv[h], preferred_element_type=F32))
        o_ref[:, sl] = jnp.where(lo, outs[0], outs[1])
    pk_ref[...] = kc
    pv_ref[...] = vc


def _swa(proj, cosf, sinf, q_norm, k_norm, sinks, batch, seq):
    T = proj.shape[0]
    W = SWA_WINDOW
    nb = seq // W
    qw = SWA_Q_HEADS * SWA_HEAD_DIM
    qn = jnp.tile(q_norm, SWA_Q_HEADS).reshape(1, qw)
    kn = jnp.tile(k_norm, SWA_KV_HEADS).reshape(1, V7X_LANES)
    return pl.pallas_call(
        _swa_kernel,
        grid=(batch, nb),
        in_specs=[pl.BlockSpec(memory_space=pltpu.SMEM),
                  pl.BlockSpec((W, qw), lambda b, n: (b * nb + n, OFF_QA // qw)),
                  pl.BlockSpec((W, V7X_LANES), lambda b, n: (b * nb + n, OFF_KA // V7X_LANES)),
                  pl.BlockSpec((W, V7X_LANES), lambda b, n: (b * nb + n, OFF_VA // V7X_LANES)),
                  pl.BlockSpec((W, V7X_LANES), lambda b, n: (b * nb + n, 0)),
                  pl.BlockSpec((W, V7X_LANES), lambda b, n: (b * nb + n, 0)),
                  pl.BlockSpec((1, qw), lambda b, n: (0, 0)),
                  pl.BlockSpec((1, V7X_LANES), lambda b, n: (0, 0))],
        out_specs=pl.BlockSpec((W, qw), lambda b, n: (b * nb + n, 0)),
        out_shape=jax.ShapeDtypeStruct((T, qw), F32),
        scratch_shapes=[pltpu.VMEM((W, V7X_LANES), F32), pltpu.VMEM((W, V7X_LANES), F32)],
        compiler_params=_cparams(("arbitrary", "arbitrary")),
        name="swa",
    )(sinks, proj, proj, proj, cosf, sinf, qn, kn)


POOL_CHUNK = 256


def _pool_kernel(u0_ref, u1_ref, u2_ref, u3_ref, w_ref, sc_ref, o_ref, ext_ref):
    n = pl.program_id(1)
    Lc = POOL_CHUNK

    @pl.when(n == 0)
    def _():
        ext_ref[:, 0:POOL_MAXW, :] = jnp.zeros((len(POOL_WINDOWS), POOL_MAXW, POOL_GROUP), F32)

    t = n * Lc + lax.broadcasted_iota(jnp.int32, (Lc, POOL_GROUP), 0)
    for g, (u_ref, w) in enumerate(zip((u0_ref, u1_ref, u2_ref, u3_ref), POOL_WINDOWS)):
        u = u_ref[...]
        ext_ref[g, POOL_MAXW:POOL_MAXW + Lc, :] = u
        acc = u
        for k in range(1, w):
            acc = acc + ext_ref[g, POOL_MAXW - k:POOL_MAXW - k + Lc, :]
        cnt = jnp.minimum(t + 1, w).astype(F32)
        pooled = acc / cnt - u
        y = jnp.dot(pooled.astype(BF16), w_ref[g], preferred_element_type=F32)
        sl = slice(g * POOL_GROUP, (g + 1) * POOL_GROUP)
        o_ref[:, sl] = y * sc_ref[:, sl]
        ext_ref[g, 0:POOL_MAXW, :] = u[Lc - POOL_MAXW:, :]


def _pool(proj, pool_w, pool_scale, batch, seq):
    T = proj.shape[0]
    Lc = POOL_CHUNK
    nt = seq // Lc
    ng = len(POOL_WINDOWS)
    width = ng * POOL_GROUP
    uspecs = [pl.BlockSpec((Lc, POOL_GROUP), functools.partial(
        lambda b, n, g: (b * nt + n, OFF_UP // POOL_GROUP + g), g=g)) for g in range(ng)]
    return pl.pallas_call(
        _pool_kernel,
        grid=(batch, nt),
        in_specs=uspecs + [pl.BlockSpec((ng, POOL_GROUP, POOL_GROUP), lambda b, n: (0, 0, 0)),
                           pl.BlockSpec((1, width), lambda b, n: (0, 0))],
        out_specs=pl.BlockSpec((Lc, width), lambda b, n: (b * nt + n, 0)),
        out_shape=jax.ShapeDtypeStruct((T, width), F32),
        scratch_shapes=[pltpu.VMEM((ng, POOL_MAXW + Lc, POOL_GROUP), F32)],
        compiler_params=_cparams(("arbitrary", "arbitrary")),
        name="pool",
    )(proj, proj, proj, proj, pool_w.astype(BF16), pool_scale.reshape(1, width))


def _out_kernel(r_ref, s_ref, a_ref, p_ref, gain_ref, w_ref, x_ref, g1_ref, o_ref, mix_ref):
    j = pl.program_id(1)

    @pl.when(j == 0)
    def _():
        for idx, ref in enumerate((r_ref, s_ref, a_ref, p_ref)):
            v = ref[...]
            width = v.shape[1]
            sl = slice(idx * width, (idx + 1) * width)
            ms = jnp.mean(v * v, axis=-1, keepdims=True)
            mix_ref[:, sl] = (v * lax.rsqrt(ms + EPS) * gain_ref[:, sl]).astype(BF16)

    acc = jnp.dot(mix_ref[...], w_ref[...], preferred_element_type=F32)
    o_ref[...] = x_ref[...] + g1_ref[0] * acc


def _out_proj(branches, gain, w_out_bf, x2, g1, seq):
    T, D = x2.shape
    B = g1.shape[0]
    tm, tn = 512, 512
    per_b = seq // tm
    width = branches[0].shape[1]
    bspec = pl.BlockSpec((tm, width), lambda i, j: (i, 0))
    return pl.pallas_call(
        _out_kernel,
        grid=(T // tm, D // tn),
        in_specs=[bspec, bspec, bspec, bspec,
                  pl.BlockSpec((1, D), lambda i, j: (0, 0)),
                  pl.BlockSpec((D, tn), lambda i, j: (0, j)),
                  pl.BlockSpec((tm, tn), lambda i, j: (i, j)),
                  pl.BlockSpec((1, 1, tn), lambda i, j: (i // per_b, 0, j))],
        out_specs=pl.BlockSpec((tm, tn), lambda i, j: (i, j)),
        out_shape=jax.ShapeDtypeStruct((T, D), F32),
        scratch_shapes=[pltpu.VMEM((tm, D), BF16)],
        compiler_params=_cparams(("arbitrary", "arbitrary")),
        name="out_proj",
    )(*branches, gain.reshape(1, D), w_out_bf, x2, g1.reshape(B, 1, D))


PEER_TOPK_TB = 128


def _iter_topk(s, k):
    n = s.shape[0]
    rows = lax.broadcasted_iota(jnp.int32, s.shape, 0).astype(F32)
    vals, idxs = [], []
    for _ in range(k):
        m = jnp.max(s, axis=0, keepdims=True)
        idx = jnp.min(jnp.where(s == m, rows, float(n)), axis=0, keepdims=True)
        s = jnp.where(rows == idx, -jnp.inf, s)
        vals.append(m)
        idxs.append(idx)
    return jnp.concatenate(vals, axis=0), jnp.concatenate(idxs, axis=0)


def _topk_kernel(q_ref, keys_ref, e_ref, g_ref):
    K = PEER_TOPK
    half = PEER_N_KEYS
    e_all, g_all = [], []
    for h in range(PEER_HEADS):
        tops = []
        for p in range(2):
            c0 = h * 2 * half + p * half
            qhp = q_ref[:, c0:c0 + half].astype(BF16)
            s = lax.dot_general(keys_ref[p], qhp, (((1,), (1,)), ((), ())), preferred_element_type=F32)
            tops.append(_iter_topk(s, K))
        (s0, i0), (s1, i1) = tops
        hs = V7X_SUBLANES
        io8 = lax.broadcasted_iota(jnp.int32, (hs, s0.shape[1]), 0).astype(F32)
        ps, pe, pp = [], [], []
        for a, b0 in [(0, 0), (0, hs)] + [(a, 0) for a in range(1, hs)]:
            ps.append(s0[a:a + 1, :] + s1[b0:b0 + hs, :])
            pe.append(i0[a:a + 1, :] * PEER_N_KEYS + i1[b0:b0 + hs, :])
            pp.append(io8 + float(a * K + b0))
        ps.append(s0[hs:K, :] + s1[0:1, :])
        pe.append(i0[hs:K, :] * PEER_N_KEYS + i1[0:1, :])
        pp.append((io8 + float(hs)) * float(K))
        cand_s = jnp.concatenate(ps, axis=0)
        cand_e = jnp.concatenate(pe, axis=0)
        rows = jnp.concatenate(pp, axis=0)
        bs, be = [], []
        for _ in range(K):
            m = jnp.max(cand_s, axis=0, keepdims=True)
            pos = jnp.min(jnp.where(cand_s == m, rows, float(K * K)), axis=0, keepdims=True)
            hit = rows == pos
            be.append(jnp.sum(jnp.where(hit, cand_e, 0.0), axis=0, keepdims=True))
            cand_s = jnp.where(hit, -jnp.inf, cand_s)
            bs.append(m)
        best = jnp.concatenate(bs, axis=0)
        pexp = jnp.exp(best - best[0:1, :])
        g_all.append(pexp / jnp.sum(pexp, axis=0, keepdims=True))
        e_all.append(jnp.concatenate(be, axis=0))
    e_ref[...] = jnp.concatenate(e_all, axis=0).T.astype(jnp.int32)
    g_ref[...] = jnp.concatenate(g_all, axis=0).T


def _peer_topk(q, sub_keys_bf):
    T, QW = q.shape
    tb = PEER_TOPK_TB
    return pl.pallas_call(
        _topk_kernel,
        grid=(T // tb,),
        in_specs=[pl.BlockSpec((tb, QW), lambda i: (i, 0)),
                  pl.BlockSpec((2, PEER_N_KEYS, PEER_N_KEYS), lambda i: (0, 0, 0))],
        out_specs=[pl.BlockSpec((tb, PEER_HK), lambda i: (i, 0)),
                   pl.BlockSpec((tb, PEER_HK), lambda i: (i, 0))],
        out_shape=[jax.ShapeDtypeStruct((T, PEER_HK), jnp.int32),
                   jax.ShapeDtypeStruct((T, PEER_HK), F32)],
        compiler_params=_cparams(("arbitrary",)),
        name="peer_topk",
    )(q, sub_keys_bf)


PEER_PACK_ROWS = 32


def _pack_kernel(u_ref, v_ref, o_ref):
    ub = lax.bitcast_convert_type(u_ref[...].astype(BF16).astype(F32), jnp.int32)
    vb = lax.bitcast_convert_type(v_ref[...].astype(BF16).astype(F32), jnp.int32)
    word = jnp.bitwise_or(lax.shift_right_logical(ub, 16), vb)
    for r in range(PEER_PACK_ROWS):
        o_ref[r] = jnp.broadcast_to(word[r:r + 1, :], o_ref.shape[1:])


def _pack_tables(u_tab, v_tab):
    E, D = u_tab.shape
    R = PEER_PACK_ROWS
    return pl.pallas_call(
        _pack_kernel,
        grid=(E // R,),
        in_specs=[pl.BlockSpec((R, D), lambda i: (i, 0)), pl.BlockSpec((R, D), lambda i: (i, 0))],
        out_specs=pl.BlockSpec((R, V7X_SUBLANES, D), lambda i: (i, 0, 0)),
        out_shape=jax.ShapeDtypeStruct((E, V7X_SUBLANES, D), jnp.int32),
        compiler_params=_cparams(("arbitrary",)),
        name="peer_pack",
    )(u_tab, v_tab)


def _peer_kernel(ec_ref, en_ref, h_ref, gate_ref, x_ref, g2_ref, tab_hbm, o_ref,
                 buf0, buf1, acc_ref, cmat_ref, sem0, sem1):
    i = pl.program_id(0)
    _peer_half(i, 0, buf0, sem0, buf1, sem1, ec_ref, PEER_TB, ec_ref, h_ref, gate_ref,
               tab_hbm, o_ref, acc_ref, cmat_ref)
    _peer_half(i, PEER_TB, buf1, sem1, buf0, sem0, en_ref, 0, ec_ref, h_ref, gate_ref,
               tab_hbm, o_ref, acc_ref, cmat_ref)
    o_ref[...] = x_ref[...] + g2_ref[0] * o_ref[...]


def _peer_half(i, row0, buf, csem, nbuf, nsem, nidx_ref, nrow0, ec_ref, h_ref, gate_ref, tab_hbm, o_ref,
               acc_ref, cmat_ref):
    D = h_ref.shape[1]
    nch = D // V7X_LANES
    half_ch = nch // 2
    G = PEER_GROUPS
    cur = 0
    nxt = 0
    zero = jnp.zeros((V7X_SUBLANES, V7X_LANES), F32)

    def start_group(idx_ref, r, g, slot, dst=nbuf, dsem=nsem):
        for s in range(V7X_SUBLANES):
            e = idx_ref[r * PEER_HK + g * V7X_SUBLANES + s]
            pltpu.make_async_copy(tab_hbm.at[e, pl.ds(0, 1), :], dst.at[slot, g, pl.ds(s, 1), :],
                                  dsem.at[slot]).start()

    def wait_slot(slot, b=buf, bsem=csem):
        pltpu.make_async_copy(b.at[slot], b.at[slot], bsem.at[slot]).wait()

    if row0 == 0:
        @pl.when(i == 0)
        def _():
            for tt in range(PEER_TB):
                def body(g, carry, tt=tt):
                    start_group(ec_ref, tt, g, tt, dst=buf, dsem=csem)
                    return carry
                lax.fori_loop(0, G, body, 0)

    for tt in range(PEER_TB):
        wait_slot(cur + tt)

        def s1_body(j, carry, tt=tt):
            parts = [[None] * PEER_S1_CHAINS for _ in range(PEER_S1_GROUPS)]
            for c in range(nch):
                cs = slice(c * V7X_LANES, (c + 1) * V7X_LANES)
                hb = jnp.broadcast_to(h_ref[row0 + tt:row0 + tt + 1, cs], (V7X_SUBLANES, V7X_LANES))
                for q in range(PEER_S1_GROUPS):
                    w = buf[cur + tt, j * PEER_S1_GROUPS + q, :, cs]
                    t = lax.bitcast_convert_type(jnp.left_shift(w, 16), F32) * hb
                    p = parts[q][c % PEER_S1_CHAINS]
                    parts[q][c % PEER_S1_CHAINS] = t if p is None else p + t
            for q in range(PEER_S1_GROUPS):
                r0 = pl.multiple_of((tt * G + j * PEER_S1_GROUPS + q) * V7X_SUBLANES, V7X_SUBLANES)
                acc_ref[pl.ds(r0, V7X_SUBLANES), :] = (parts[q][0] + parts[q][1]) + (parts[q][2] + parts[q][3])
            start_group(nidx_ref, nrow0 + tt, j, nxt + tt)
            return carry

        lax.fori_loop(0, G // PEER_S1_GROUPS, s1_body, 0)

    ones_row = jnp.ones((V7X_SUBLANES, V7X_LANES), BF16)
    ones_sq = jnp.ones((V7X_LANES, V7X_LANES), BF16)
    nt_dims = (((1,), (1,)), ((), ()))
    acc = acc_ref[...]
    acc_hi = acc.astype(BF16)
    acc_lo = (acc - acc_hi.astype(F32)).astype(BF16)
    a8 = (lax.dot_general(ones_row, acc_hi, nt_dims, preferred_element_type=F32)
          + lax.dot_general(ones_row, acc_lo, nt_dims, preferred_element_type=F32))
    a_tok = jnp.concatenate([a8[0:1, tt * PEER_HK:(tt + 1) * PEER_HK] for tt in range(PEER_TB)], axis=0)
    c_tok = gate_ref[row0:row0 + PEER_TB, :] * _gelu(a_tok)
    r_i = lax.broadcasted_iota(jnp.int32, (PEER_HK, V7X_LANES), 0)
    c_i = lax.broadcasted_iota(jnp.int32, (PEER_HK, V7X_LANES), 1)
    eye = r_i == c_i
    for tt in range(PEER_TB):
        c_diag = jnp.where(eye, jnp.broadcast_to(c_tok[tt:tt + 1, :], (PEER_HK, V7X_LANES)), 0.0)
        d_hi = c_diag.astype(BF16)
        d_lo = (c_diag - d_hi.astype(F32)).astype(BF16)
        cmat_ref[tt * PEER_HK:(tt + 1) * PEER_HK, :] = (
            jnp.dot(d_hi, ones_sq, preferred_element_type=F32) + jnp.dot(d_lo, ones_sq, preferred_element_type=F32))

    for tt in range(PEER_TB):
        for half in range(2):
            def s2_body(j, accs, tt=tt, half=half):
                new = list(accs)
                for q in range(PEER_S2_GROUPS):
                    g = j * PEER_S2_GROUPS + q
                    r0 = pl.multiple_of((tt * G + g) * V7X_SUBLANES, V7X_SUBLANES)
                    cm = cmat_ref[pl.ds(r0, V7X_SUBLANES), :]
                    for cc in range(half_ch):
                        c = half * half_ch + cc
                        w = buf[cur + tt, g, :, c * V7X_LANES:(c + 1) * V7X_LANES]
                        v = lax.bitcast_convert_type(jnp.bitwise_and(w, jnp.int32(-65536)), F32)
                        new[cc] = new[cc] + cm * v
                start_group(nidx_ref, nrow0 + tt, G // 2 + half * (G // 4) + j, nxt + tt)
                return tuple(new)

            accs = lax.fori_loop(0, G // PEER_S2_GROUPS, s2_body, tuple(zero for _ in range(half_ch)))
            for cc in range(half_ch):
                c = half * half_ch + cc
                o_ref[row0 + tt:row0 + tt + 1, c * V7X_LANES:(c + 1) * V7X_LANES] = jnp.sum(
                    accs[cc], axis=0, keepdims=True)

    if row0 != 0:
        @pl.when(i == pl.num_programs(0) - 1)
        def _():
            for tt in range(PEER_TB):
                wait_slot(tt, b=nbuf, bsem=nsem)


def _peer_gather(experts, gates, h32, x2, g2, uv_tab, seq):
    T, D = x2.shape
    B = g2.shape[0]
    tb = 2 * PEER_TB
    per_b = seq // tb
    nsteps = T // tb
    G = PEER_GROUPS
    assert G // PEER_S1_GROUPS == G // 2 and G // PEER_S2_GROUPS == G // 4
    row = pl.BlockSpec((tb, D), lambda i: (i, 0))
    eflat = experts.reshape(T * PEER_HK)
    return pl.pallas_call(
        _peer_kernel,
        grid=(nsteps,),
        in_specs=[pl.BlockSpec((tb * PEER_HK,), lambda i: (i,), memory_space=pltpu.SMEM),
                  pl.BlockSpec((tb * PEER_HK,), lambda i: (jnp.minimum(i + 1, nsteps - 1),),
                               memory_space=pltpu.SMEM),
                  row,
                  pl.BlockSpec((tb, PEER_HK), lambda i: (i, 0)),
                  row,
                  pl.BlockSpec((1, 1, D), lambda i: (i // per_b, 0, 0)),
                  pl.BlockSpec(memory_space=pl.ANY)],
        out_specs=row,
        out_shape=jax.ShapeDtypeStruct((T, D), F32),
        scratch_shapes=[pltpu.VMEM((PEER_TB, G, V7X_SUBLANES, D), jnp.int32),
                        pltpu.VMEM((PEER_TB, G, V7X_SUBLANES, D), jnp.int32),
                        pltpu.VMEM((PEER_TB * PEER_HK, V7X_LANES), F32),
                        pltpu.VMEM((PEER_TB * PEER_HK, V7X_LANES), F32),
                        pltpu.SemaphoreType.DMA((PEER_TB,)),
                        pltpu.SemaphoreType.DMA((PEER_TB,))],
        compiler_params=_cparams(("arbitrary",)),
        name="peer_gather",
    )(eflat, eflat, h32, gates, x2, g2.reshape(B, 1, D), uv_tab)


def _layer(x2, mod, cos_r, sin_r, cos_a, sin_a, batch, seq, norm1_g, norm2_g, w_in, w_out, out_norm_g,
           ssm_a_re, ssm_a_im, ssm_log_dt, ssm_b_re, ssm_b_im, ssm_c_re, ssm_c_im,
           ssm_d, ssm_w_glu, ssm_b_glu, attn_q_norm, attn_k_norm, attn_sinks,
           pool_w, pool_scale, peer_w_query, peer_sub_keys, peer_u, peer_v):
    D = D_MODEL
    sh1, sc1, g1, sh2, sc2, g2 = [mod[:batch, k * D:(k + 1) * D] for k in range(6)]

    (h1,) = _norm_mod(x2, norm1_g, sc1, sh1, seq, (BF16,))
    proj = _matmul(h1, _to_bf16(w_in), tm=1024, tn=640)

    ret = _retention(proj, cos_r, sin_r, batch, seq)
    ssm_params = _ssm_params(ssm_a_re, ssm_a_im, ssm_log_dt, ssm_b_re, ssm_b_im, ssm_c_re, ssm_c_im)
    ssm = _glu(_ssm_scan(proj, ssm_params, ssm_d, batch, seq), _to_bf16(ssm_w_glu), ssm_b_glu)
    swa = _swa(proj, cos_a, sin_a, attn_q_norm, attn_k_norm, attn_sinks, batch, seq)
    pool = _pool(proj, pool_w, pool_scale, batch, seq)
    x2 = _out_proj((ret, ssm, swa, pool), out_norm_g, _to_bf16(w_out), x2, g1, seq)

    h2b, h2f = _norm_mod(x2, norm2_g, sc2, sh2, seq, (BF16, F32))
    q = _matmul(h2b, _to_bf16(peer_w_query), tm=1024, tn=512)
    experts, gates = _peer_topk(q, peer_sub_keys.astype(BF16))
    return _peer_gather(experts, gates, h2f, x2, g2, _pack_tables(peer_u, peer_v), seq)


def kernel(x, c, positions, ada_w, ada_b, norm1_g, norm2_g, w_in, w_out, out_norm_g, ssm_a_re, ssm_a_im, ssm_log_dt, ssm_b_re, ssm_b_im, ssm_c_re, ssm_c_im, ssm_d, ssm_w_glu, ssm_b_glu, attn_q_norm, attn_k_norm, attn_sinks, pool_w, pool_scale, peer_w_query, peer_sub_keys, peer_u, peer_v):
    B, S, D = x.shape
    depth = ada_w.shape[0]
    c8 = jnp.concatenate([c, jnp.zeros((V7X_SUBLANES - B, D), c.dtype)], axis=0)
    mod = _ada_mod(c8, ada_w, ada_b)
    cos_r, sin_r = _rope_tables(positions, RET_DK)
    cos_a, sin_a = _rope_tables(positions, SWA_HEAD_DIM)
    x2 = x.reshape(B * S, D)
    for i in range(depth):
        x2 = _layer(x2, mod[i], cos_r, sin_r, cos_a, sin_a, B, S, norm1_g[i], norm2_g[i], w_in[i], w_out[i],
                    out_norm_g[i], ssm_a_re[i], ssm_a_im[i], ssm_log_dt[i], ssm_b_re[i], ssm_b_im[i],
                    ssm_c_re[i], ssm_c_im[i], ssm_d[i], ssm_w_glu[i], ssm_b_glu[i],
                    attn_q_norm[i], attn_k_norm[i], attn_sinks[i], pool_w[i], pool_scale[i],
                    peer_w_query[i], peer_sub_keys[i], peer_u[i], peer_v[i])
    return x2.reshape(B, S, D)
```

```python
import functools
import math

import numpy as np
import jax
import jax.numpy as jnp
from jax import lax
from jax.experimental import pallas as pl
from jax.experimental.pallas import tpu as pltpu

F32 = jnp.float32
BF16 = jnp.bfloat16
HIGHEST = lax.Precision.HIGHEST

V7X_LANES = 128
V7X_SUBLANES = 8
V7X_VMEM_BYTES = 64 * 1024 * 1024
VMEM_LIMIT = 52 * 1024 * 1024

D_MODEL = 4096
EPS = 1e-6
ROPE_THETA = 10000.0

RET_HEADS = 4
RET_DV = 256
RET_DK = 128
RET_CHUNK = 128

SSM_WIDTH = 1024
SSM_GROUP = 16
SSM_GROUPS = 64
SSM_STATE = 64
SSM_COLS = 128
SSM_CH = SSM_COLS // SSM_GROUP * SSM_STATE
SSM_CHUNK = 256

SWA_HEAD_DIM = 64
SWA_Q_HEADS = 16
SWA_KV_HEADS = 2
SWA_WINDOW = 128

POOL_WINDOWS = (2, 4, 8, 16)
POOL_GROUP = 256
POOL_MAXW = 16

OFF_QR, OFF_KR, OFF_VR, OFF_GR, OFF_US, OFF_QA, OFF_KA, OFF_VA, OFF_UP = (
    0, 512, 1024, 2048, 3072, 4096, 5120, 5248, 5376)
IN_COLS = 6400

PEER_N_KEYS = 128
PEER_HEADS = 8
PEER_TOPK = 16
PEER_HK = PEER_HEADS * PEER_TOPK
PEER_TB = 8
PEER_GROUPS = PEER_HK // V7X_SUBLANES
PEER_S1_GROUPS = 2
PEER_S1_CHAINS = 4
PEER_S2_GROUPS = 4


def _cparams(sem):
    return pltpu.CompilerParams(dimension_semantics=sem, vmem_limit_bytes=VMEM_LIMIT)


def _gelu(x):
    return 0.5 * x * (1.0 + lax.erf(x * math.sqrt(0.5)))


def _sigmoid(x):
    return 1.0 / (1.0 + jnp.exp(-x))


def _ada_kernel(c_ref, w_ref, b_ref, o_ref):
    c = c_ref[...]
    a = c * _sigmoid(c)
    o_ref[0] = jnp.dot(a, w_ref[0], preferred_element_type=F32, precision=HIGHEST) + b_ref[0]


def _ada_mod(c8, ada_w, ada_b):
    L, D, N = ada_w.shape
    tn = 1024
    return pl.pallas_call(
        _ada_kernel,
        grid=(L, N // tn),
        in_specs=[pl.BlockSpec((8, D), lambda l, j: (0, 0)),
                  pl.BlockSpec((1, D, tn), lambda l, j: (l, 0, j)),
                  pl.BlockSpec((1, 1, tn), lambda l, j: (l, 0, j))],
        out_specs=pl.BlockSpec((1, 8, tn), lambda l, j: (l, 0, j)),
        out_shape=jax.ShapeDtypeStruct((L, 8, N), F32),
        compiler_params=_cparams(("arbitrary", "arbitrary")),
        name="ada_mod",
    )(c8, ada_w, ada_b.reshape(L, 1, N))


def _norm_mod_kernel(x_ref, g_ref, sc_ref, sh_ref, *o_refs):
    x = x_ref[...]
    ms = jnp.mean(x * x, axis=-1, keepdims=True)
    y = x * lax.rsqrt(ms + EPS) * g_ref[...]
    h = y * (1.0 + sc_ref[0]) + sh_ref[0]
    for o in o_refs:
        o[...] = h.astype(o.dtype)


def _norm_mod(x2, gain, sc, sh, seq, out_dtypes):
    T, D = x2.shape
    B = sc.shape[0]
    tm = 256
    per_b = seq // tm
    row = pl.BlockSpec((tm, D), lambda i: (i, 0))
    bspec = pl.BlockSpec((1, 1, D), lambda i: (i // per_b, 0, 0))
    outs = pl.pallas_call(
        _norm_mod_kernel,
        grid=(T // tm,),
        in_specs=[row, pl.BlockSpec((1, D), lambda i: (0, 0)), bspec, bspec],
        out_specs=[row for _ in out_dtypes],
        out_shape=[jax.ShapeDtypeStruct((T, D), dt) for dt in out_dtypes],
        compiler_params=_cparams(("arbitrary",)),
        name="norm_mod",
    )(x2, gain.reshape(1, D), sc.reshape(B, 1, D), sh.reshape(B, 1, D))
    return outs


def _mm_kernel(a_ref, w_ref, o_ref):
    o_ref[...] = jnp.dot(a_ref[...], w_ref[...], preferred_element_type=F32)


def _matmul(a, w, tm=512, tn=512):
    M, K = a.shape
    N = w.shape[1]
    return pl.pallas_call(
        _mm_kernel,
        grid=(M // tm, N // tn),
        in_specs=[pl.BlockSpec((tm, K), lambda i, j: (i, 0)),
                  pl.BlockSpec((K, tn), lambda i, j: (0, j))],
        out_specs=pl.BlockSpec((tm, tn), lambda i, j: (i, j)),
        out_shape=jax.ShapeDtypeStruct((M, N), F32),
        compiler_params=_cparams(("arbitrary", "arbitrary")),
        name="matmul",
    )(a, w)


def _cast_kernel3(x_ref, o_ref):
    o_ref[...] = x_ref[0].astype(o_ref.dtype)


def _to_bf16(w_stack, layer, rows=256):
    _, K, N = w_stack.shape
    rows = min(rows, K)
    return pl.pallas_call(
        _cast_kernel3,
        grid=(K // rows,),
        in_specs=[pl.BlockSpec((1, rows, N), lambda i: (layer, i, 0))],
        out_specs=pl.BlockSpec((rows, N), lambda i: (i, 0)),
        out_shape=jax.ShapeDtypeStruct((K, N), BF16),
        compiler_params=_cparams(("arbitrary",)),
        name="cast_bf16",
    )(w_stack)


def _rope_tables(positions, d):
    half = d // 2
    inv = ROPE_THETA ** (-jnp.arange(half, dtype=F32) * 2.0 / d)
    ang = positions.astype(F32).reshape(-1, 1) * inv
    cos, sin = jnp.cos(ang), jnp.sin(ang)
    reps = V7X_LANES // d
    cosf = jnp.concatenate([cos, cos] * reps, axis=-1)
    sinf = jnp.concatenate([-sin, sin] * reps, axis=-1)
    return cosf, sinf


def _ret_consts():
    L, H = RET_CHUNK, RET_HEADS
    log_g = np.log(1.0 - 2.0 ** (-5.0 - np.arange(H, dtype=np.float64)))
    idx = np.arange(L, dtype=np.float64)
    diff = idx[:, None] - idx[None, :]
    decay = np.where(diff >= 0, np.exp(np.maximum(diff, 0.0)[None] * log_g[:, None, None]), 0.0)
    w_k = np.exp((L - 1 - idx)[None, :] * log_g[:, None])
    w_q = np.exp((idx + 1)[None, :] * log_g[:, None])
    gam = np.exp(L * log_g)
    wk_full = np.broadcast_to(w_k[:, :, None], (H, L, RET_DK))
    wq_full = np.broadcast_to(w_q[:, :, None], (H, L, RET_DK))
    gam_full = np.broadcast_to(gam[:, None, None], (H, 1, RET_DV))
    return (jnp.asarray(decay, F32), jnp.asarray(wq_full, F32), jnp.asarray(wk_full, F32),
            jnp.asarray(gam_full, F32))


def _ret_kernel(q_ref, k_ref, v_ref, g_ref, cos_ref, sin_ref, dec_ref, wq_ref, wk_ref, gam_ref,
                o_ref, r_ref):
    n = pl.program_id(1)

    @pl.when(n == 0)
    def _():
        r_ref[...] = jnp.zeros_like(r_ref)

    cos = cos_ref[...]
    sin = sin_ref[...]
    for h in range(RET_HEADS):
        ks = slice(h * RET_DK, (h + 1) * RET_DK)
        vs = slice(h * RET_DV, (h + 1) * RET_DV)
        q = q_ref[:, ks]
        k = k_ref[:, ks]
        q = q * cos + pltpu.roll(q, RET_DK // 2, 1) * sin
        k = (k * cos + pltpu.roll(k, RET_DK // 2, 1) * sin) * (RET_DK ** -0.5)
        vb = v_ref[:, vs].astype(BF16)
        s = lax.dot_general(q.astype(BF16), k.astype(BF16), (((1,), (1,)), ((), ())),
                            preferred_element_type=F32) * dec_ref[h]
        o = jnp.dot(s.astype(BF16), vb, preferred_element_type=F32)
        r = r_ref[h]
        o = o + jnp.dot((q * wq_ref[h]).astype(BF16), r.astype(BF16), preferred_element_type=F32)
        kv = lax.dot_general((k * wk_ref[h]).astype(BF16), vb, (((0,), (0,)), ((), ())),
                             preferred_element_type=F32)
        r_ref[h] = gam_ref[h] * r + kv
        mu = jnp.mean(o, axis=-1, keepdims=True)
        oc = o - mu
        var = jnp.mean(oc * oc, axis=-1, keepdims=True)
        g = g_ref[:, vs]
        o_ref[:, vs] = g * _sigmoid(g) * (oc * lax.rsqrt(var + EPS))


def _retention(proj, cosf, sinf, batch, seq):
    T = proj.shape[0]
    L = RET_CHUNK
    n = seq // L
    H = RET_HEADS
    dec, wq, wk, gam = _ret_consts()
    kw, vw = H * RET_DK, H * RET_DV
    const3 = lambda b, c: (0, 0, 0)
    return pl.pallas_call(
        _ret_kernel,
        grid=(batch, n),
        in_specs=[
            pl.BlockSpec((L, kw), lambda b, c: (b * n + c, OFF_QR // kw)),
            pl.BlockSpec((L, kw), lambda b, c: (b * n + c, OFF_KR // kw)),
            pl.BlockSpec((L, vw), lambda b, c: (b * n + c, OFF_VR // vw)),
            pl.BlockSpec((L, vw), lambda b, c: (b * n + c, OFF_GR // vw)),
            pl.BlockSpec((L, RET_DK), lambda b, c: (b * n + c, 0)),
            pl.BlockSpec((L, RET_DK), lambda b, c: (b * n + c, 0)),
            pl.BlockSpec((H, L, L), const3),
            pl.BlockSpec((H, L, RET_DK), const3),
            pl.BlockSpec((H, L, RET_DK), const3),
            pl.BlockSpec((H, 1, RET_DV), const3),
        ],
        out_specs=pl.BlockSpec((L, vw), lambda b, c: (b * n + c, 0)),
        out_shape=jax.ShapeDtypeStruct((T, vw), F32),
        scratch_shapes=[pltpu.VMEM((H, RET_DK, RET_DV), F32)],
        compiler_params=_cparams(("arbitrary", "arbitrary")),
        name="retention",
    )(proj, proj, proj, proj, cosf, sinf, dec, wq, wk, gam)


def _ssm_params(a_re, a_im, log_dt, b_re, b_im, c_re, c_im):
    dt = jnp.exp(log_dt)[:, None]
    mag = jnp.exp(a_re * dt)
    ab_re = mag * jnp.cos(a_im * dt)
    ab_im = mag * jnp.sin(a_im * dt)
    den = a_re * a_re + a_im * a_im
    nr = ab_re - 1.0
    f_re = (nr * a_re + ab_im * a_im) / den
    f_im = (ab_im * a_re - nr * a_im) / den
    bb_re = f_re[..., None] * b_re - f_im[..., None] * b_im
    bb_im = f_re[..., None] * b_im + f_im[..., None] * b_re
    nblk = SSM_WIDTH // SSM_COLS
    gpb = SSM_COLS // SSM_GROUP
    eye = jnp.eye(gpb, dtype=F32)

    def blockdiag_in(bb):
        t = bb.reshape(nblk, gpb, SSM_STATE, SSM_GROUP)
        m = jnp.einsum('ngpc,gh->ngchp', t, eye)
        return m.reshape(nblk, SSM_COLS, SSM_CH)

    def blockdiag_out(cc):
        t = cc.reshape(nblk, gpb, SSM_GROUP, SSM_STATE)
        m = jnp.einsum('ngcp,gh->ngphc', t, eye)
        return m.reshape(nblk, SSM_CH, SSM_COLS)

    rounds = int(math.log2(SSM_CHUNK))
    pr, pi = [ab_re], [ab_im]
    for _ in range(rounds - 1):
        r, i = pr[-1], pi[-1]
        pr.append(r * r - i * i)
        pi.append(2.0 * r * i)
    apr = jnp.stack(pr, 0).reshape(rounds, nblk, SSM_CH).transpose(1, 0, 2)
    api = jnp.stack(pi, 0).reshape(rounds, nblk, SSM_CH).transpose(1, 0, 2)
    qr, qi = [ab_re], [ab_im]
    for _ in range(V7X_SUBLANES - 1):
        r, i = qr[-1], qi[-1]
        qr.append(r * ab_re - i * ab_im)
        qi.append(r * ab_im + i * ab_re)
    ppr = jnp.stack(qr, 0).reshape(V7X_SUBLANES, nblk, SSM_CH).transpose(1, 0, 2)
    ppi = jnp.stack(qi, 0).reshape(V7X_SUBLANES, nblk, SSM_CH).transpose(1, 0, 2)
    return (blockdiag_in(bb_re).astype(BF16), blockdiag_in(bb_im).astype(BF16),
            blockdiag_out(c_re).astype(BF16), blockdiag_out(c_im).astype(BF16), apr, api, ppr, ppi)


def _cmul_add(xr, xi, p_r, p_i, sr, si):
    return xr + p_r * sr - p_i * si, xi + p_r * si + p_i * sr


def _ssm_kernel(u_ref, bre_ref, bim_ref, cre_ref, cim_ref, apr_ref, api_ref, ppr_ref, ppi_ref, d_ref, o_ref,
                cr_ref, ci_ref, xr_s, xi_s, hr_s, hi_s):
    n = pl.program_id(2)
    hs = V7X_SUBLANES
    ng = SSM_CHUNK // hs

    @pl.when(n == 0)
    def _():
        cr_ref[...] = jnp.zeros_like(cr_ref)
        ci_ref[...] = jnp.zeros_like(ci_ref)

    u = u_ref[...]
    ub = u.astype(BF16)
    xr = jnp.dot(ub, bre_ref[0], preferred_element_type=F32)
    xi = jnp.dot(ub, bim_ref[0], preferred_element_type=F32)
    sub = jnp.bitwise_and(lax.broadcasted_iota(jnp.int32, xr.shape, 0), hs - 1)
    in_rounds = int(math.log2(hs))
    for k in range(in_rounds):
        s = 1 << k
        keep = sub >= s
        sr = jnp.where(keep, pltpu.roll(xr, s, 0), 0.0)
        si = jnp.where(keep, pltpu.roll(xi, s, 0), 0.0)
        xr, xi = _cmul_add(xr, xi, apr_ref[0, k:k + 1, :], api_ref[0, k:k + 1, :], sr, si)
    nlt = SSM_CH // V7X_LANES
    for q in range(nlt):
        xr_s[q] = xr[:, q * V7X_LANES:(q + 1) * V7X_LANES]
        xi_s[q] = xi[:, q * V7X_LANES:(q + 1) * V7X_LANES]

    er = jnp.concatenate([xr_s[q, pl.ds(hs - 1, ng, stride=hs), :] for q in range(nlt)], axis=1)
    ei = jnp.concatenate([xi_s[q, pl.ds(hs - 1, ng, stride=hs), :] for q in range(nlt)], axis=1)
    c_r = cr_ref[...]
    c_i = ci_ref[...]
    grow = lax.broadcasted_iota(jnp.int32, er.shape, 0)
    first = grow == 0
    a8r = apr_ref[0, in_rounds:in_rounds + 1, :]
    a8i = api_ref[0, in_rounds:in_rounds + 1, :]
    er = er + jnp.where(first, a8r * c_r - a8i * c_i, 0.0)
    ei = ei + jnp.where(first, a8r * c_i + a8i * c_r, 0.0)
    for k in range(int(math.log2(ng))):
        s = 1 << k
        keep = grow >= s
        sr = jnp.where(keep, pltpu.roll(er, s, 0), 0.0)
        si = jnp.where(keep, pltpu.roll(ei, s, 0), 0.0)
        kk = in_rounds + k
        er, ei = _cmul_add(er, ei, apr_ref[0, kk:kk + 1, :], api_ref[0, kk:kk + 1, :], sr, si)
    cr_ref[...] = er[ng - 1:ng, :]
    ci_ref[...] = ei[ng - 1:ng, :]
    hr_s[...] = jnp.where(first, c_r, pltpu.roll(er, 1, 0))
    hi_s[...] = jnp.where(first, c_i, pltpu.roll(ei, 1, 0))
    p_r = ppr_ref[0]
    p_i = ppi_ref[0]
    for g in range(ng):
        rows = slice(g * hs, (g + 1) * hs)
        for q in range(nlt):
            ls = slice(q * V7X_LANES, (q + 1) * V7X_LANES)
            hb_r = jnp.broadcast_to(hr_s[g:g + 1, ls], (hs, V7X_LANES))
            hb_i = jnp.broadcast_to(hi_s[g:g + 1, ls], (hs, V7X_LANES))
            yr, yi = _cmul_add(xr_s[q, rows, :], xi_s[q, rows, :], p_r[:, ls], p_i[:, ls], hb_r, hb_i)
            xr_s[q, rows, :] = yr
            xi_s[q, rows, :] = yi
    hr = jnp.concatenate([xr_s[q] for q in range(nlt)], axis=1)
    hi = jnp.concatenate([xi_s[q] for q in range(nlt)], axis=1)
    y = (jnp.dot(hr.astype(BF16), cre_ref[0], preferred_element_type=F32)
         - jnp.dot(hi.astype(BF16), cim_ref[0], preferred_element_type=F32))
    o_ref[...] = _gelu(y + d_ref[...] * u)


def _ssm_scan(proj, params, d, batch, seq):
    T = proj.shape[0]
    bre, bim, cre, cim, apr, api, ppr, ppi = params
    nblk = SSM_WIDTH // SSM_COLS
    nt = seq // SSM_CHUNK
    rounds = apr.shape[1]
    ngroups = SSM_CHUNK // V7X_SUBLANES
    wspec_in = pl.BlockSpec((1, SSM_COLS, SSM_CH), lambda b, c, n: (c, 0, 0))
    wspec_out = pl.BlockSpec((1, SSM_CH, SSM_COLS), lambda b, c, n: (c, 0, 0))
    pspec = pl.BlockSpec((1, rounds, SSM_CH), lambda b, c, n: (c, 0, 0))
    qspec = pl.BlockSpec((1, V7X_SUBLANES, SSM_CH), lambda b, c, n: (c, 0, 0))
    return pl.pallas_call(
        _ssm_kernel,
        grid=(batch, nblk, nt),
        in_specs=[pl.BlockSpec((SSM_CHUNK, SSM_COLS), lambda b, c, n: (b * nt + n, OFF_US // SSM_COLS + c)),
                  wspec_in, wspec_in, wspec_out, wspec_out, pspec, pspec, qspec, qspec,
                  pl.BlockSpec((1, SSM_COLS), lambda b, c, n: (0, c))],
        out_specs=pl.BlockSpec((SSM_CHUNK, SSM_COLS), lambda b, c, n: (b * nt + n, c)),
        out_shape=jax.ShapeDtypeStruct((T, SSM_WIDTH), F32),
        scratch_shapes=[pltpu.VMEM((1, SSM_CH), F32), pltpu.VMEM((1, SSM_CH), F32),
                        pltpu.VMEM((SSM_CH // V7X_LANES, SSM_CHUNK, V7X_LANES), F32),
                        pltpu.VMEM((SSM_CH // V7X_LANES, SSM_CHUNK, V7X_LANES), F32),
                        pltpu.VMEM((ngroups, SSM_CH), F32), pltpu.VMEM((ngroups, SSM_CH), F32)],
        compiler_params=_cparams(("arbitrary", "arbitrary", "arbitrary")),
        name="ssm_scan",
    )(proj, bre, bim, cre, cim, apr, api, ppr, ppi, d.reshape(1, SSM_WIDTH))


def _glu_kernel(y_ref, w_ref, b_ref, o_ref):
    y = y_ref[...]
    z = jnp.dot(y.astype(BF16), w_ref[...], preferred_element_type=F32) + b_ref[...]
    o_ref[...] = y * _sigmoid(z)


def _glu(y, w, b):
    T, W = y.shape
    tm = 512
    return pl.pallas_call(
        _glu_kernel,
        grid=(T // tm,),
        in_specs=[pl.BlockSpec((tm, W), lambda i: (i, 0)),
                  pl.BlockSpec((W, W), lambda i: (0, 0)),
                  pl.BlockSpec((1, W), lambda i: (0, 0))],
        out_specs=pl.BlockSpec((tm, W), lambda i: (i, 0)),
        out_shape=jax.ShapeDtypeStruct((T, W), F32),
        compiler_params=_cparams(("arbitrary",)),
        name="ssm_glu",
    )(y, w, b.reshape(1, W))


def _swa_kernel(sink_ref, q_ref, k_ref, v_ref, cos_ref, sin_ref, qn_ref, kn_ref, o_ref, pk_ref, pv_ref):
    n = pl.program_id(1)
    W = SWA_WINDOW
    hd = SWA_HEAD_DIM

    @pl.when(n == 0)
    def _():
        pk_ref[...] = jnp.zeros_like(pk_ref)
        pv_ref[...] = jnp.zeros_like(pv_ref)

    cos = cos_ref[...]
    sin = sin_ref[...]
    lane = lax.broadcasted_iota(jnp.int32, (W, V7X_LANES), 1)
    lo = lane < hd
    first_half = (lane % hd) < (hd // 2)

    def head_norm(t, gain):
        tt = t * t
        s_all = jnp.sum(tt, axis=-1, keepdims=True)
        s_lo = jnp.sum(jnp.where(lo, tt, 0.0), axis=-1, keepdims=True)
        ms = jnp.where(lo, s_lo, s_all - s_lo) * (1.0 / hd)
        return t * lax.rsqrt(ms + EPS) * gain

    def rope(t):
        sw = jnp.where(first_half, pltpu.roll(t, V7X_LANES - hd // 2, 1), pltpu.roll(t, hd // 2, 1))
        return t * cos + sw * sin

    kc = rope(head_norm(k_ref[...], kn_ref[...]))
    vc = v_ref[...]
    keys = jnp.concatenate([pk_ref[...], kc], axis=0)
    vals = jnp.concatenate([pv_ref[...], vc], axis=0)
    lane2 = lax.broadcasted_iota(jnp.int32, (2 * W, V7X_LANES), 1)
    lo2 = lane2 < hd

    def dup(x, h):
        sw = pltpu.roll(x, hd, 1)
        return (jnp.where(lo2, x, sw) if h == 0 else jnp.where(lo2, sw, x)).astype(BF16)

    kk = [dup(keys, h) for h in range(SWA_KV_HEADS)]
    vv = [dup(vals, h) for h in range(SWA_KV_HEADS)]
    qi = lax.broadcasted_iota(jnp.int32, (W, 2 * W), 0)
    ci = lax.broadcasted_iota(jnp.int32, (W, 2 * W), 1)
    prev_floor = qi + jnp.where(n > 0, 0, 2 * W)
    valid = jnp.logical_or(jnp.logical_and(ci < W, ci > prev_floor),
                           jnp.logical_and(ci >= W, (ci - W) <= qi))
    tiles = SWA_Q_HEADS * hd // V7X_LANES
    per_kv = tiles // SWA_KV_HEADS
    for m in range(tiles):
        sl = slice(m * V7X_LANES, (m + 1) * V7X_LANES)
        qt = rope(head_norm(q_ref[:, sl], qn_ref[:, sl]))
        h = m // per_kv
        outs = []
        for half in range(2):
            sel = lo if half == 0 else jnp.logical_not(lo)
            qm = jnp.where(sel, qt, 0.0).astype(BF16)
            s = lax.dot_general(qm, kk[h], (((1,), (1,)), ((), ())),
                                preferred_element_type=F32) * (hd ** -0.5)
            s = jnp.where(valid, s, -1e30)
            sink = sink_ref[2 * m + half]
            mx = jnp.maximum(jnp.max(s, axis=-1, keepdims=True), sink)
            p = jnp.exp(s - mx)
            den = jnp.sum(p, axis=-1, keepdims=True) + jnp.exp(sink - mx)
            p = p / den
            outs.append(jnp.dot(p.astype(BF16), vv[h], preferred_element_type=F32))
        o_ref[:, sl] = jnp.where(lo, outs[0], outs[1])
    pk_ref[...] = kc
    pv_ref[...] = vc


def _swa(proj, cosf, sinf, q_norm, k_norm, sinks, batch, seq):
    T = proj.shape[0]
    W = SWA_WINDOW
    nb = seq // W
    qw = SWA_Q_HEADS * SWA_HEAD_DIM
    qn = jnp.tile(q_norm, SWA_Q_HEADS).reshape(1, qw)
    kn = jnp.tile(k_norm, SWA_KV_HEADS).reshape(1, V7X_LANES)
    return pl.pallas_call(
        _swa_kernel,
        grid=(batch, nb),
        in_specs=[pl.BlockSpec(memory_space=pltpu.SMEM),
                  pl.BlockSpec((W, qw), lambda b, n: (b * nb + n, OFF_QA // qw)),
                  pl.BlockSpec((W, V7X_LANES), lambda b, n: (b * nb + n, OFF_KA // V7X_LANES)),
                  pl.BlockSpec((W, V7X_LANES), lambda b, n: (b * nb + n, OFF_VA // V7X_LANES)),
                  pl.BlockSpec((W, V7X_LANES), lambda b, n: (b * nb + n, 0)),
                  pl.BlockSpec((W, V7X_LANES), lambda b, n: (b * nb + n, 0)),
                  pl.BlockSpec((1, qw), lambda b, n: (0, 0)),
                  pl.BlockSpec((1, V7X_LANES), lambda b, n: (0, 0))],
        out_specs=pl.BlockSpec((W, qw), lambda b, n: (b * nb + n, 0)),
        out_shape=jax.ShapeDtypeStruct((T, qw), F32),
        scratch_shapes=[pltpu.VMEM((W, V7X_LANES), F32), pltpu.VMEM((W, V7X_LANES), F32)],
        compiler_params=_cparams(("arbitrary", "arbitrary")),
        name="swa",
    )(sinks, proj, proj, proj, cosf, sinf, qn, kn)


POOL_CHUNK = 256


def _pool_kernel(u0_ref, u1_ref, u2_ref, u3_ref, w_ref, sc_ref, o_ref, ext_ref):
    n = pl.program_id(1)
    Lc = POOL_CHUNK

    @pl.when(n == 0)
    def _():
        ext_ref[:, 0:POOL_MAXW, :] = jnp.zeros((len(POOL_WINDOWS), POOL_MAXW, POOL_GROUP), F32)

    t = n * Lc + lax.broadcasted_iota(jnp.int32, (Lc, POOL_GROUP), 0)
    for g, (u_ref, w) in enumerate(zip((u0_ref, u1_ref, u2_ref, u3_ref), POOL_WINDOWS)):
        u = u_ref[...]
        ext_ref[g, POOL_MAXW:POOL_MAXW + Lc, :] = u
        acc = u
        for k in range(1, w):
            acc = acc + ext_ref[g, POOL_MAXW - k:POOL_MAXW - k + Lc, :]
        cnt = jnp.minimum(t + 1, w).astype(F32)
        pooled = acc / cnt - u
        y = jnp.dot(pooled.astype(BF16), w_ref[g], preferred_element_type=F32)
        sl = slice(g * POOL_GROUP, (g + 1) * POOL_GROUP)
        o_ref[:, sl] = y * sc_ref[:, sl]
        ext_ref[g, 0:POOL_MAXW, :] = u[Lc - POOL_MAXW:, :]


def _pool(proj, pool_w, pool_scale, batch, seq):
    T = proj.shape[0]
    Lc = POOL_CHUNK
    nt = seq // Lc
    ng = len(POOL_WINDOWS)
    width = ng * POOL_GROUP
    uspecs = [pl.BlockSpec((Lc, POOL_GROUP), functools.partial(
        lambda b, n, g: (b * nt + n, OFF_UP // POOL_GROUP + g), g=g)) for g in range(ng)]
    return pl.pallas_call(
        _pool_kernel,
        grid=(batch, nt),
        in_specs=uspecs + [pl.BlockSpec((ng, POOL_GROUP, POOL_GROUP), lambda b, n: (0, 0, 0)),
                           pl.BlockSpec((1, width), lambda b, n: (0, 0))],
        out_specs=pl.BlockSpec((Lc, width), lambda b, n: (b * nt + n, 0)),
        out_shape=jax.ShapeDtypeStruct((T, width), F32),
        scratch_shapes=[pltpu.VMEM((ng, POOL_MAXW + Lc, POOL_GROUP), F32)],
        compiler_params=_cparams(("arbitrary", "arbitrary")),
        name="pool",
    )(proj, proj, proj, proj, pool_w.astype(BF16), pool_scale.reshape(1, width))


def _out_kernel(r_ref, s_ref, a_ref, p_ref, gain_ref, w_ref, x_ref, g1_ref, o_ref, mix_ref):
    j = pl.program_id(1)

    @pl.when(j == 0)
    def _():
        for idx, ref in enumerate((r_ref, s_ref, a_ref, p_ref)):
            v = ref[...]
            width = v.shape[1]
            sl = slice(idx * width, (idx + 1) * width)
            ms = jnp.mean(v * v, axis=-1, keepdims=True)
            mix_ref[:, sl] = (v * lax.rsqrt(ms + EPS) * gain_ref[:, sl]).astype(BF16)

    acc = jnp.dot(mix_ref[...], w_ref[...], preferred_element_type=F32)
    o_ref[...] = x_ref[...] + g1_ref[0] * acc


def _out_proj(branches, gain, w_out_bf, x2, g1, seq):
    T, D = x2.shape
    B = g1.shape[0]
    tm, tn = 512, 512
    per_b = seq // tm
    width = branches[0].shape[1]
    bspec = pl.BlockSpec((tm, width), lambda i, j: (i, 0))
    return pl.pallas_call(
        _out_kernel,
        grid=(T // tm, D // tn),
        in_specs=[bspec, bspec, bspec, bspec,
                  pl.BlockSpec((1, D), lambda i, j: (0, 0)),
                  pl.BlockSpec((D, tn), lambda i, j: (0, j)),
                  pl.BlockSpec((tm, tn), lambda i, j: (i, j)),
                  pl.BlockSpec((1, 1, tn), lambda i, j: (i // per_b, 0, j))],
        out_specs=pl.BlockSpec((tm, tn), lambda i, j: (i, j)),
        out_shape=jax.ShapeDtypeStruct((T, D), F32),
        scratch_shapes=[pltpu.VMEM((tm, D), BF16)],
        compiler_params=_cparams(("arbitrary", "arbitrary")),
        name="out_proj",
    )(*branches, gain.reshape(1, D), w_out_bf, x2, g1.reshape(B, 1, D))


PEER_TOPK_TB = 128


def _iter_topk(s, k):
    n = s.shape[0]
    rows = lax.broadcasted_iota(jnp.int32, s.shape, 0).astype(F32)
    vals, idxs = [], []
    for _ in range(k):
        m = jnp.max(s, axis=0, keepdims=True)
        idx = jnp.min(jnp.where(s == m, rows, float(n)), axis=0, keepdims=True)
        s = jnp.where(rows == idx, -jnp.inf, s)
        vals.append(m)
        idxs.append(idx)
    return jnp.concatenate(vals, axis=0), jnp.concatenate(idxs, axis=0)


def _topk_kernel(q_ref, keys_ref, e_ref, g_ref):
    K = PEER_TOPK
    half = PEER_N_KEYS
    e_all, g_all = [], []
    for h in range(PEER_HEADS):
        tops = []
        for p in range(2):
            c0 = h * 2 * half + p * half
            qhp = q_ref[:, c0:c0 + half].astype(BF16)
            s = lax.dot_general(keys_ref[p], qhp, (((1,), (1,)), ((), ())), preferred_element_type=F32)
            tops.append(_iter_topk(s, K))
        (s0, i0), (s1, i1) = tops
        hs = V7X_SUBLANES
        io8 = lax.broadcasted_iota(jnp.int32, (hs, s0.shape[1]), 0).astype(F32)
        ps, pe, pp = [], [], []
        for a, b0 in [(0, 0), (0, hs)] + [(a, 0) for a in range(1, hs)]:
            ps.append(s0[a:a + 1, :] + s1[b0:b0 + hs, :])
            pe.append(i0[a:a + 1, :] * PEER_N_KEYS + i1[b0:b0 + hs, :])
            pp.append(io8 + float(a * K + b0))
        ps.append(s0[hs:K, :] + s1[0:1, :])
        pe.append(i0[hs:K, :] * PEER_N_KEYS + i1[0:1, :])
        pp.append((io8 + float(hs)) * float(K))
        cand_s = jnp.concatenate(ps, axis=0)
        cand_e = jnp.concatenate(pe, axis=0)
        rows = jnp.concatenate(pp, axis=0)
        bs, be = [], []
        for _ in range(K):
            m = jnp.max(cand_s, axis=0, keepdims=True)
            pos = jnp.min(jnp.where(cand_s == m, rows, float(K * K)), axis=0, keepdims=True)
            hit = rows == pos
            be.append(jnp.sum(jnp.where(hit, cand_e, 0.0), axis=0, keepdims=True))
            cand_s = jnp.where(hit, -jnp.inf, cand_s)
            bs.append(m)
        best = jnp.concatenate(bs, axis=0)
        pexp = jnp.exp(best - best[0:1, :])
        g_all.append(pexp / jnp.sum(pexp, axis=0, keepdims=True))
        e_all.append(jnp.concatenate(be, axis=0))
    e_ref[...] = jnp.concatenate(e_all, axis=0).T.astype(jnp.int32)
    g_ref[...] = jnp.concatenate(g_all, axis=0).T


def _peer_topk(q, sub_keys_bf):
    T, QW = q.shape
    tb = PEER_TOPK_TB
    return pl.pallas_call(
        _topk_kernel,
        grid=(T // tb,),
        in_specs=[pl.BlockSpec((tb, QW), lambda i: (i, 0)),
                  pl.BlockSpec((2, PEER_N_KEYS, PEER_N_KEYS), lambda i: (0, 0, 0))],
        out_specs=[pl.BlockSpec((tb, PEER_HK), lambda i: (i, 0)),
                   pl.BlockSpec((tb, PEER_HK), lambda i: (i, 0))],
        out_shape=[jax.ShapeDtypeStruct((T, PEER_HK), jnp.int32),
                   jax.ShapeDtypeStruct((T, PEER_HK), F32)],
        compiler_params=_cparams(("arbitrary",)),
        name="peer_topk",
    )(q, sub_keys_bf)


PEER_PACK_ROWS = 64


def _pack_kernel(u_ref, v_ref, o_hbm, word_ref, sem):
    i = pl.program_id(0)
    n = pl.num_programs(0)
    slot = i % 2

    def row_copy(step, s):
        return pltpu.make_async_copy(word_ref.at[s], o_hbm.at[pl.ds(step * PEER_PACK_ROWS, PEER_PACK_ROWS), 0, :],
                                     sem.at[s])

    @pl.when(i >= 2)
    def _():
        row_copy(i - 2, slot).wait()

    ub = lax.bitcast_convert_type(u_ref[0].astype(BF16).astype(F32), jnp.int32)
    vb = lax.bitcast_convert_type(v_ref[0].astype(BF16).astype(F32), jnp.int32)
    word_ref[slot] = jnp.bitwise_or(lax.shift_right_logical(ub, 16), vb)
    row_copy(i, slot).start()

    @pl.when(i == n - 1)
    def _():
        row_copy(i, slot).wait()

        @pl.when(n >= 2)
        def _():
            row_copy(i - 1, 1 - slot).wait()


def _pack_tables(u_stack, v_stack, layer):
    _, E, D = u_stack.shape
    R = PEER_PACK_ROWS
    spec = pl.BlockSpec((1, R, D), lambda i: (layer, i, 0))
    return pl.pallas_call(
        _pack_kernel,
        grid=(E // R,),
        in_specs=[spec, spec],
        out_specs=pl.BlockSpec(memory_space=pl.ANY),
        out_shape=jax.ShapeDtypeStruct((E, V7X_SUBLANES, D), jnp.int32),
        scratch_shapes=[pltpu.VMEM((2, R, D), jnp.int32), pltpu.SemaphoreType.DMA((2,))],
        compiler_params=_cparams(("arbitrary",)),
        name="peer_pack",
    )(u_stack, v_stack)


def _peer_kernel(ec_ref, en_ref, h_ref, gate_ref, x_ref, g2_ref, tab_hbm, o_ref,
                 buf0, buf1, acc_ref, cmat_ref, sem0, sem1):
    i = pl.program_id(0)
    _peer_half(i, 0, buf0, sem0, buf1, sem1, ec_ref, PEER_TB, ec_ref, h_ref, gate_ref,
               tab_hbm, o_ref, acc_ref, cmat_ref)
    _peer_half(i, PEER_TB, buf1, sem1, buf0, sem0, en_ref, 0, ec_ref, h_ref, gate_ref,
               tab_hbm, o_ref, acc_ref, cmat_ref)
    o_ref[...] = x_ref[...] + g2_ref[0] * o_ref[...]


def _peer_half(i, row0, buf, csem, nbuf, nsem, nidx_ref, nrow0, ec_ref, h_ref, gate_ref, tab_hbm, o_ref,
               acc_ref, cmat_ref):
    D = h_ref.shape[1]
    nch = D // V7X_LANES
    half_ch = nch // 2
    G = PEER_GROUPS
    cur = 0
    nxt = 0
    zero = jnp.zeros((V7X_SUBLANES, V7X_LANES), F32)

    def start_group(idx_ref, r, g, slot, dst=nbuf, dsem=nsem):
        for s in range(V7X_SUBLANES):
            e = idx_ref[r * PEER_HK + g * V7X_SUBLANES + s]
            pltpu.make_async_copy(tab_hbm.at[e, pl.ds(0, 1), :], dst.at[slot, g, pl.ds(s, 1), :],
                                  dsem.at[slot]).start()

    def wait_slot(slot, b=buf, bsem=csem):
        pltpu.make_async_copy(b.at[slot], b.at[slot], bsem.at[slot]).wait()

    if row0 == 0:
        @pl.when(i == 0)
        def _():
            for tt in range(PEER_TB):
                def body(g, carry, tt=tt):
                    start_group(ec_ref, tt, g, tt, dst=buf, dsem=csem)
                    return carry
                lax.fori_loop(0, G, body, 0)

    for tt in range(PEER_TB):
        wait_slot(cur + tt)

        def s1_body(j, carry, tt=tt):
            parts = [[None] * PEER_S1_CHAINS for _ in range(PEER_S1_GROUPS)]
            for c in range(nch):
                cs = slice(c * V7X_LANES, (c + 1) * V7X_LANES)
                hb = jnp.broadcast_to(h_ref[row0 + tt:row0 + tt + 1, cs], (V7X_SUBLANES, V7X_LANES))
                for q in range(PEER_S1_GROUPS):
                    w = buf[cur + tt, j * PEER_S1_GROUPS + q, :, cs]
                    t = lax.bitcast_convert_type(jnp.left_shift(w, 16), F32) * hb
                    p = parts[q][c % PEER_S1_CHAINS]
                    parts[q][c % PEER_S1_CHAINS] = t if p is None else p + t
            for q in range(PEER_S1_GROUPS):
                r0 = pl.multiple_of((tt * G + j * PEER_S1_GROUPS + q) * V7X_SUBLANES, V7X_SUBLANES)
                acc_ref[pl.ds(r0, V7X_SUBLANES), :] = (parts[q][0] + parts[q][1]) + (parts[q][2] + parts[q][3])
            start_group(nidx_ref, nrow0 + tt, j, nxt + tt)
            return carry

        lax.fori_loop(0, G // PEER_S1_GROUPS, s1_body, 0)

    ones_row = jnp.ones((V7X_SUBLANES, V7X_LANES), BF16)
    ones_sq = jnp.ones((V7X_LANES, V7X_LANES), BF16)
    nt_dims = (((1,), (1,)), ((), ()))
    acc = acc_ref[...]
    acc_hi = acc.astype(BF16)
    acc_lo = (acc - acc_hi.astype(F32)).astype(BF16)
    a8 = (lax.dot_general(ones_row, acc_hi, nt_dims, preferred_element_type=F32)
          + lax.dot_general(ones_row, acc_lo, nt_dims, preferred_element_type=F32))
    a_tok = jnp.concatenate([a8[0:1, tt * PEER_HK:(tt + 1) * PEER_HK] for tt in range(PEER_TB)], axis=0)
    c_tok = gate_ref[row0:row0 + PEER_TB, :] * _gelu(a_tok)
    r_i = lax.broadcasted_iota(jnp.int32, (PEER_HK, V7X_LANES), 0)
    c_i = lax.broadcasted_iota(jnp.int32, (PEER_HK, V7X_LANES), 1)
    eye = r_i == c_i
    for tt in range(PEER_TB):
        c_diag = jnp.where(eye, jnp.broadcast_to(c_tok[tt:tt + 1, :], (PEER_HK, V7X_LANES)), 0.0)
        d_hi = c_diag.astype(BF16)
        d_lo = (c_diag - d_hi.astype(F32)).astype(BF16)
        cmat_ref[tt * PEER_HK:(tt + 1) * PEER_HK, :] = (
            jnp.dot(d_hi, ones_sq, preferred_element_type=F32) + jnp.dot(d_lo, ones_sq, preferred_element_type=F32))

    for tt in range(PEER_TB):
        for half in range(2):
            def s2_body(j, accs, tt=tt, half=half):
                new = list(accs)
                for q in range(PEER_S2_GROUPS):
                    g = j * PEER_S2_GROUPS + q
                    r0 = pl.multiple_of((tt * G + g) * V7X_SUBLANES, V7X_SUBLANES)
                    cm = cmat_ref[pl.ds(r0, V7X_SUBLANES), :]
                    for cc in range(half_ch):
                        c = half * half_ch + cc
                        w = buf[cur + tt, g, :, c * V7X_LANES:(c + 1) * V7X_LANES]
                        v = lax.bitcast_convert_type(jnp.bitwise_and(w, jnp.int32(-65536)), F32)
                        new[cc] = new[cc] + cm * v
                start_group(nidx_ref, nrow0 + tt, G // 2 + half * (G // 4) + j, nxt + tt)
                return tuple(new)

            accs = lax.fori_loop(0, G // PEER_S2_GROUPS, s2_body, tuple(zero for _ in range(half_ch)))
            for cc in range(half_ch):
                c = half * half_ch + cc
                o_ref[row0 + tt:row0 + tt + 1, c * V7X_LANES:(c + 1) * V7X_LANES] = jnp.sum(
                    accs[cc], axis=0, keepdims=True)

    if row0 != 0:
        @pl.when(i == pl.num_programs(0) - 1)
        def _():
            for tt in range(PEER_TB):
                wait_slot(tt, b=nbuf, bsem=nsem)


def _peer_gather(experts, gates, h32, x2, g2, uv_tab, seq):
    T, D = x2.shape
    B = g2.shape[0]
    tb = 2 * PEER_TB
    per_b = seq // tb
    nsteps = T // tb
    G = PEER_GROUPS
    assert G // PEER_S1_GROUPS == G // 2 and G // PEER_S2_GROUPS == G // 4
    row = pl.BlockSpec((tb, D), lambda i: (i, 0))
    eflat = experts.reshape(T * PEER_HK)
    return pl.pallas_call(
        _peer_kernel,
        grid=(nsteps,),
        in_specs=[pl.BlockSpec((tb * PEER_HK,), lambda i: (i,), memory_space=pltpu.SMEM),
                  pl.BlockSpec((tb * PEER_HK,), lambda i: (jnp.minimum(i + 1, nsteps - 1),),
                               memory_space=pltpu.SMEM),
                  row,
                  pl.BlockSpec((tb, PEER_HK), lambda i: (i, 0)),
                  row,
                  pl.BlockSpec((1, 1, D), lambda i: (i // per_b, 0, 0)),
                  pl.BlockSpec(memory_space=pl.ANY)],
        out_specs=row,
        out_shape=jax.ShapeDtypeStruct((T, D), F32),
        scratch_shapes=[pltpu.VMEM((PEER_TB, G, V7X_SUBLANES, D), jnp.int32),
                        pltpu.VMEM((PEER_TB, G, V7X_SUBLANES, D), jnp.int32),
                        pltpu.VMEM((PEER_TB * PEER_HK, V7X_LANES), F32),
                        pltpu.VMEM((PEER_TB * PEER_HK, V7X_LANES), F32),
                        pltpu.SemaphoreType.DMA((PEER_TB,)),
                        pltpu.SemaphoreType.DMA((PEER_TB,))],
        compiler_params=_cparams(("arbitrary",)),
        name="peer_gather",
    )(eflat, eflat, h32, gates, x2, g2.reshape(B, 1, D), uv_tab)


def _layer(li, x2, mod, cos_r, sin_r, cos_a, sin_a, batch, seq, norm1_g, norm2_g, w_in_all, w_out_all, out_norm_g,
           ssm_a_re, ssm_a_im, ssm_log_dt, ssm_b_re, ssm_b_im, ssm_c_re, ssm_c_im,
           ssm_d, ssm_w_glu_all, ssm_b_glu, attn_q_norm, attn_k_norm, attn_sinks,
           pool_w, pool_scale, peer_w_query_all, peer_sub_keys, peer_u_all, peer_v_all):
    D = D_MODEL
    sh1, sc1, g1, sh2, sc2, g2 = [mod[:batch, k * D:(k + 1) * D] for k in range(6)]

    (h1,) = _norm_mod(x2, norm1_g, sc1, sh1, seq, (BF16,))
    proj = _matmul(h1, _to_bf16(w_in_all, li), tm=1024, tn=640)

    ret = _retention(proj, cos_r, sin_r, batch, seq)
    ssm_params = _ssm_params(ssm_a_re, ssm_a_im, ssm_log_dt, ssm_b_re, ssm_b_im, ssm_c_re, ssm_c_im)
    ssm = _glu(_ssm_scan(proj, ssm_params, ssm_d, batch, seq), _to_bf16(ssm_w_glu_all, li), ssm_b_glu)
    swa = _swa(proj, cos_a, sin_a, attn_q_norm, attn_k_norm, attn_sinks, batch, seq)
    pool = _pool(proj, pool_w, pool_scale, batch, seq)
    x2 = _out_proj((ret, ssm, swa, pool), out_norm_g, _to_bf16(w_out_all, li), x2, g1, seq)

    h2b, h2f = _norm_mod(x2, norm2_g, sc2, sh2, seq, (BF16, F32))
    q = _matmul(h2b, _to_bf16(peer_w_query_all, li), tm=1024, tn=512)
    experts, gates = _peer_topk(q, peer_sub_keys.astype(BF16))
    return _peer_gather(experts, gates, h2f, x2, g2, _pack_tables(peer_u_all, peer_v_all, li), seq)


def kernel(x, c, positions, ada_w, ada_b, norm1_g, norm2_g, w_in, w_out, out_norm_g, ssm_a_re, ssm_a_im, ssm_log_dt, ssm_b_re, ssm_b_im, ssm_c_re, ssm_c_im, ssm_d, ssm_w_glu, ssm_b_glu, attn_q_norm, attn_k_norm, attn_sinks, pool_w, pool_scale, peer_w_query, peer_sub_keys, peer_u, peer_v):
    B, S, D = x.shape
    depth = ada_w.shape[0]
    c8 = jnp.concatenate([c, jnp.zeros((V7X_SUBLANES - B, D), c.dtype)], axis=0)
    mod = _ada_mod(c8, ada_w, ada_b)
    cos_r, sin_r = _rope_tables(positions, RET_DK)
    cos_a, sin_a = _rope_tables(positions, SWA_HEAD_DIM)
    x2 = x.reshape(B * S, D)
    for i in range(depth):
        x2 = _layer(i, x2, mod[i], cos_r, sin_r, cos_a, sin_a, B, S, norm1_g[i], norm2_g[i], w_in, w_out,
                    out_norm_g[i], ssm_a_re[i], ssm_a_im[i], ssm_log_dt[i], ssm_b_re[i], ssm_b_im[i],
                    ssm_c_re[i], ssm_c_im[i], ssm_d[i], ssm_w_glu, ssm_b_glu[i],
                    attn_q_norm[i], attn_k_norm[i], attn_sinks[i], pool_w[i], pool_scale[i],
                    peer_w_query, peer_sub_keys[i], peer_u, peer_v)
    return x2.reshape(B, S, D)
```

```python
import functools
import math

import numpy as np
import jax
import jax.numpy as jnp
from jax import lax
from jax.experimental import pallas as pl
from jax.experimental.pallas import tpu as pltpu

F32 = jnp.float32
BF16 = jnp.bfloat16
HIGHEST = lax.Precision.HIGHEST

V7X_LANES = 128
V7X_SUBLANES = 8
V7X_VMEM_BYTES = 64 * 1024 * 1024
VMEM_LIMIT = 52 * 1024 * 1024

D_MODEL = 4096
EPS = 1e-6
ROPE_THETA = 10000.0

RET_HEADS = 4
RET_DV = 256
RET_DK = 128
RET_CHUNK = 128

SSM_WIDTH = 1024
SSM_GROUP = 16
SSM_GROUPS = 64
SSM_STATE = 64
SSM_COLS = 128
SSM_CH = SSM_COLS // SSM_GROUP * SSM_STATE
SSM_CHUNK = 256

SWA_HEAD_DIM = 64
SWA_Q_HEADS = 16
SWA_KV_HEADS = 2
SWA_WINDOW = 128

POOL_WINDOWS = (2, 4, 8, 16)
POOL_GROUP = 256
POOL_MAXW = 16

OFF_QR, OFF_KR, OFF_VR, OFF_GR, OFF_US, OFF_QA, OFF_KA, OFF_VA, OFF_UP = (
    0, 512, 1024, 2048, 3072, 4096, 5120, 5248, 5376)
IN_COLS = 6400

PEER_N_KEYS = 128
PEER_HEADS = 8
PEER_TOPK = 16
PEER_HK = PEER_HEADS * PEER_TOPK
PEER_TB = 8
PEER_GROUPS = PEER_HK // V7X_SUBLANES
PEER_S1_GROUPS = 4
PEER_S1_CHAINS = 4
PEER_S2_GROUPS = 4


def _cparams(sem):
    return pltpu.CompilerParams(dimension_semantics=sem, vmem_limit_bytes=VMEM_LIMIT)


def _gelu(x):
    return 0.5 * x * (1.0 + lax.erf(x * math.sqrt(0.5)))


def _sigmoid(x):
    return 1.0 / (1.0 + jnp.exp(-x))


def _ada_kernel(c_ref, w_ref, b_ref, o_ref):
    c = c_ref[...]
    a = c * _sigmoid(c)
    o_ref[0] = jnp.dot(a, w_ref[0], preferred_element_type=F32, precision=HIGHEST) + b_ref[0]


def _ada_mod(c8, ada_w, ada_b):
    L, D, N = ada_w.shape
    tn = 1024
    return pl.pallas_call(
        _ada_kernel,
        grid=(L, N // tn),
        in_specs=[pl.BlockSpec((8, D), lambda l, j: (0, 0)),
                  pl.BlockSpec((1, D, tn), lambda l, j: (l, 0, j)),
                  pl.BlockSpec((1, 1, tn), lambda l, j: (l, 0, j))],
        out_specs=pl.BlockSpec((1, 8, tn), lambda l, j: (l, 0, j)),
        out_shape=jax.ShapeDtypeStruct((L, 8, N), F32),
        compiler_params=_cparams(("arbitrary", "arbitrary")),
        name="ada_mod",
    )(c8, ada_w, ada_b.reshape(L, 1, N))


def _norm_mod_kernel(x_ref, g_ref, sc_ref, sh_ref, *o_refs):
    x = x_ref[...]
    ms = jnp.mean(x * x, axis=-1, keepdims=True)
    y = x * lax.rsqrt(ms + EPS) * g_ref[...]
    h = y * (1.0 + sc_ref[0]) + sh_ref[0]
    for o in o_refs:
        o[...] = h.astype(o.dtype)


def _norm_mod(x2, gain, sc, sh, seq, out_dtypes):
    T, D = x2.shape
    B = sc.shape[0]
    tm = 256
    per_b = seq // tm
    row = pl.BlockSpec((tm, D), lambda i: (i, 0))
    bspec = pl.BlockSpec((1, 1, D), lambda i: (i // per_b, 0, 0))
    outs = pl.pallas_call(
        _norm_mod_kernel,
        grid=(T // tm,),
        in_specs=[row, pl.BlockSpec((1, D), lambda i: (0, 0)), bspec, bspec],
        out_specs=[row for _ in out_dtypes],
        out_shape=[jax.ShapeDtypeStruct((T, D), dt) for dt in out_dtypes],
        compiler_params=_cparams(("arbitrary",)),
        name="norm_mod",
    )(x2, gain.reshape(1, D), sc.reshape(B, 1, D), sh.reshape(B, 1, D))
    return outs


def _mm_kernel(a_ref, w_ref, o_ref):
    o_ref[...] = jnp.dot(a_ref[...], w_ref[...], preferred_element_type=F32)


def _matmul(a, w, tm=512, tn=512):
    M, K = a.shape
    N = w.shape[1]
    return pl.pallas_call(
        _mm_kernel,
        grid=(M // tm, N // tn),
        in_specs=[pl.BlockSpec((tm, K), lambda i, j: (i, 0)),
                  pl.BlockSpec((K, tn), lambda i, j: (0, j))],
        out_specs=pl.BlockSpec((tm, tn), lambda i, j: (i, j)),
        out_shape=jax.ShapeDtypeStruct((M, N), F32),
        compiler_params=_cparams(("arbitrary", "arbitrary")),
        name="matmul",
    )(a, w)


def _cast_kernel3(x_ref, o_ref):
    o_ref[...] = x_ref[0].astype(o_ref.dtype)


def _to_bf16(w_stack, layer, rows=256):
    _, K, N = w_stack.shape
    rows = min(rows, K)
    return pl.pallas_call(
        _cast_kernel3,
        grid=(K // rows,),
        in_specs=[pl.BlockSpec((1, rows, N), lambda i: (layer, i, 0))],
        out_specs=pl.BlockSpec((rows, N), lambda i: (i, 0)),
        out_shape=jax.ShapeDtypeStruct((K, N), BF16),
        compiler_params=_cparams(("arbitrary",)),
        name="cast_bf16",
    )(w_stack)


def _rope_tables(positions, d):
    half = d // 2
    inv = ROPE_THETA ** (-jnp.arange(half, dtype=F32) * 2.0 / d)
    ang = positions.astype(F32).reshape(-1, 1) * inv
    cos, sin = jnp.cos(ang), jnp.sin(ang)
    reps = V7X_LANES // d
    cosf = jnp.concatenate([cos, cos] * reps, axis=-1)
    sinf = jnp.concatenate([-sin, sin] * reps, axis=-1)
    return cosf, sinf


def _ret_consts():
    L, H = RET_CHUNK, RET_HEADS
    log_g = np.log(1.0 - 2.0 ** (-5.0 - np.arange(H, dtype=np.float64)))
    idx = np.arange(L, dtype=np.float64)
    diff = idx[:, None] - idx[None, :]
    decay = np.where(diff >= 0, np.exp(np.maximum(diff, 0.0)[None] * log_g[:, None, None]), 0.0)
    w_k = np.exp((L - 1 - idx)[None, :] * log_g[:, None])
    w_q = np.exp((idx + 1)[None, :] * log_g[:, None])
    gam = np.exp(L * log_g)
    wk_full = np.broadcast_to(w_k[:, :, None], (H, L, RET_DK))
    wq_full = np.broadcast_to(w_q[:, :, None], (H, L, RET_DK))
    gam_full = np.broadcast_to(gam[:, None, None], (H, 1, RET_DV))
    return (jnp.asarray(decay, F32), jnp.asarray(wq_full, F32), jnp.asarray(wk_full, F32),
            jnp.asarray(gam_full, F32))


def _ret_kernel(q_ref, k_ref, v_ref, g_ref, cos_ref, sin_ref, dec_ref, wq_ref, wk_ref, gam_ref,
                o_ref, r_ref):
    n = pl.program_id(1)

    @pl.when(n == 0)
    def _():
        r_ref[...] = jnp.zeros_like(r_ref)

    cos = cos_ref[...]
    sin = sin_ref[...]
    for h in range(RET_HEADS):
        ks = slice(h * RET_DK, (h + 1) * RET_DK)
        vs = slice(h * RET_DV, (h + 1) * RET_DV)
        q = q_ref[:, ks]
        k = k_ref[:, ks]
        q = q * cos + pltpu.roll(q, RET_DK // 2, 1) * sin
        k = (k * cos + pltpu.roll(k, RET_DK // 2, 1) * sin) * (RET_DK ** -0.5)
        vb = v_ref[:, vs].astype(BF16)
        s = lax.dot_general(q.astype(BF16), k.astype(BF16), (((1,), (1,)), ((), ())),
                            preferred_element_type=F32) * dec_ref[h]
        o = jnp.dot(s.astype(BF16), vb, preferred_element_type=F32)
        r = r_ref[h]
        o = o + jnp.dot((q * wq_ref[h]).astype(BF16), r.astype(BF16), preferred_element_type=F32)
        kv = lax.dot_general((k * wk_ref[h]).astype(BF16), vb, (((0,), (0,)), ((), ())),
                             preferred_element_type=F32)
        r_ref[h] = gam_ref[h] * r + kv
        mu = jnp.mean(o, axis=-1, keepdims=True)
        oc = o - mu
        var = jnp.mean(oc * oc, axis=-1, keepdims=True)
        g = g_ref[:, vs]
        o_ref[:, vs] = g * _sigmoid(g) * (oc * lax.rsqrt(var + EPS))


def _retention(proj, cosf, sinf, batch, seq):
    T = proj.shape[0]
    L = RET_CHUNK
    n = seq // L
    H = RET_HEADS
    dec, wq, wk, gam = _ret_consts()
    kw, vw = H * RET_DK, H * RET_DV
    const3 = lambda b, c: (0, 0, 0)
    return pl.pallas_call(
        _ret_kernel,
        grid=(batch, n),
        in_specs=[
            pl.BlockSpec((L, kw), lambda b, c: (b * n + c, OFF_QR // kw)),
            pl.BlockSpec((L, kw), lambda b, c: (b * n + c, OFF_KR // kw)),
            pl.BlockSpec((L, vw), lambda b, c: (b * n + c, OFF_VR // vw)),
            pl.BlockSpec((L, vw), lambda b, c: (b * n + c, OFF_GR // vw)),
            pl.BlockSpec((L, RET_DK), lambda b, c: (b * n + c, 0)),
            pl.BlockSpec((L, RET_DK), lambda b, c: (b * n + c, 0)),
            pl.BlockSpec((H, L, L), const3),
            pl.BlockSpec((H, L, RET_DK), const3),
            pl.BlockSpec((H, L, RET_DK), const3),
            pl.BlockSpec((H, 1, RET_DV), const3),
        ],
        out_specs=pl.BlockSpec((L, vw), lambda b, c: (b * n + c, 0)),
        out_shape=jax.ShapeDtypeStruct((T, vw), F32),
        scratch_shapes=[pltpu.VMEM((H, RET_DK, RET_DV), F32)],
        compiler_params=_cparams(("arbitrary", "arbitrary")),
        name="retention",
    )(proj, proj, proj, proj, cosf, sinf, dec, wq, wk, gam)


def _ssm_params(a_re, a_im, log_dt, b_re, b_im, c_re, c_im):
    dt = jnp.exp(log_dt)[:, None]
    mag = jnp.exp(a_re * dt)
    ab_re = mag * jnp.cos(a_im * dt)
    ab_im = mag * jnp.sin(a_im * dt)
    den = a_re * a_re + a_im * a_im
    nr = ab_re - 1.0
    f_re = (nr * a_re + ab_im * a_im) / den
    f_im = (ab_im * a_re - nr * a_im) / den
    bb_re = f_re[..., None] * b_re - f_im[..., None] * b_im
    bb_im = f_re[..., None] * b_im + f_im[..., None] * b_re
    nblk = SSM_WIDTH // SSM_COLS
    gpb = SSM_COLS // SSM_GROUP
    eye = jnp.eye(gpb, dtype=F32)

    def blockdiag_in(bb):
        t = bb.reshape(nblk, gpb, SSM_STATE, SSM_GROUP)
        m = jnp.einsum('ngpc,gh->ngchp', t, eye)
        return m.reshape(nblk, SSM_COLS, SSM_CH)

    def blockdiag_out(cc):
        t = cc.reshape(nblk, gpb, SSM_GROUP, SSM_STATE)
        m = jnp.einsum('ngcp,gh->ngphc', t, eye)
        return m.reshape(nblk, SSM_CH, SSM_COLS)

    rounds = int(math.log2(SSM_CHUNK))
    pr, pi = [ab_re], [ab_im]
    for _ in range(rounds - 1):
        r, i = pr[-1], pi[-1]
        pr.append(r * r - i * i)
        pi.append(2.0 * r * i)
    apr = jnp.stack(pr, 0).reshape(rounds, nblk, SSM_CH).transpose(1, 0, 2)
    api = jnp.stack(pi, 0).reshape(rounds, nblk, SSM_CH).transpose(1, 0, 2)
    qr, qi = [ab_re], [ab_im]
    for _ in range(V7X_SUBLANES - 1):
        r, i = qr[-1], qi[-1]
        qr.append(r * ab_re - i * ab_im)
        qi.append(r * ab_im + i * ab_re)
    ppr = jnp.stack(qr, 0).reshape(V7X_SUBLANES, nblk, SSM_CH).transpose(1, 0, 2)
    ppi = jnp.stack(qi, 0).reshape(V7X_SUBLANES, nblk, SSM_CH).transpose(1, 0, 2)
    return (blockdiag_in(bb_re).astype(BF16), blockdiag_in(bb_im).astype(BF16),
            blockdiag_out(c_re).astype(BF16), blockdiag_out(c_im).astype(BF16), apr, api, ppr, ppi)


def _cmul_add(xr, xi, p_r, p_i, sr, si):
    return xr + p_r * sr - p_i * si, xi + p_r * si + p_i * sr


def _ssm_kernel(u_ref, bre_ref, bim_ref, cre_ref, cim_ref, apr_ref, api_ref, ppr_ref, ppi_ref, d_ref, o_ref,
                cr_ref, ci_ref, xr_s, xi_s, hr_s, hi_s):
    n = pl.program_id(2)
    hs = V7X_SUBLANES
    ng = SSM_CHUNK // hs

    @pl.when(n == 0)
    def _():
        cr_ref[...] = jnp.zeros_like(cr_ref)
        ci_ref[...] = jnp.zeros_like(ci_ref)

    u = u_ref[...]
    ub = u.astype(BF16)
    xr = jnp.dot(ub, bre_ref[0], preferred_element_type=F32)
    xi = jnp.dot(ub, bim_ref[0], preferred_element_type=F32)
    sub = jnp.bitwise_and(lax.broadcasted_iota(jnp.int32, xr.shape, 0), hs - 1)
    in_rounds = int(math.log2(hs))
    for k in range(in_rounds):
        s = 1 << k
        keep = sub >= s
        sr = jnp.where(keep, pltpu.roll(xr, s, 0), 0.0)
        si = jnp.where(keep, pltpu.roll(xi, s, 0), 0.0)
        xr, xi = _cmul_add(xr, xi, apr_ref[0, k:k + 1, :], api_ref[0, k:k + 1, :], sr, si)
    nlt = SSM_CH // V7X_LANES
    for q in range(nlt):
        xr_s[q] = xr[:, q * V7X_LANES:(q + 1) * V7X_LANES]
        xi_s[q] = xi[:, q * V7X_LANES:(q + 1) * V7X_LANES]

    er = jnp.concatenate([xr_s[q, pl.ds(hs - 1, ng, stride=hs), :] for q in range(nlt)], axis=1)
    ei = jnp.concatenate([xi_s[q, pl.ds(hs - 1, ng, stride=hs), :] for q in range(nlt)], axis=1)
    c_r = cr_ref[...]
    c_i = ci_ref[...]
    grow = lax.broadcasted_iota(jnp.int32, er.shape, 0)
    first = grow == 0
    a8r = apr_ref[0, in_rounds:in_rounds + 1, :]
    a8i = api_ref[0, in_rounds:in_rounds + 1, :]
    er = er + jnp.where(first, a8r * c_r - a8i * c_i, 0.0)
    ei = ei + jnp.where(first, a8r * c_i + a8i * c_r, 0.0)
    for k in range(int(math.log2(ng))):
        s = 1 << k
        keep = grow >= s
        sr = jnp.where(keep, pltpu.roll(er, s, 0), 0.0)
        si = jnp.where(keep, pltpu.roll(ei, s, 0), 0.0)
        kk = in_rounds + k
        er, ei = _cmul_add(er, ei, apr_ref[0, kk:kk + 1, :], api_ref[0, kk:kk + 1, :], sr, si)
    cr_ref[...] = er[ng - 1:ng, :]
    ci_ref[...] = ei[ng - 1:ng, :]
    hr_s[...] = jnp.where(first, c_r, pltpu.roll(er, 1, 0))
    hi_s[...] = jnp.where(first, c_i, pltpu.roll(ei, 1, 0))
    p_r = ppr_ref[0]
    p_i = ppi_ref[0]
    for g in range(ng):
        rows = slice(g * hs, (g + 1) * hs)
        for q in range(nlt):
            ls = slice(q * V7X_LANES, (q + 1) * V7X_LANES)
            hb_r = jnp.broadcast_to(hr_s[g:g + 1, ls], (hs, V7X_LANES))
            hb_i = jnp.broadcast_to(hi_s[g:g + 1, ls], (hs, V7X_LANES))
            yr, yi = _cmul_add(xr_s[q, rows, :], xi_s[q, rows, :], p_r[:, ls], p_i[:, ls], hb_r, hb_i)
            xr_s[q, rows, :] = yr
            xi_s[q, rows, :] = yi
    hr = jnp.concatenate([xr_s[q] for q in range(nlt)], axis=1)
    hi = jnp.concatenate([xi_s[q] for q in range(nlt)], axis=1)
    y = (jnp.dot(hr.astype(BF16), cre_ref[0], preferred_element_type=F32)
         - jnp.dot(hi.astype(BF16), cim_ref[0], preferred_element_type=F32))
    o_ref[...] = _gelu(y + d_ref[...] * u)


def _ssm_scan(proj, params, d, batch, seq):
    T = proj.shape[0]
    bre, bim, cre, cim, apr, api, ppr, ppi = params
    nblk = SSM_WIDTH // SSM_COLS
    nt = seq // SSM_CHUNK
    rounds = apr.shape[1]
    ngroups = SSM_CHUNK // V7X_SUBLANES
    wspec_in = pl.BlockSpec((1, SSM_COLS, SSM_CH), lambda b, c, n: (c, 0, 0))
    wspec_out = pl.BlockSpec((1, SSM_CH, SSM_COLS), lambda b, c, n: (c, 0, 0))
    pspec = pl.BlockSpec((1, rounds, SSM_CH), lambda b, c, n: (c, 0, 0))
    qspec = pl.BlockSpec((1, V7X_SUBLANES, SSM_CH), lambda b, c, n: (c, 0, 0))
    return pl.pallas_call(
        _ssm_kernel,
        grid=(batch, nblk, nt),
        in_specs=[pl.BlockSpec((SSM_CHUNK, SSM_COLS), lambda b, c, n: (b * nt + n, OFF_US // SSM_COLS + c)),
                  wspec_in, wspec_in, wspec_out, wspec_out, pspec, pspec, qspec, qspec,
                  pl.BlockSpec((1, SSM_COLS), lambda b, c, n: (0, c))],
        out_specs=pl.BlockSpec((SSM_CHUNK, SSM_COLS), lambda b, c, n: (b * nt + n, c)),
        out_shape=jax.ShapeDtypeStruct((T, SSM_WIDTH), F32),
        scratch_shapes=[pltpu.VMEM((1, SSM_CH), F32), pltpu.VMEM((1, SSM_CH), F32),
                        pltpu.VMEM((SSM_CH // V7X_LANES, SSM_CHUNK, V7X_LANES), F32),
                        pltpu.VMEM((SSM_CH // V7X_LANES, SSM_CHUNK, V7X_LANES), F32),
                        pltpu.VMEM((ngroups, SSM_CH), F32), pltpu.VMEM((ngroups, SSM_CH), F32)],
        compiler_params=_cparams(("arbitrary", "arbitrary", "arbitrary")),
        name="ssm_scan",
    )(proj, bre, bim, cre, cim, apr, api, ppr, ppi, d.reshape(1, SSM_WIDTH))


def _glu_kernel(y_ref, w_ref, b_ref, o_ref):
    y = y_ref[...]
    z = jnp.dot(y.astype(BF16), w_ref[...], preferred_element_type=F32) + b_ref[...]
    o_ref[...] = y * _sigmoid(z)


def _glu(y, w, b):
    T, W = y.shape
    tm = 512
    return pl.pallas_call(
        _glu_kernel,
        grid=(T // tm,),
        in_specs=[pl.BlockSpec((tm, W), lambda i: (i, 0)),
                  pl.BlockSpec((W, W), lambda i: (0, 0)),
                  pl.BlockSpec((1, W), lambda i: (0, 0))],
        out_specs=pl.BlockSpec((tm, W), lambda i: (i, 0)),
        out_shape=jax.ShapeDtypeStruct((T, W), F32),
        compiler_params=_cparams(("arbitrary",)),
        name="ssm_glu",
    )(y, w, b.reshape(1, W))


def _swa_kernel(sink_ref, q_ref, k_ref, v_ref, cos_ref, sin_ref, qn_ref, kn_ref, o_ref, pk_ref, pv_ref):
    n = pl.program_id(1)
    W = SWA_WINDOW
    hd = SWA_HEAD_DIM

    @pl.when(n == 0)
    def _():
        pk_ref[...] = jnp.zeros_like(pk_ref)
        pv_ref[...] = jnp.zeros_like(pv_ref)

    cos = cos_ref[...]
    sin = sin_ref[...]
    lane = lax.broadcasted_iota(jnp.int32, (W, V7X_LANES), 1)
    lo = lane < hd
    first_half = (lane % hd) < (hd // 2)

    def head_norm(t, gain):
        tt = t * t
        s_all = jnp.sum(tt, axis=-1, keepdims=True)
        s_lo = jnp.sum(jnp.where(lo, tt, 0.0), axis=-1, keepdims=True)
        ms = jnp.where(lo, s_lo, s_all - s_lo) * (1.0 / hd)
        return t * lax.rsqrt(ms + EPS) * gain

    def rope(t):
        sw = jnp.where(first_half, pltpu.roll(t, V7X_LANES - hd // 2, 1), pltpu.roll(t, hd // 2, 1))
        return t * cos + sw * sin

    kc = rope(head_norm(k_ref[...], kn_ref[...]))
    vc = v_ref[...]
    keys = jnp.concatenate([pk_ref[...], kc], axis=0)
    vals = jnp.concatenate([pv_ref[...], vc], axis=0)
    lane2 = lax.broadcasted_iota(jnp.int32, (2 * W, V7X_LANES), 1)
    lo2 = lane2 < hd

    def dup(x, h):
        sw = pltpu.roll(x, hd, 1)
        return (jnp.where(lo2, x, sw) if h == 0 else jnp.where(lo2, sw, x)).astype(BF16)

    kk = [dup(keys, h) for h in range(SWA_KV_HEADS)]
    vv = [dup(vals, h) for h in range(SWA_KV_HEADS)]
    qi = lax.broadcasted_iota(jnp.int32, (W, 2 * W), 0)
    ci = lax.broadcasted_iota(jnp.int32, (W, 2 * W), 1)
    prev_floor = qi + jnp.where(n > 0, 0, 2 * W)
    valid = jnp.logical_or(jnp.logical_and(ci < W, ci > prev_floor),
                           jnp.logical_and(ci >= W, (ci - W) <= qi))
    tiles = SWA_Q_HEADS * hd // V7X_LANES
    per_kv = tiles // SWA_KV_HEADS
    for m in range(tiles):
        sl = slice(m * V7X_LANES, (m + 1) * V7X_LANES)
        qt = rope(head_norm(q_ref[:, sl], qn_ref[:, sl]))
        h = m // per_kv
        outs = []
        for half in range(2):
            sel = lo if half == 0 else jnp.logical_not(lo)
            qm = jnp.where(sel, qt, 0.0).astype(BF16)
            s = lax.dot_general(qm, kk[h], (((1,), (1,)), ((), ())),
                                preferred_element_type=F32) * (hd ** -0.5)
            s = jnp.where(valid, s, -1e30)
            sink = sink_ref[2 * m + half]
            mx = jnp.maximum(jnp.max(s, axis=-1, keepdims=True), sink)
            p = jnp.exp(s - mx)
            den = jnp.sum(p, axis=-1, keepdims=True) + jnp.exp(sink - mx)
            p = p / den
            outs.append(jnp.dot(p.astype(BF16), vv[h], preferred_element_type=F32))
        o_ref[:, sl] = jnp.where(lo, outs[0], outs[1])
    pk_ref[...] = kc
    pv_ref[...] = vc


def _swa(proj, cosf, sinf, q_norm, k_norm, sinks, batch, seq):
    T = proj.shape[0]
    W = SWA_WINDOW
    nb = seq // W
    qw = SWA_Q_HEADS * SWA_HEAD_DIM
    qn = jnp.tile(q_norm, SWA_Q_HEADS).reshape(1, qw)
    kn = jnp.tile(k_norm, SWA_KV_HEADS).reshape(1, V7X_LANES)
    return pl.pallas_call(
        _swa_kernel,
        grid=(batch, nb),
        in_specs=[pl.BlockSpec(memory_space=pltpu.SMEM),
                  pl.BlockSpec((W, qw), lambda b, n: (b * nb + n, OFF_QA // qw)),
                  pl.BlockSpec((W, V7X_LANES), lambda b, n: (b * nb + n, OFF_KA // V7X_LANES)),
                  pl.BlockSpec((W, V7X_LANES), lambda b, n: (b * nb + n, OFF_VA // V7X_LANES)),
                  pl.BlockSpec((W, V7X_LANES), lambda b, n: (b * nb + n, 0)),
                  pl.BlockSpec((W, V7X_LANES), lambda b, n: (b * nb + n, 0)),
                  pl.BlockSpec((1, qw), lambda b, n: (0, 0)),
                  pl.BlockSpec((1, V7X_LANES), lambda b, n: (0, 0))],
        out_specs=pl.BlockSpec((W, qw), lambda b, n: (b * nb + n, 0)),
        out_shape=jax.ShapeDtypeStruct((T, qw), F32),
        scratch_shapes=[pltpu.VMEM((W, V7X_LANES), F32), pltpu.VMEM((W, V7X_LANES), F32)],
        compiler_params=_cparams(("arbitrary", "arbitrary")),
        name="swa",
    )(sinks, proj, proj, proj, cosf, sinf, qn, kn)


POOL_CHUNK = 256


def _pool_kernel(u0_ref, u1_ref, u2_ref, u3_ref, w_ref, sc_ref, o_ref, ext_ref):
    n = pl.program_id(1)
    Lc = POOL_CHUNK

    @pl.when(n == 0)
    def _():
        ext_ref[:, 0:POOL_MAXW, :] = jnp.zeros((len(POOL_WINDOWS), POOL_MAXW, POOL_GROUP), F32)

    t = n * Lc + lax.broadcasted_iota(jnp.int32, (Lc, POOL_GROUP), 0)
    for g, (u_ref, w) in enumerate(zip((u0_ref, u1_ref, u2_ref, u3_ref), POOL_WINDOWS)):
        u = u_ref[...]
        ext_ref[g, POOL_MAXW:POOL_MAXW + Lc, :] = u
        acc = u
        for k in range(1, w):
            acc = acc + ext_ref[g, POOL_MAXW - k:POOL_MAXW - k + Lc, :]
        cnt = jnp.minimum(t + 1, w).astype(F32)
        pooled = acc / cnt - u
        y = jnp.dot(pooled.astype(BF16), w_ref[g], preferred_element_type=F32)
        sl = slice(g * POOL_GROUP, (g + 1) * POOL_GROUP)
        o_ref[:, sl] = y * sc_ref[:, sl]
        ext_ref[g, 0:POOL_MAXW, :] = u[Lc - POOL_MAXW:, :]


def _pool(proj, pool_w, pool_scale, batch, seq):
    T = proj.shape[0]
    Lc = POOL_CHUNK
    nt = seq // Lc
    ng = len(POOL_WINDOWS)
    width = ng * POOL_GROUP
    uspecs = [pl.BlockSpec((Lc, POOL_GROUP), functools.partial(
        lambda b, n, g: (b * nt + n, OFF_UP // POOL_GROUP + g), g=g)) for g in range(ng)]
    return pl.pallas_call(
        _pool_kernel,
        grid=(batch, nt),
        in_specs=uspecs + [pl.BlockSpec((ng, POOL_GROUP, POOL_GROUP), lambda b, n: (0, 0, 0)),
                           pl.BlockSpec((1, width), lambda b, n: (0, 0))],
        out_specs=pl.BlockSpec((Lc, width), lambda b, n: (b * nt + n, 0)),
        out_shape=jax.ShapeDtypeStruct((T, width), F32),
        scratch_shapes=[pltpu.VMEM((ng, POOL_MAXW + Lc, POOL_GROUP), F32)],
        compiler_params=_cparams(("arbitrary", "arbitrary")),
        name="pool",
    )(proj, proj, proj, proj, pool_w.astype(BF16), pool_scale.reshape(1, width))


def _out_kernel(r_ref, s_ref, a_ref, p_ref, gain_ref, w_ref, x_ref, g1_ref, o_ref, mix_ref):
    j = pl.program_id(1)

    @pl.when(j == 0)
    def _():
        for idx, ref in enumerate((r_ref, s_ref, a_ref, p_ref)):
            v = ref[...]
            width = v.shape[1]
            sl = slice(idx * width, (idx + 1) * width)
            ms = jnp.mean(v * v, axis=-1, keepdims=True)
            mix_ref[:, sl] = (v * lax.rsqrt(ms + EPS) * gain_ref[:, sl]).astype(BF16)

    acc = jnp.dot(mix_ref[...], w_ref[...], preferred_element_type=F32)
    o_ref[...] = x_ref[...] + g1_ref[0] * acc


def _out_proj(branches, gain, w_out_bf, x2, g1, seq):
    T, D = x2.shape
    B = g1.shape[0]
    tm, tn = 512, 512
    per_b = seq // tm
    width = branches[0].shape[1]
    bspec = pl.BlockSpec((tm, width), lambda i, j: (i, 0))
    return pl.pallas_call(
        _out_kernel,
        grid=(T // tm, D // tn),
        in_specs=[bspec, bspec, bspec, bspec,
                  pl.BlockSpec((1, D), lambda i, j: (0, 0)),
                  pl.BlockSpec((D, tn), lambda i, j: (0, j)),
                  pl.BlockSpec((tm, tn), lambda i, j: (i, j)),
                  pl.BlockSpec((1, 1, tn), lambda i, j: (i // per_b, 0, j))],
        out_specs=pl.BlockSpec((tm, tn), lambda i, j: (i, j)),
        out_shape=jax.ShapeDtypeStruct((T, D), F32),
        scratch_shapes=[pltpu.VMEM((tm, D), BF16)],
        compiler_params=_cparams(("arbitrary", "arbitrary")),
        name="out_proj",
    )(*branches, gain.reshape(1, D), w_out_bf, x2, g1.reshape(B, 1, D))


PEER_TOPK_TB = 128


def _iter_topk(s, k):
    n = s.shape[0]
    rows = lax.broadcasted_iota(jnp.int32, s.shape, 0).astype(F32)
    vals, idxs = [], []
    for _ in range(k):
        m = jnp.max(s, axis=0, keepdims=True)
        idx = jnp.min(jnp.where(s == m, rows, float(n)), axis=0, keepdims=True)
        s = jnp.where(rows == idx, -jnp.inf, s)
        vals.append(m)
        idxs.append(idx)
    return jnp.concatenate(vals, axis=0), jnp.concatenate(idxs, axis=0)


def _topk_kernel(q_ref, keys_ref, e_ref, g_ref):
    K = PEER_TOPK
    half = PEER_N_KEYS
    e_all, g_all = [], []
    for h in range(PEER_HEADS):
        tops = []
        for p in range(2):
            c0 = h * 2 * half + p * half
            qhp = q_ref[:, c0:c0 + half].astype(BF16)
            s = lax.dot_general(keys_ref[p], qhp, (((1,), (1,)), ((), ())), preferred_element_type=F32)
            tops.append(_iter_topk(s, K))
        (s0, i0), (s1, i1) = tops
        hs = V7X_SUBLANES
        io8 = lax.broadcasted_iota(jnp.int32, (hs, s0.shape[1]), 0).astype(F32)
        ps, pe, pp = [], [], []
        for a, b0 in [(0, 0), (0, hs)] + [(a, 0) for a in range(1, hs)]:
            ps.append(s0[a:a + 1, :] + s1[b0:b0 + hs, :])
            pe.append(i0[a:a + 1, :] * PEER_N_KEYS + i1[b0:b0 + hs, :])
            pp.append(io8 + float(a * K + b0))
        ps.append(s0[hs:K, :] + s1[0:1, :])
        pe.append(i0[hs:K, :] * PEER_N_KEYS + i1[0:1, :])
        pp.append((io8 + float(hs)) * float(K))
        cand_s = jnp.concatenate(ps, axis=0)
        cand_e = jnp.concatenate(pe, axis=0)
        rows = jnp.concatenate(pp, axis=0)
        bs, be = [], []
        for _ in range(K):
            m = jnp.max(cand_s, axis=0, keepdims=True)
            pos = jnp.min(jnp.where(cand_s == m, rows, float(K * K)), axis=0, keepdims=True)
            hit = rows == pos
            be.append(jnp.sum(jnp.where(hit, cand_e, 0.0), axis=0, keepdims=True))
            cand_s = jnp.where(hit, -jnp.inf, cand_s)
            bs.append(m)
        best = jnp.concatenate(bs, axis=0)
        pexp = jnp.exp(best - best[0:1, :])
        g_all.append(pexp / jnp.sum(pexp, axis=0, keepdims=True))
        e_all.append(jnp.concatenate(be, axis=0))
    e_ref[...] = jnp.concatenate(e_all, axis=0).T.astype(jnp.int32)
    g_ref[...] = jnp.concatenate(g_all, axis=0).T


def _peer_topk(q, sub_keys_bf):
    T, QW = q.shape
    tb = PEER_TOPK_TB
    return pl.pallas_call(
        _topk_kernel,
        grid=(T // tb,),
        in_specs=[pl.BlockSpec((tb, QW), lambda i: (i, 0)),
                  pl.BlockSpec((2, PEER_N_KEYS, PEER_N_KEYS), lambda i: (0, 0, 0))],
        out_specs=[pl.BlockSpec((tb, PEER_HK), lambda i: (i, 0)),
                   pl.BlockSpec((tb, PEER_HK), lambda i: (i, 0))],
        out_shape=[jax.ShapeDtypeStruct((T, PEER_HK), jnp.int32),
                   jax.ShapeDtypeStruct((T, PEER_HK), F32)],
        compiler_params=_cparams(("arbitrary",)),
        name="peer_topk",
    )(q, sub_keys_bf)


PEER_PACK_ROWS = 64


def _pack_kernel(u_ref, v_ref, o_hbm, word_ref, sem):
    i = pl.program_id(0)
    n = pl.num_programs(0)
    slot = i % 2

    def row_copy(step, s):
        return pltpu.make_async_copy(word_ref.at[s], o_hbm.at[pl.ds(step * PEER_PACK_ROWS, PEER_PACK_ROWS), 0, :],
                                     sem.at[s])

    @pl.when(i >= 2)
    def _():
        row_copy(i - 2, slot).wait()

    ub = lax.bitcast_convert_type(u_ref[0].astype(BF16).astype(F32), jnp.int32)
    vb = lax.bitcast_convert_type(v_ref[0].astype(BF16).astype(F32), jnp.int32)
    word_ref[slot] = jnp.bitwise_or(lax.shift_right_logical(ub, 16), vb)
    row_copy(i, slot).start()

    @pl.when(i == n - 1)
    def _():
        row_copy(i, slot).wait()

        @pl.when(n >= 2)
        def _():
            row_copy(i - 1, 1 - slot).wait()


def _pack_tables(u_stack, v_stack, layer):
    _, E, D = u_stack.shape
    R = PEER_PACK_ROWS
    spec = pl.BlockSpec((1, R, D), lambda i: (layer, i, 0))
    return pl.pallas_call(
        _pack_kernel,
        grid=(E // R,),
        in_specs=[spec, spec],
        out_specs=pl.BlockSpec(memory_space=pl.ANY),
        out_shape=jax.ShapeDtypeStruct((E, V7X_SUBLANES, D), jnp.int32),
        scratch_shapes=[pltpu.VMEM((2, R, D), jnp.int32), pltpu.SemaphoreType.DMA((2,))],
        compiler_params=_cparams(("arbitrary",)),
        name="peer_pack",
    )(u_stack, v_stack)


def _peer_kernel(ec_ref, en_ref, h_ref, gate_ref, x_ref, g2_ref, tab_hbm, o_ref,
                 buf0, buf1, acc_ref, cmat_ref, sem0, sem1):
    i = pl.program_id(0)
    _peer_half(i, 0, buf0, sem0, buf1, sem1, ec_ref, PEER_TB, ec_ref, h_ref, gate_ref,
               tab_hbm, o_ref, acc_ref, cmat_ref)
    _peer_half(i, PEER_TB, buf1, sem1, buf0, sem0, en_ref, 0, ec_ref, h_ref, gate_ref,
               tab_hbm, o_ref, acc_ref, cmat_ref)
    o_ref[...] = x_ref[...] + g2_ref[0] * o_ref[...]


def _peer_half(i, row0, buf, csem, nbuf, nsem, nidx_ref, nrow0, ec_ref, h_ref, gate_ref, tab_hbm, o_ref,
               acc_ref, cmat_ref):
    D = h_ref.shape[1]
    nch = D // V7X_LANES
    half_ch = nch // 2
    G = PEER_GROUPS
    cur = 0
    nxt = 0
    zero = jnp.zeros((V7X_SUBLANES, V7X_LANES), F32)

    def start_group(idx_ref, r, g, slot, dst=nbuf, dsem=nsem):
        for s in range(V7X_SUBLANES):
            e = idx_ref[r * PEER_HK + g * V7X_SUBLANES + s]
            pltpu.make_async_copy(tab_hbm.at[e, pl.ds(0, 1), :], dst.at[slot, g, pl.ds(s, 1), :],
                                  dsem.at[slot]).start()

    def wait_slot(slot, b=buf, bsem=csem):
        pltpu.make_async_copy(b.at[slot], b.at[slot], bsem.at[slot]).wait()

    if row0 == 0:
        @pl.when(i == 0)
        def _():
            for tt in range(PEER_TB):
                def body(g, carry, tt=tt):
                    start_group(ec_ref, tt, g, tt, dst=buf, dsem=csem)
                    return carry
                lax.fori_loop(0, G, body, 0)

    for tt in range(PEER_TB):
        wait_slot(cur + tt)

        def s1_body(j, carry, tt=tt):
            parts = [[None] * PEER_S1_CHAINS for _ in range(PEER_S1_GROUPS)]
            for c in range(nch):
                cs = slice(c * V7X_LANES, (c + 1) * V7X_LANES)
                hb = jnp.broadcast_to(h_ref[row0 + tt:row0 + tt + 1, cs], (V7X_SUBLANES, V7X_LANES))
                for q in range(PEER_S1_GROUPS):
                    w = buf[cur + tt, j * PEER_S1_GROUPS + q, :, cs]
                    t = lax.bitcast_convert_type(jnp.left_shift(w, 16), F32) * hb
                    p = parts[q][c % PEER_S1_CHAINS]
                    parts[q][c % PEER_S1_CHAINS] = t if p is None else p + t
            for q in range(PEER_S1_GROUPS):
                r0 = pl.multiple_of((tt * G + j * PEER_S1_GROUPS + q) * V7X_SUBLANES, V7X_SUBLANES)
                acc_ref[pl.ds(r0, V7X_SUBLANES), :] = (parts[q][0] + parts[q][1]) + (parts[q][2] + parts[q][3])
            for d in range(PEER_S1_GROUPS // 2):
                start_group(nidx_ref, nrow0 + tt, j * (PEER_S1_GROUPS // 2) + d, nxt + tt)
            return carry

        lax.fori_loop(0, G // PEER_S1_GROUPS, s1_body, 0)

    ones_row = jnp.ones((V7X_SUBLANES, V7X_LANES), BF16)
    ones_sq = jnp.ones((V7X_LANES, V7X_LANES), BF16)
    nt_dims = (((1,), (1,)), ((), ()))
    acc = acc_ref[...]
    acc_hi = acc.astype(BF16)
    acc_lo = (acc - acc_hi.astype(F32)).astype(BF16)
    a8 = (lax.dot_general(ones_row, acc_hi, nt_dims, preferred_element_type=F32)
          + lax.dot_general(ones_row, acc_lo, nt_dims, preferred_element_type=F32))
    a_tok = jnp.concatenate([a8[0:1, tt * PEER_HK:(tt + 1) * PEER_HK] for tt in range(PEER_TB)], axis=0)
    c_tok = gate_ref[row0:row0 + PEER_TB, :] * _gelu(a_tok)
    r_i = lax.broadcasted_iota(jnp.int32, (PEER_HK, V7X_LANES), 0)
    c_i = lax.broadcasted_iota(jnp.int32, (PEER_HK, V7X_LANES), 1)
    eye = r_i == c_i
    for tt in range(PEER_TB):
        c_diag = jnp.where(eye, jnp.broadcast_to(c_tok[tt:tt + 1, :], (PEER_HK, V7X_LANES)), 0.0)
        d_hi = c_diag.astype(BF16)
        d_lo = (c_diag - d_hi.astype(F32)).astype(BF16)
        cmat_ref[tt * PEER_HK:(tt + 1) * PEER_HK, :] = (
            jnp.dot(d_hi, ones_sq, preferred_element_type=F32) + jnp.dot(d_lo, ones_sq, preferred_element_type=F32))

    for tt in range(PEER_TB):
        for half in range(2):
            def s2_body(j, accs, tt=tt, half=half):
                new = list(accs)
                for q in range(PEER_S2_GROUPS):
                    g = j * PEER_S2_GROUPS + q
                    r0 = pl.multiple_of((tt * G + g) * V7X_SUBLANES, V7X_SUBLANES)
                    cm = cmat_ref[pl.ds(r0, V7X_SUBLANES), :]
                    for cc in range(half_ch):
                        c = half * half_ch + cc
                        w = buf[cur + tt, g, :, c * V7X_LANES:(c + 1) * V7X_LANES]
                        v = lax.bitcast_convert_type(jnp.bitwise_and(w, jnp.int32(-65536)), F32)
                        new[cc] = new[cc] + cm * v
                start_group(nidx_ref, nrow0 + tt, G // 2 + half * (G // 4) + j, nxt + tt)
                return tuple(new)

            accs = lax.fori_loop(0, G // PEER_S2_GROUPS, s2_body, tuple(zero for _ in range(half_ch)))
            for cc in range(half_ch):
                c = half * half_ch + cc
                o_ref[row0 + tt:row0 + tt + 1, c * V7X_LANES:(c + 1) * V7X_LANES] = jnp.sum(
                    accs[cc], axis=0, keepdims=True)

    if row0 != 0:
        @pl.when(i == pl.num_programs(0) - 1)
        def _():
            for tt in range(PEER_TB):
                wait_slot(tt, b=nbuf, bsem=nsem)


def _peer_gather(experts, gates, h32, x2, g2, uv_tab, seq):
    T, D = x2.shape
    B = g2.shape[0]
    tb = 2 * PEER_TB
    per_b = seq // tb
    nsteps = T // tb
    G = PEER_GROUPS
    assert PEER_S1_GROUPS % 2 == 0 and G % PEER_S1_GROUPS == 0 and G // PEER_S2_GROUPS == G // 4
    assert PEER_S1_CHAINS == 4
    row = pl.BlockSpec((tb, D), lambda i: (i, 0))
    eflat = experts.reshape(T * PEER_HK)
    return pl.pallas_call(
        _peer_kernel,
        grid=(nsteps,),
        in_specs=[pl.BlockSpec((tb * PEER_HK,), lambda i: (i,), memory_space=pltpu.SMEM),
                  pl.BlockSpec((tb * PEER_HK,), lambda i: (jnp.minimum(i + 1, nsteps - 1),),
                               memory_space=pltpu.SMEM),
                  row,
                  pl.BlockSpec((tb, PEER_HK), lambda i: (i, 0)),
                  row,
                  pl.BlockSpec((1, 1, D), lambda i: (i // per_b, 0, 0)),
                  pl.BlockSpec(memory_space=pl.ANY)],
        out_specs=row,
        out_shape=jax.ShapeDtypeStruct((T, D), F32),
        scratch_shapes=[pltpu.VMEM((PEER_TB, G, V7X_SUBLANES, D), jnp.int32),
                        pltpu.VMEM((PEER_TB, G, V7X_SUBLANES, D), jnp.int32),
                        pltpu.VMEM((PEER_TB * PEER_HK, V7X_LANES), F32),
                        pltpu.VMEM((PEER_TB * PEER_HK, V7X_LANES), F32),
                        pltpu.SemaphoreType.DMA((PEER_TB,)),
                        pltpu.SemaphoreType.DMA((PEER_TB,))],
        compiler_params=_cparams(("arbitrary",)),
        name="peer_gather",
    )(eflat, eflat, h32, gates, x2, g2.reshape(B, 1, D), uv_tab)


def _layer(li, x2, mod, cos_r, sin_r, cos_a, sin_a, batch, seq, norm1_g, norm2_g, w_in_all, w_out_all, out_norm_g,
           ssm_a_re, ssm_a_im, ssm_log_dt, ssm_b_re, ssm_b_im, ssm_c_re, ssm_c_im,
           ssm_d, ssm_w_glu_all, ssm_b_glu, attn_q_norm, attn_k_norm, attn_sinks,
           pool_w, pool_scale, peer_w_query_all, peer_sub_keys, peer_u_all, peer_v_all):
    D = D_MODEL
    sh1, sc1, g1, sh2, sc2, g2 = [mod[:batch, k * D:(k + 1) * D] for k in range(6)]

    (h1,) = _norm_mod(x2, norm1_g, sc1, sh1, seq, (BF16,))
    proj = _matmul(h1, _to_bf16(w_in_all, li), tm=1024, tn=640)

    ret = _retention(proj, cos_r, sin_r, batch, seq)
    ssm_params = _ssm_params(ssm_a_re, ssm_a_im, ssm_log_dt, ssm_b_re, ssm_b_im, ssm_c_re, ssm_c_im)
    ssm = _glu(_ssm_scan(proj, ssm_params, ssm_d, batch, seq), _to_bf16(ssm_w_glu_all, li), ssm_b_glu)
    swa = _swa(proj, cos_a, sin_a, attn_q_norm, attn_k_norm, attn_sinks, batch, seq)
    pool = _pool(proj, pool_w, pool_scale, batch, seq)
    x2 = _out_proj((ret, ssm, swa, pool), out_norm_g, _to_bf16(w_out_all, li), x2, g1, seq)

    h2b, h2f = _norm_mod(x2, norm2_g, sc2, sh2, seq, (BF16, F32))
    q = _matmul(h2b, _to_bf16(peer_w_query_all, li), tm=1024, tn=512)
    experts, gates = _peer_topk(q, peer_sub_keys.astype(BF16))
    return _peer_gather(experts, gates, h2f, x2, g2, _pack_tables(peer_u_all, peer_v_all, li), seq)


def kernel(x, c, positions, ada_w, ada_b, norm1_g, norm2_g, w_in, w_out, out_norm_g, ssm_a_re, ssm_a_im, ssm_log_dt, ssm_b_re, ssm_b_im, ssm_c_re, ssm_c_im, ssm_d, ssm_w_glu, ssm_b_glu, attn_q_norm, attn_k_norm, attn_sinks, pool_w, pool_scale, peer_w_query, peer_sub_keys, peer_u, peer_v):
    B, S, D = x.shape
    depth = ada_w.shape[0]
    c8 = jnp.concatenate([c, jnp.zeros((V7X_SUBLANES - B, D), c.dtype)], axis=0)
    mod = _ada_mod(c8, ada_w, ada_b)
    cos_r, sin_r = _rope_tables(positions, RET_DK)
    cos_a, sin_a = _rope_tables(positions, SWA_HEAD_DIM)
    x2 = x.reshape(B * S, D)
    for i in range(depth):
        x2 = _layer(i, x2, mod[i], cos_r, sin_r, cos_a, sin_a, B, S, norm1_g[i], norm2_g[i], w_in, w_out,
                    out_norm_g[i], ssm_a_re[i], ssm_a_im[i], ssm_log_dt[i], ssm_b_re[i], ssm_b_im[i],
                    ssm_c_re[i], ssm_c_im[i], ssm_d[i], ssm_w_glu, ssm_b_glu[i],
                    attn_q_norm[i], attn_k_norm[i], attn_sinks[i], pool_w[i], pool_scale[i],
                    peer_w_query, peer_sub_keys[i], peer_u, peer_v)
    return x2.reshape(B, S, D)
```

```python
import functools
import math

import numpy as np
import jax
import jax.numpy as jnp
from jax import lax
from jax.experimental import pallas as pl
from jax.experimental.pallas import tpu as pltpu

F32 = jnp.float32
BF16 = jnp.bfloat16
HIGHEST = lax.Precision.HIGHEST

V7X_LANES = 128
V7X_SUBLANES = 8
V7X_VMEM_BYTES = 64 * 1024 * 1024
VMEM_LIMIT = 52 * 1024 * 1024

D_MODEL = 4096
EPS = 1e-6
ROPE_THETA = 10000.0

RET_HEADS = 4
RET_DV = 256
RET_DK = 128
RET_CHUNK = 128

SSM_WIDTH = 1024
SSM_GROUP = 16
SSM_GROUPS = 64
SSM_STATE = 64
SSM_COLS = 128
SSM_CH = SSM_COLS // SSM_GROUP * SSM_STATE
SSM_CHUNK = 256

SWA_HEAD_DIM = 64
SWA_Q_HEADS = 16
SWA_KV_HEADS = 2
SWA_WINDOW = 128

POOL_WINDOWS = (2, 4, 8, 16)
POOL_GROUP = 256
POOL_MAXW = 16

OFF_QR, OFF_KR, OFF_VR, OFF_GR, OFF_US, OFF_QA, OFF_KA, OFF_VA, OFF_UP = (
    0, 512, 1024, 2048, 3072, 4096, 5120, 5248, 5376)
IN_COLS = 6400

PEER_N_KEYS = 128
PEER_HEADS = 8
PEER_TOPK = 16
PEER_HK = PEER_HEADS * PEER_TOPK
PEER_TB = 8
PEER_GROUPS = PEER_HK // V7X_SUBLANES
PEER_S1_GROUPS = 4
PEER_S1_CHAINS = 4
PEER_S2_GROUPS = 4


def _cparams(sem):
    return pltpu.CompilerParams(dimension_semantics=sem, vmem_limit_bytes=VMEM_LIMIT)


def _gelu(x):
    return 0.5 * x * (1.0 + lax.erf(x * math.sqrt(0.5)))


def _sigmoid(x):
    return 1.0 / (1.0 + jnp.exp(-x))


def _ada_kernel(c_ref, w_ref, b_ref, o_ref):
    c = c_ref[...]
    a = c * _sigmoid(c)
    o_ref[0] = jnp.dot(a, w_ref[0], preferred_element_type=F32, precision=HIGHEST) + b_ref[0]


def _ada_mod(c8, ada_w, ada_b):
    L, D, N = ada_w.shape
    tn = 1024
    return pl.pallas_call(
        _ada_kernel,
        grid=(L, N // tn),
        in_specs=[pl.BlockSpec((8, D), lambda l, j: (0, 0)),
                  pl.BlockSpec((1, D, tn), lambda l, j: (l, 0, j)),
                  pl.BlockSpec((1, 1, tn), lambda l, j: (l, 0, j))],
        out_specs=pl.BlockSpec((1, 8, tn), lambda l, j: (l, 0, j)),
        out_shape=jax.ShapeDtypeStruct((L, 8, N), F32),
        compiler_params=_cparams(("arbitrary", "arbitrary")),
        name="ada_mod",
    )(c8, ada_w, ada_b.reshape(L, 1, N))


def _norm_mod_kernel(x_ref, g_ref, sc_ref, sh_ref, *o_refs):
    x = x_ref[...]
    ms = jnp.mean(x * x, axis=-1, keepdims=True)
    y = x * lax.rsqrt(ms + EPS) * g_ref[...]
    h = y * (1.0 + sc_ref[0]) + sh_ref[0]
    for o in o_refs:
        o[...] = h.astype(o.dtype)


def _norm_mod(x2, gain, sc, sh, seq, out_dtypes):
    T, D = x2.shape
    B = sc.shape[0]
    tm = 256
    per_b = seq // tm
    row = pl.BlockSpec((tm, D), lambda i: (i, 0))
    bspec = pl.BlockSpec((1, 1, D), lambda i: (i // per_b, 0, 0))
    outs = pl.pallas_call(
        _norm_mod_kernel,
        grid=(T // tm,),
        in_specs=[row, pl.BlockSpec((1, D), lambda i: (0, 0)), bspec, bspec],
        out_specs=[row for _ in out_dtypes],
        out_shape=[jax.ShapeDtypeStruct((T, D), dt) for dt in out_dtypes],
        compiler_params=_cparams(("arbitrary",)),
        name="norm_mod",
    )(x2, gain.reshape(1, D), sc.reshape(B, 1, D), sh.reshape(B, 1, D))
    return outs


def _mm_kernel(a_ref, w_ref, o_ref):
    o_ref[...] = jnp.dot(a_ref[...], w_ref[...], preferred_element_type=F32)


def _matmul(a, w, tm=512, tn=512):
    M, K = a.shape
    N = w.shape[1]
    return pl.pallas_call(
        _mm_kernel,
        grid=(M // tm, N // tn),
        in_specs=[pl.BlockSpec((tm, K), lambda i, j: (i, 0)),
                  pl.BlockSpec((K, tn), lambda i, j: (0, j))],
        out_specs=pl.BlockSpec((tm, tn), lambda i, j: (i, j)),
        out_shape=jax.ShapeDtypeStruct((M, N), F32),
        compiler_params=_cparams(("arbitrary", "arbitrary")),
        name="matmul",
    )(a, w)


def _cast_kernel3(x_ref, o_ref):
    o_ref[...] = x_ref[0].astype(o_ref.dtype)


def _to_bf16(w_stack, layer, rows=256):
    _, K, N = w_stack.shape
    rows = min(rows, K)
    return pl.pallas_call(
        _cast_kernel3,
        grid=(K // rows,),
        in_specs=[pl.BlockSpec((1, rows, N), lambda i: (layer, i, 0))],
        out_specs=pl.BlockSpec((rows, N), lambda i: (i, 0)),
        out_shape=jax.ShapeDtypeStruct((K, N), BF16),
        compiler_params=_cparams(("arbitrary",)),
        name="cast_bf16",
    )(w_stack)


def _rope_tables(positions, d):
    half = d // 2
    inv = ROPE_THETA ** (-jnp.arange(half, dtype=F32) * 2.0 / d)
    ang = positions.astype(F32).reshape(-1, 1) * inv
    cos, sin = jnp.cos(ang), jnp.sin(ang)
    reps = V7X_LANES // d
    cosf = jnp.concatenate([cos, cos] * reps, axis=-1)
    sinf = jnp.concatenate([-sin, sin] * reps, axis=-1)
    return cosf, sinf


def _ret_consts():
    L, H = RET_CHUNK, RET_HEADS
    log_g = np.log(1.0 - 2.0 ** (-5.0 - np.arange(H, dtype=np.float64)))
    idx = np.arange(L, dtype=np.float64)
    diff = idx[:, None] - idx[None, :]
    decay = np.where(diff >= 0, np.exp(np.maximum(diff, 0.0)[None] * log_g[:, None, None]), 0.0)
    w_k = np.exp((L - 1 - idx)[None, :] * log_g[:, None])
    w_q = np.exp((idx + 1)[None, :] * log_g[:, None])
    gam = np.exp(L * log_g)
    wk_full = np.broadcast_to(w_k[:, :, None], (H, L, RET_DK))
    wq_full = np.broadcast_to(w_q[:, :, None], (H, L, RET_DK))
    gam_full = np.broadcast_to(gam[:, None, None], (H, 1, RET_DV))
    return (jnp.asarray(decay, F32), jnp.asarray(wq_full, F32), jnp.asarray(wk_full, F32),
            jnp.asarray(gam_full, F32))


def _ret_kernel(q_ref, k_ref, v_ref, g_ref, cos_ref, sin_ref, dec_ref, wq_ref, wk_ref, gam_ref,
                o_ref, r_ref):
    n = pl.program_id(1)

    @pl.when(n == 0)
    def _():
        r_ref[...] = jnp.zeros_like(r_ref)

    cos = cos_ref[...]
    sin = sin_ref[...]
    for h in range(RET_HEADS):
        ks = slice(h * RET_DK, (h + 1) * RET_DK)
        vs = slice(h * RET_DV, (h + 1) * RET_DV)
        q = q_ref[:, ks]
        k = k_ref[:, ks]
        q = q * cos + pltpu.roll(q, RET_DK // 2, 1) * sin
        k = (k * cos + pltpu.roll(k, RET_DK // 2, 1) * sin) * (RET_DK ** -0.5)
        vb = v_ref[:, vs].astype(BF16)
        s = lax.dot_general(q.astype(BF16), k.astype(BF16), (((1,), (1,)), ((), ())),
                            preferred_element_type=F32) * dec_ref[h]
        o = jnp.dot(s.astype(BF16), vb, preferred_element_type=F32)
        r = r_ref[h]
        o = o + jnp.dot((q * wq_ref[h]).astype(BF16), r.astype(BF16), preferred_element_type=F32)
        kv = lax.dot_general((k * wk_ref[h]).astype(BF16), vb, (((0,), (0,)), ((), ())),
                             preferred_element_type=F32)
        r_ref[h] = gam_ref[h] * r + kv
        mu = jnp.mean(o, axis=-1, keepdims=True)
        oc = o - mu
        var = jnp.mean(oc * oc, axis=-1, keepdims=True)
        g = g_ref[:, vs]
        o_ref[:, vs] = g * _sigmoid(g) * (oc * lax.rsqrt(var + EPS))


def _retention(proj, cosf, sinf, batch, seq):
    T = proj.shape[0]
    L = RET_CHUNK
    n = seq // L
    H = RET_HEADS
    dec, wq, wk, gam = _ret_consts()
    kw, vw = H * RET_DK, H * RET_DV
    const3 = lambda b, c: (0, 0, 0)
    return pl.pallas_call(
        _ret_kernel,
        grid=(batch, n),
        in_specs=[
            pl.BlockSpec((L, kw), lambda b, c: (b * n + c, OFF_QR // kw)),
            pl.BlockSpec((L, kw), lambda b, c: (b * n + c, OFF_KR // kw)),
            pl.BlockSpec((L, vw), lambda b, c: (b * n + c, OFF_VR // vw)),
            pl.BlockSpec((L, vw), lambda b, c: (b * n + c, OFF_GR // vw)),
            pl.BlockSpec((L, RET_DK), lambda b, c: (b * n + c, 0)),
            pl.BlockSpec((L, RET_DK), lambda b, c: (b * n + c, 0)),
            pl.BlockSpec((H, L, L), const3),
            pl.BlockSpec((H, L, RET_DK), const3),
            pl.BlockSpec((H, L, RET_DK), const3),
            pl.BlockSpec((H, 1, RET_DV), const3),
        ],
        out_specs=pl.BlockSpec((L, vw), lambda b, c: (b * n + c, 0)),
        out_shape=jax.ShapeDtypeStruct((T, vw), F32),
        scratch_shapes=[pltpu.VMEM((H, RET_DK, RET_DV), F32)],
        compiler_params=_cparams(("arbitrary", "arbitrary")),
        name="retention",
    )(proj, proj, proj, proj, cosf, sinf, dec, wq, wk, gam)


def _ssm_params(a_re, a_im, log_dt, b_re, b_im, c_re, c_im):
    dt = jnp.exp(log_dt)[:, None]
    mag = jnp.exp(a_re * dt)
    ab_re = mag * jnp.cos(a_im * dt)
    ab_im = mag * jnp.sin(a_im * dt)
    den = a_re * a_re + a_im * a_im
    nr = ab_re - 1.0
    f_re = (nr * a_re + ab_im * a_im) / den
    f_im = (ab_im * a_re - nr * a_im) / den
    bb_re = f_re[..., None] * b_re - f_im[..., None] * b_im
    bb_im = f_re[..., None] * b_im + f_im[..., None] * b_re
    nblk = SSM_WIDTH // SSM_COLS
    gpb = SSM_COLS // SSM_GROUP
    eye = jnp.eye(gpb, dtype=F32)

    def blockdiag_in(bb):
        t = bb.reshape(nblk, gpb, SSM_STATE, SSM_GROUP)
        m = jnp.einsum('ngpc,gh->ngchp', t, eye)
        return m.reshape(nblk, SSM_COLS, SSM_CH)

    def blockdiag_out(cc):
        t = cc.reshape(nblk, gpb, SSM_GROUP, SSM_STATE)
        m = jnp.einsum('ngcp,gh->ngphc', t, eye)
        return m.reshape(nblk, SSM_CH, SSM_COLS)

    rounds = int(math.log2(SSM_CHUNK))
    pr, pi = [ab_re], [ab_im]
    for _ in range(rounds - 1):
        r, i = pr[-1], pi[-1]
        pr.append(r * r - i * i)
        pi.append(2.0 * r * i)
    apr = jnp.stack(pr, 0).reshape(rounds, nblk, SSM_CH).transpose(1, 0, 2)
    api = jnp.stack(pi, 0).reshape(rounds, nblk, SSM_CH).transpose(1, 0, 2)
    qr, qi = [ab_re], [ab_im]
    for _ in range(V7X_SUBLANES - 1):
        r, i = qr[-1], qi[-1]
        qr.append(r * ab_re - i * ab_im)
        qi.append(r * ab_im + i * ab_re)
    ppr = jnp.stack(qr, 0).reshape(V7X_SUBLANES, nblk, SSM_CH).transpose(1, 0, 2)
    ppi = jnp.stack(qi, 0).reshape(V7X_SUBLANES, nblk, SSM_CH).transpose(1, 0, 2)
    return (blockdiag_in(bb_re).astype(BF16), blockdiag_in(bb_im).astype(BF16),
            blockdiag_out(c_re).astype(BF16), blockdiag_out(c_im).astype(BF16), apr, api, ppr, ppi)


def _cmul_add(xr, xi, p_r, p_i, sr, si):
    return xr + p_r * sr - p_i * si, xi + p_r * si + p_i * sr


def _ssm_kernel(u_ref, bre_ref, bim_ref, cre_ref, cim_ref, apr_ref, api_ref, ppr_ref, ppi_ref, d_ref, o_ref,
                cr_ref, ci_ref, xr_s, xi_s, hr_s, hi_s):
    n = pl.program_id(2)
    hs = V7X_SUBLANES
    ng = SSM_CHUNK // hs

    @pl.when(n == 0)
    def _():
        cr_ref[...] = jnp.zeros_like(cr_ref)
        ci_ref[...] = jnp.zeros_like(ci_ref)

    u = u_ref[...]
    ub = u.astype(BF16)
    xr = jnp.dot(ub, bre_ref[0], preferred_element_type=F32)
    xi = jnp.dot(ub, bim_ref[0], preferred_element_type=F32)
    sub = jnp.bitwise_and(lax.broadcasted_iota(jnp.int32, xr.shape, 0), hs - 1)
    in_rounds = int(math.log2(hs))
    for k in range(in_rounds):
        s = 1 << k
        keep = sub >= s
        sr = jnp.where(keep, pltpu.roll(xr, s, 0), 0.0)
        si = jnp.where(keep, pltpu.roll(xi, s, 0), 0.0)
        xr, xi = _cmul_add(xr, xi, apr_ref[0, k:k + 1, :], api_ref[0, k:k + 1, :], sr, si)
    nlt = SSM_CH // V7X_LANES
    for q in range(nlt):
        xr_s[q] = xr[:, q * V7X_LANES:(q + 1) * V7X_LANES]
        xi_s[q] = xi[:, q * V7X_LANES:(q + 1) * V7X_LANES]

    er = jnp.concatenate([xr_s[q, pl.ds(hs - 1, ng, stride=hs), :] for q in range(nlt)], axis=1)
    ei = jnp.concatenate([xi_s[q, pl.ds(hs - 1, ng, stride=hs), :] for q in range(nlt)], axis=1)
    c_r = cr_ref[...]
    c_i = ci_ref[...]
    grow = lax.broadcasted_iota(jnp.int32, er.shape, 0)
    first = grow == 0
    a8r = apr_ref[0, in_rounds:in_rounds + 1, :]
    a8i = api_ref[0, in_rounds:in_rounds + 1, :]
    er = er + jnp.where(first, a8r * c_r - a8i * c_i, 0.0)
    ei = ei + jnp.where(first, a8r * c_i + a8i * c_r, 0.0)
    for k in range(int(math.log2(ng))):
        s = 1 << k
        keep = grow >= s
        sr = jnp.where(keep, pltpu.roll(er, s, 0), 0.0)
        si = jnp.where(keep, pltpu.roll(ei, s, 0), 0.0)
        kk = in_rounds + k
        er, ei = _cmul_add(er, ei, apr_ref[0, kk:kk + 1, :], api_ref[0, kk:kk + 1, :], sr, si)
    cr_ref[...] = er[ng - 1:ng, :]
    ci_ref[...] = ei[ng - 1:ng, :]
    hr_s[...] = jnp.where(first, c_r, pltpu.roll(er, 1, 0))
    hi_s[...] = jnp.where(first, c_i, pltpu.roll(ei, 1, 0))
    p_r = ppr_ref[0]
    p_i = ppi_ref[0]
    for g in range(ng):
        rows = slice(g * hs, (g + 1) * hs)
        for q in range(nlt):
            ls = slice(q * V7X_LANES, (q + 1) * V7X_LANES)
            hb_r = jnp.broadcast_to(hr_s[g:g + 1, ls], (hs, V7X_LANES))
            hb_i = jnp.broadcast_to(hi_s[g:g + 1, ls], (hs, V7X_LANES))
            yr, yi = _cmul_add(xr_s[q, rows, :], xi_s[q, rows, :], p_r[:, ls], p_i[:, ls], hb_r, hb_i)
            xr_s[q, rows, :] = yr
            xi_s[q, rows, :] = yi
    hr = jnp.concatenate([xr_s[q] for q in range(nlt)], axis=1)
    hi = jnp.concatenate([xi_s[q] for q in range(nlt)], axis=1)
    y = (jnp.dot(hr.astype(BF16), cre_ref[0], preferred_element_type=F32)
         - jnp.dot(hi.astype(BF16), cim_ref[0], preferred_element_type=F32))
    o_ref[...] = _gelu(y + d_ref[...] * u)


def _ssm_scan(proj, params, d, batch, seq):
    T = proj.shape[0]
    bre, bim, cre, cim, apr, api, ppr, ppi = params
    nblk = SSM_WIDTH // SSM_COLS
    nt = seq // SSM_CHUNK
    rounds = apr.shape[1]
    ngroups = SSM_CHUNK // V7X_SUBLANES
    wspec_in = pl.BlockSpec((1, SSM_COLS, SSM_CH), lambda b, c, n: (c, 0, 0))
    wspec_out = pl.BlockSpec((1, SSM_CH, SSM_COLS), lambda b, c, n: (c, 0, 0))
    pspec = pl.BlockSpec((1, rounds, SSM_CH), lambda b, c, n: (c, 0, 0))
    qspec = pl.BlockSpec((1, V7X_SUBLANES, SSM_CH), lambda b, c, n: (c, 0, 0))
    return pl.pallas_call(
        _ssm_kernel,
        grid=(batch, nblk, nt),
        in_specs=[pl.BlockSpec((SSM_CHUNK, SSM_COLS), lambda b, c, n: (b * nt + n, OFF_US // SSM_COLS + c)),
                  wspec_in, wspec_in, wspec_out, wspec_out, pspec, pspec, qspec, qspec,
                  pl.BlockSpec((1, SSM_COLS), lambda b, c, n: (0, c))],
        out_specs=pl.BlockSpec((SSM_CHUNK, SSM_COLS), lambda b, c, n: (b * nt + n, c)),
        out_shape=jax.ShapeDtypeStruct((T, SSM_WIDTH), F32),
        scratch_shapes=[pltpu.VMEM((1, SSM_CH), F32), pltpu.VMEM((1, SSM_CH), F32),
                        pltpu.VMEM((SSM_CH // V7X_LANES, SSM_CHUNK, V7X_LANES), F32),
                        pltpu.VMEM((SSM_CH // V7X_LANES, SSM_CHUNK, V7X_LANES), F32),
                        pltpu.VMEM((ngroups, SSM_CH), F32), pltpu.VMEM((ngroups, SSM_CH), F32)],
        compiler_params=_cparams(("arbitrary", "arbitrary", "arbitrary")),
        name="ssm_scan",
    )(proj, bre, bim, cre, cim, apr, api, ppr, ppi, d.reshape(1, SSM_WIDTH))


def _glu_kernel(y_ref, w_ref, b_ref, o_ref):
    y = y_ref[...]
    z = jnp.dot(y.astype(BF16), w_ref[...], preferred_element_type=F32) + b_ref[...]
    o_ref[...] = y * _sigmoid(z)


def _glu(y, w, b):
    T, W = y.shape
    tm = 512
    return pl.pallas_call(
        _glu_kernel,
        grid=(T // tm,),
        in_specs=[pl.BlockSpec((tm, W), lambda i: (i, 0)),
                  pl.BlockSpec((W, W), lambda i: (0, 0)),
                  pl.BlockSpec((1, W), lambda i: (0, 0))],
        out_specs=pl.BlockSpec((tm, W), lambda i: (i, 0)),
        out_shape=jax.ShapeDtypeStruct((T, W), F32),
        compiler_params=_cparams(("arbitrary",)),
        name="ssm_glu",
    )(y, w, b.reshape(1, W))


def _swa_kernel(sink_ref, q_ref, k_ref, v_ref, cos_ref, sin_ref, qn_ref, kn_ref, o_ref, pk_ref, pv_ref):
    n = pl.program_id(1)
    W = SWA_WINDOW
    hd = SWA_HEAD_DIM

    @pl.when(n == 0)
    def _():
        pk_ref[...] = jnp.zeros_like(pk_ref)
        pv_ref[...] = jnp.zeros_like(pv_ref)

    cos = cos_ref[...]
    sin = sin_ref[...]
    lane = lax.broadcasted_iota(jnp.int32, (W, V7X_LANES), 1)
    lo = lane < hd
    first_half = (lane % hd) < (hd // 2)

    def head_norm(t, gain):
        tt = t * t
        s_all = jnp.sum(tt, axis=-1, keepdims=True)
        s_lo = jnp.sum(jnp.where(lo, tt, 0.0), axis=-1, keepdims=True)
        ms = jnp.where(lo, s_lo, s_all - s_lo) * (1.0 / hd)
        return t * lax.rsqrt(ms + EPS) * gain

    def rope(t):
        sw = jnp.where(first_half, pltpu.roll(t, V7X_LANES - hd // 2, 1), pltpu.roll(t, hd // 2, 1))
        return t * cos + sw * sin

    kc = rope(head_norm(k_ref[...], kn_ref[...]))
    vc = v_ref[...]
    keys = jnp.concatenate([pk_ref[...], kc], axis=0)
    vals = jnp.concatenate([pv_ref[...], vc], axis=0)
    lane2 = lax.broadcasted_iota(jnp.int32, (2 * W, V7X_LANES), 1)
    lo2 = lane2 < hd

    def dup(x, h):
        sw = pltpu.roll(x, hd, 1)
        return (jnp.where(lo2, x, sw) if h == 0 else jnp.where(lo2, sw, x)).astype(BF16)

    kk = [dup(keys, h) for h in range(SWA_KV_HEADS)]
    vv = [dup(vals, h) for h in range(SWA_KV_HEADS)]
    qi = lax.broadcasted_iota(jnp.int32, (W, 2 * W), 0)
    ci = lax.broadcasted_iota(jnp.int32, (W, 2 * W), 1)
    prev_floor = qi + jnp.where(n > 0, 0, 2 * W)
    valid = jnp.logical_or(jnp.logical_and(ci < W, ci > prev_floor),
                           jnp.logical_and(ci >= W, (ci - W) <= qi))
    tiles = SWA_Q_HEADS * hd // V7X_LANES
    per_kv = tiles // SWA_KV_HEADS
    for m in range(tiles):
        sl = slice(m * V7X_LANES, (m + 1) * V7X_LANES)
        qt = rope(head_norm(q_ref[:, sl], qn_ref[:, sl]))
        h = m // per_kv
        outs = []
        for half in range(2):
            sel = lo if half == 0 else jnp.logical_not(lo)
            qm = jnp.where(sel, qt, 0.0).astype(BF16)
            s = lax.dot_general(qm, kk[h], (((1,), (1,)), ((), ())),
                                preferred_element_type=F32) * (hd ** -0.5)
            s = jnp.where(valid, s, -1e30)
            sink = sink_ref[2 * m + half]
            mx = jnp.maximum(jnp.max(s, axis=-1, keepdims=True), sink)
            p = jnp.exp(s - mx)
            den = jnp.sum(p, axis=-1, keepdims=True) + jnp.exp(sink - mx)
            p = p / den
            outs.append(jnp.dot(p.astype(BF16), vv[h], preferred_element_type=F32))
        o_ref[:, sl] = jnp.where(lo, outs[0], outs[1])
    pk_ref[...] = kc
    pv_ref[...] = vc


def _swa(proj, cosf, sinf, q_norm, k_norm, sinks, batch, seq):
    T = proj.shape[0]
    W = SWA_WINDOW
    nb = seq // W
    qw = SWA_Q_HEADS * SWA_HEAD_DIM
    qn = jnp.tile(q_norm, SWA_Q_HEADS).reshape(1, qw)
    kn = jnp.tile(k_norm, SWA_KV_HEADS).reshape(1, V7X_LANES)
    return pl.pallas_call(
        _swa_kernel,
        grid=(batch, nb),
        in_specs=[pl.BlockSpec(memory_space=pltpu.SMEM),
                  pl.BlockSpec((W, qw), lambda b, n: (b * nb + n, OFF_QA // qw)),
                  pl.BlockSpec((W, V7X_LANES), lambda b, n: (b * nb + n, OFF_KA // V7X_LANES)),
                  pl.BlockSpec((W, V7X_LANES), lambda b, n: (b * nb + n, OFF_VA // V7X_LANES)),
                  pl.BlockSpec((W, V7X_LANES), lambda b, n: (b * nb + n, 0)),
                  pl.BlockSpec((W, V7X_LANES), lambda b, n: (b * nb + n, 0)),
                  pl.BlockSpec((1, qw), lambda b, n: (0, 0)),
                  pl.BlockSpec((1, V7X_LANES), lambda b, n: (0, 0))],
        out_specs=pl.BlockSpec((W, qw), lambda b, n: (b * nb + n, 0)),
        out_shape=jax.ShapeDtypeStruct((T, qw), F32),
        scratch_shapes=[pltpu.VMEM((W, V7X_LANES), F32), pltpu.VMEM((W, V7X_LANES), F32)],
        compiler_params=_cparams(("arbitrary", "arbitrary")),
        name="swa",
    )(sinks, proj, proj, proj, cosf, sinf, qn, kn)


POOL_CHUNK = 256


def _pool_kernel(u0_ref, u1_ref, u2_ref, u3_ref, w_ref, sc_ref, o_ref, ext_ref):
    n = pl.program_id(1)
    Lc = POOL_CHUNK

    @pl.when(n == 0)
    def _():
        ext_ref[:, 0:POOL_MAXW, :] = jnp.zeros((len(POOL_WINDOWS), POOL_MAXW, POOL_GROUP), F32)

    t = n * Lc + lax.broadcasted_iota(jnp.int32, (Lc, POOL_GROUP), 0)
    for g, (u_ref, w) in enumerate(zip((u0_ref, u1_ref, u2_ref, u3_ref), POOL_WINDOWS)):
        u = u_ref[...]
        ext_ref[g, POOL_MAXW:POOL_MAXW + Lc, :] = u
        acc = u
        for k in range(1, w):
            acc = acc + ext_ref[g, POOL_MAXW - k:POOL_MAXW - k + Lc, :]
        cnt = jnp.minimum(t + 1, w).astype(F32)
        pooled = acc / cnt - u
        y = jnp.dot(pooled.astype(BF16), w_ref[g], preferred_element_type=F32)
        sl = slice(g * POOL_GROUP, (g + 1) * POOL_GROUP)
        o_ref[:, sl] = y * sc_ref[:, sl]
        ext_ref[g, 0:POOL_MAXW, :] = u[Lc - POOL_MAXW:, :]


def _pool(proj, pool_w, pool_scale, batch, seq):
    T = proj.shape[0]
    Lc = POOL_CHUNK
    nt = seq // Lc
    ng = len(POOL_WINDOWS)
    width = ng * POOL_GROUP
    uspecs = [pl.BlockSpec((Lc, POOL_GROUP), functools.partial(
        lambda b, n, g: (b * nt + n, OFF_UP // POOL_GROUP + g), g=g)) for g in range(ng)]
    return pl.pallas_call(
        _pool_kernel,
        grid=(batch, nt),
        in_specs=uspecs + [pl.BlockSpec((ng, POOL_GROUP, POOL_GROUP), lambda b, n: (0, 0, 0)),
                           pl.BlockSpec((1, width), lambda b, n: (0, 0))],
        out_specs=pl.BlockSpec((Lc, width), lambda b, n: (b * nt + n, 0)),
        out_shape=jax.ShapeDtypeStruct((T, width), F32),
        scratch_shapes=[pltpu.VMEM((ng, POOL_MAXW + Lc, POOL_GROUP), F32)],
        compiler_params=_cparams(("arbitrary", "arbitrary")),
        name="pool",
    )(proj, proj, proj, proj, pool_w.astype(BF16), pool_scale.reshape(1, width))


def _out_kernel(r_ref, s_ref, a_ref, p_ref, gain_ref, w_ref, x_ref, g1_ref, o_ref, mix_ref):
    j = pl.program_id(1)

    @pl.when(j == 0)
    def _():
        for idx, ref in enumerate((r_ref, s_ref, a_ref, p_ref)):
            v = ref[...]
            width = v.shape[1]
            sl = slice(idx * width, (idx + 1) * width)
            ms = jnp.mean(v * v, axis=-1, keepdims=True)
            mix_ref[:, sl] = (v * lax.rsqrt(ms + EPS) * gain_ref[:, sl]).astype(BF16)

    acc = jnp.dot(mix_ref[...], w_ref[...], preferred_element_type=F32)
    o_ref[...] = x_ref[...] + g1_ref[0] * acc


def _out_proj(branches, gain, w_out_bf, x2, g1, seq):
    T, D = x2.shape
    B = g1.shape[0]
    tm, tn = 512, 512
    per_b = seq // tm
    width = branches[0].shape[1]
    bspec = pl.BlockSpec((tm, width), lambda i, j: (i, 0))
    return pl.pallas_call(
        _out_kernel,
        grid=(T // tm, D // tn),
        in_specs=[bspec, bspec, bspec, bspec,
                  pl.BlockSpec((1, D), lambda i, j: (0, 0)),
                  pl.BlockSpec((D, tn), lambda i, j: (0, j)),
                  pl.BlockSpec((tm, tn), lambda i, j: (i, j)),
                  pl.BlockSpec((1, 1, tn), lambda i, j: (i // per_b, 0, j))],
        out_specs=pl.BlockSpec((tm, tn), lambda i, j: (i, j)),
        out_shape=jax.ShapeDtypeStruct((T, D), F32),
        scratch_shapes=[pltpu.VMEM((tm, D), BF16)],
        compiler_params=_cparams(("arbitrary", "arbitrary")),
        name="out_proj",
    )(*branches, gain.reshape(1, D), w_out_bf, x2, g1.reshape(B, 1, D))


PEER_TOPK_TB = 128


def _iter_topk(s, k):
    n = s.shape[0]
    rows = lax.broadcasted_iota(jnp.int32, s.shape, 0).astype(F32)
    vals, idxs = [], []
    for _ in range(k):
        m = jnp.max(s, axis=0, keepdims=True)
        idx = jnp.min(jnp.where(s == m, rows, float(n)), axis=0, keepdims=True)
        s = jnp.where(rows == idx, -jnp.inf, s)
        vals.append(m)
        idxs.append(idx)
    return jnp.concatenate(vals, axis=0), jnp.concatenate(idxs, axis=0)


def _topk_kernel(q_ref, keys_ref, e_ref, g_ref):
    K = PEER_TOPK
    half = PEER_N_KEYS
    e_all, g_all = [], []
    for h in range(PEER_HEADS):
        tops = []
        for p in range(2):
            c0 = h * 2 * half + p * half
            qhp = q_ref[:, c0:c0 + half].astype(BF16)
            s = lax.dot_general(keys_ref[p], qhp, (((1,), (1,)), ((), ())), preferred_element_type=F32)
            tops.append(_iter_topk(s, K))
        (s0, i0), (s1, i1) = tops
        hs = V7X_SUBLANES
        io8 = lax.broadcasted_iota(jnp.int32, (hs, s0.shape[1]), 0).astype(F32)
        ps, pe, pp = [], [], []
        for a, b0 in [(0, 0), (0, hs)] + [(a, 0) for a in range(1, hs)]:
            ps.append(s0[a:a + 1, :] + s1[b0:b0 + hs, :])
            pe.append(i0[a:a + 1, :] * PEER_N_KEYS + i1[b0:b0 + hs, :])
            pp.append(io8 + float(a * K + b0))
        ps.append(s0[hs:K, :] + s1[0:1, :])
        pe.append(i0[hs:K, :] * PEER_N_KEYS + i1[0:1, :])
        pp.append((io8 + float(hs)) * float(K))
        cand_s = jnp.concatenate(ps, axis=0)
        cand_e = jnp.concatenate(pe, axis=0)
        rows = jnp.concatenate(pp, axis=0)
        bs, be = [], []
        for _ in range(K):
            m = jnp.max(cand_s, axis=0, keepdims=True)
            pos = jnp.min(jnp.where(cand_s == m, rows, float(K * K)), axis=0, keepdims=True)
            hit = rows == pos
            be.append(jnp.sum(jnp.where(hit, cand_e, 0.0), axis=0, keepdims=True))
            cand_s = jnp.where(hit, -jnp.inf, cand_s)
            bs.append(m)
        best = jnp.concatenate(bs, axis=0)
        pexp = jnp.exp(best - best[0:1, :])
        g_all.append(pexp / jnp.sum(pexp, axis=0, keepdims=True))
        e_all.append(jnp.concatenate(be, axis=0))
    e_ref[...] = jnp.concatenate(e_all, axis=0).T.astype(jnp.int32)
    g_ref[...] = jnp.concatenate(g_all, axis=0).T


def _peer_topk(q, sub_keys_bf):
    T, QW = q.shape
    tb = PEER_TOPK_TB
    return pl.pallas_call(
        _topk_kernel,
        grid=(T // tb,),
        in_specs=[pl.BlockSpec((tb, QW), lambda i: (i, 0)),
                  pl.BlockSpec((2, PEER_N_KEYS, PEER_N_KEYS), lambda i: (0, 0, 0))],
        out_specs=[pl.BlockSpec((tb, PEER_HK), lambda i: (i, 0)),
                   pl.BlockSpec((tb, PEER_HK), lambda i: (i, 0))],
        out_shape=[jax.ShapeDtypeStruct((T, PEER_HK), jnp.int32),
                   jax.ShapeDtypeStruct((T, PEER_HK), F32)],
        compiler_params=_cparams(("arbitrary",)),
        name="peer_topk",
    )(q, sub_keys_bf)


PEER_PACK_ROWS = 64


def _pack_kernel(u_ref, v_ref, o_ref):
    ub = lax.bitcast_convert_type(u_ref[0].astype(BF16).astype(F32), jnp.int32)
    vb = lax.bitcast_convert_type(v_ref[0].astype(BF16).astype(F32), jnp.int32)
    word = jnp.bitwise_or(lax.shift_right_logical(ub, 16), vb)
    for r in range(PEER_PACK_ROWS):
        o_ref[r] = word[r:r + 1, :]


def _pack_tables(u_stack, v_stack, layer):
    _, E, D = u_stack.shape
    R = PEER_PACK_ROWS
    spec = pl.BlockSpec((1, R, D), lambda i: (layer, i, 0))
    return pl.pallas_call(
        _pack_kernel,
        grid=(E // R,),
        in_specs=[spec, spec],
        out_specs=pl.BlockSpec((R, 1, D), lambda i: (i, 0, 0)),
        out_shape=jax.ShapeDtypeStruct((E, 1, D), jnp.int32),
        compiler_params=_cparams(("arbitrary",)),
        name="peer_pack",
    )(u_stack, v_stack)


def _peer_kernel(ec_ref, en_ref, h_ref, gate_ref, x_ref, g2_ref, tab_hbm, o_ref,
                 buf0, buf1, acc_ref, cmat_ref, sem0, sem1):
    i = pl.program_id(0)
    _peer_half(i, 0, buf0, sem0, buf1, sem1, ec_ref, PEER_TB, ec_ref, h_ref, gate_ref,
               tab_hbm, o_ref, acc_ref, cmat_ref)
    _peer_half(i, PEER_TB, buf1, sem1, buf0, sem0, en_ref, 0, ec_ref, h_ref, gate_ref,
               tab_hbm, o_ref, acc_ref, cmat_ref)
    o_ref[...] = x_ref[...] + g2_ref[0] * o_ref[...]


def _peer_half(i, row0, buf, csem, nbuf, nsem, nidx_ref, nrow0, ec_ref, h_ref, gate_ref, tab_hbm, o_ref,
               acc_ref, cmat_ref):
    D = h_ref.shape[1]
    nch = D // V7X_LANES
    half_ch = nch // 2
    G = PEER_GROUPS
    cur = 0
    nxt = 0
    zero = jnp.zeros((V7X_SUBLANES, V7X_LANES), F32)

    def start_group(idx_ref, r, g, slot, dst=nbuf, dsem=nsem):
        for s in range(V7X_SUBLANES):
            e = idx_ref[r * PEER_HK + g * V7X_SUBLANES + s]
            pltpu.make_async_copy(tab_hbm.at[e], dst.at[slot, g, pl.ds(s, 1), :],
                                  dsem.at[slot]).start()

    def wait_slot(slot, b=buf, bsem=csem):
        pltpu.make_async_copy(b.at[slot], b.at[slot], bsem.at[slot]).wait()

    if row0 == 0:
        @pl.when(i == 0)
        def _():
            for tt in range(PEER_TB):
                def body(g, carry, tt=tt):
                    start_group(ec_ref, tt, g, tt, dst=buf, dsem=csem)
                    return carry
                lax.fori_loop(0, G, body, 0)

    for tt in range(PEER_TB):
        wait_slot(cur + tt)

        def s1_body(j, carry, tt=tt):
            parts = [[None] * PEER_S1_CHAINS for _ in range(PEER_S1_GROUPS)]
            for c in range(nch):
                cs = slice(c * V7X_LANES, (c + 1) * V7X_LANES)
                hb = jnp.broadcast_to(h_ref[row0 + tt:row0 + tt + 1, cs], (V7X_SUBLANES, V7X_LANES))
                for q in range(PEER_S1_GROUPS):
                    w = buf[cur + tt, j * PEER_S1_GROUPS + q, :, cs]
                    t = lax.bitcast_convert_type(jnp.left_shift(w, 16), F32) * hb
                    p = parts[q][c % PEER_S1_CHAINS]
                    parts[q][c % PEER_S1_CHAINS] = t if p is None else p + t
            for q in range(PEER_S1_GROUPS):
                r0 = pl.multiple_of((tt * G + j * PEER_S1_GROUPS + q) * V7X_SUBLANES, V7X_SUBLANES)
                acc_ref[pl.ds(r0, V7X_SUBLANES), :] = (parts[q][0] + parts[q][1]) + (parts[q][2] + parts[q][3])
            for d in range(PEER_S1_GROUPS // 2):
                start_group(nidx_ref, nrow0 + tt, j * (PEER_S1_GROUPS // 2) + d, nxt + tt)
            return carry

        lax.fori_loop(0, G // PEER_S1_GROUPS, s1_body, 0)

    ones_row = jnp.ones((V7X_SUBLANES, V7X_LANES), BF16)
    ones_sq = jnp.ones((V7X_LANES, V7X_LANES), BF16)
    nt_dims = (((1,), (1,)), ((), ()))
    acc = acc_ref[...]
    acc_hi = acc.astype(BF16)
    acc_lo = (acc - acc_hi.astype(F32)).astype(BF16)
    a8 = (lax.dot_general(ones_row, acc_hi, nt_dims, preferred_element_type=F32)
          + lax.dot_general(ones_row, acc_lo, nt_dims, preferred_element_type=F32))
    a_tok = jnp.concatenate([a8[0:1, tt * PEER_HK:(tt + 1) * PEER_HK] for tt in range(PEER_TB)], axis=0)
    c_tok = gate_ref[row0:row0 + PEER_TB, :] * _gelu(a_tok)
    r_i = lax.broadcasted_iota(jnp.int32, (PEER_HK, V7X_LANES), 0)
    c_i = lax.broadcasted_iota(jnp.int32, (PEER_HK, V7X_LANES), 1)
    eye = r_i == c_i
    for tt in range(PEER_TB):
        c_diag = jnp.where(eye, jnp.broadcast_to(c_tok[tt:tt + 1, :], (PEER_HK, V7X_LANES)), 0.0)
        d_hi = c_diag.astype(BF16)
        d_lo = (c_diag - d_hi.astype(F32)).astype(BF16)
        cmat_ref[tt * PEER_HK:(tt + 1) * PEER_HK, :] = (
            jnp.dot(d_hi, ones_sq, preferred_element_type=F32) + jnp.dot(d_lo, ones_sq, preferred_element_type=F32))

    for tt in range(PEER_TB):
        for half in range(2):
            def s2_body(j, accs, tt=tt, half=half):
                new = list(accs)
                for q in range(PEER_S2_GROUPS):
                    g = j * PEER_S2_GROUPS + q
                    r0 = pl.multiple_of((tt * G + g) * V7X_SUBLANES, V7X_SUBLANES)
                    cm = cmat_ref[pl.ds(r0, V7X_SUBLANES), :]
                    for cc in range(half_ch):
                        c = half * half_ch + cc
                        w = buf[cur + tt, g, :, c * V7X_LANES:(c + 1) * V7X_LANES]
                        v = lax.bitcast_convert_type(jnp.bitwise_and(w, jnp.int32(-65536)), F32)
                        new[cc] = new[cc] + cm * v
                start_group(nidx_ref, nrow0 + tt, G // 2 + half * (G // 4) + j, nxt + tt)
                return tuple(new)

            accs = lax.fori_loop(0, G // PEER_S2_GROUPS, s2_body, tuple(zero for _ in range(half_ch)))
            for cc in range(half_ch):
                c = half * half_ch + cc
                o_ref[row0 + tt:row0 + tt + 1, c * V7X_LANES:(c + 1) * V7X_LANES] = jnp.sum(
                    accs[cc], axis=0, keepdims=True)

    if row0 != 0:
        @pl.when(i == pl.num_programs(0) - 1)
        def _():
            for tt in range(PEER_TB):
                wait_slot(tt, b=nbuf, bsem=nsem)


def _peer_gather(experts, gates, h32, x2, g2, uv_tab, seq):
    T, D = x2.shape
    B = g2.shape[0]
    tb = 2 * PEER_TB
    per_b = seq // tb
    nsteps = T // tb
    G = PEER_GROUPS
    assert PEER_S1_GROUPS % 2 == 0 and G % PEER_S1_GROUPS == 0 and G // PEER_S2_GROUPS == G // 4
    assert PEER_S1_CHAINS == 4
    row = pl.BlockSpec((tb, D), lambda i: (i, 0))
    eflat = experts.reshape(T * PEER_HK)
    return pl.pallas_call(
        _peer_kernel,
        grid=(nsteps,),
        in_specs=[pl.BlockSpec((tb * PEER_HK,), lambda i: (i,), memory_space=pltpu.SMEM),
                  pl.BlockSpec((tb * PEER_HK,), lambda i: (jnp.minimum(i + 1, nsteps - 1),),
                               memory_space=pltpu.SMEM),
                  row,
                  pl.BlockSpec((tb, PEER_HK), lambda i: (i, 0)),
                  row,
                  pl.BlockSpec((1, 1, D), lambda i: (i // per_b, 0, 0)),
                  pl.BlockSpec(memory_space=pl.ANY)],
        out_specs=row,
        out_shape=jax.ShapeDtypeStruct((T, D), F32),
        scratch_shapes=[pltpu.VMEM((PEER_TB, G, V7X_SUBLANES, D), jnp.int32),
                        pltpu.VMEM((PEER_TB, G, V7X_SUBLANES, D), jnp.int32),
                        pltpu.VMEM((PEER_TB * PEER_HK, V7X_LANES), F32),
                        pltpu.VMEM((PEER_TB * PEER_HK, V7X_LANES), F32),
                        pltpu.SemaphoreType.DMA((PEER_TB,)),
                        pltpu.SemaphoreType.DMA((PEER_TB,))],
        compiler_params=_cparams(("arbitrary",)),
        name="peer_gather",
    )(eflat, eflat, h32, gates, x2, g2.reshape(B, 1, D), uv_tab)


def _layer(li, x2, mod, cos_r, sin_r, cos_a, sin_a, batch, seq, norm1_g, norm2_g, w_in_all, w_out_all, out_norm_g,
           ssm_a_re, ssm_a_im, ssm_log_dt, ssm_b_re, ssm_b_im, ssm_c_re, ssm_c_im,
           ssm_d, ssm_w_glu_all, ssm_b_glu, attn_q_norm, attn_k_norm, attn_sinks,
           pool_w, pool_scale, peer_w_query_all, peer_sub_keys, peer_u_all, peer_v_all):
    D = D_MODEL
    sh1, sc1, g1, sh2, sc2, g2 = [mod[:batch, k * D:(k + 1) * D] for k in range(6)]

    (h1,) = _norm_mod(x2, norm1_g, sc1, sh1, seq, (BF16,))
    proj = _matmul(h1, _to_bf16(w_in_all, li), tm=1024, tn=640)

    ret = _retention(proj, cos_r, sin_r, batch, seq)
    ssm_params = _ssm_params(ssm_a_re, ssm_a_im, ssm_log_dt, ssm_b_re, ssm_b_im, ssm_c_re, ssm_c_im)
    ssm = _glu(_ssm_scan(proj, ssm_params, ssm_d, batch, seq), _to_bf16(ssm_w_glu_all, li), ssm_b_glu)
    swa = _swa(proj, cos_a, sin_a, attn_q_norm, attn_k_norm, attn_sinks, batch, seq)
    pool = _pool(proj, pool_w, pool_scale, batch, seq)
    x2 = _out_proj((ret, ssm, swa, pool), out_norm_g, _to_bf16(w_out_all, li), x2, g1, seq)

    h2b, h2f = _norm_mod(x2, norm2_g, sc2, sh2, seq, (BF16, F32))
    q = _matmul(h2b, _to_bf16(peer_w_query_all, li), tm=1024, tn=512)
    experts, gates = _peer_topk(q, peer_sub_keys.astype(BF16))
    return _peer_gather(experts, gates, h2f, x2, g2, _pack_tables(peer_u_all, peer_v_all, li), seq)


def kernel(x, c, positions, ada_w, ada_b, norm1_g, norm2_g, w_in, w_out, out_norm_g, ssm_a_re, ssm_a_im, ssm_log_dt, ssm_b_re, ssm_b_im, ssm_c_re, ssm_c_im, ssm_d, ssm_w_glu, ssm_b_glu, attn_q_norm, attn_k_norm, attn_sinks, pool_w, pool_scale, peer_w_query, peer_sub_keys, peer_u, peer_v):
    B, S, D = x.shape
    depth = ada_w.shape[0]
    c8 = jnp.concatenate([c, jnp.zeros((V7X_SUBLANES - B, D), c.dtype)], axis=0)
    mod = _ada_mod(c8, ada_w, ada_b)
    cos_r, sin_r = _rope_tables(positions, RET_DK)
    cos_a, sin_a = _rope_tables(positions, SWA_HEAD_DIM)
    x2 = x.reshape(B * S, D)
    for i in range(depth):
        x2 = _layer(i, x2, mod[i], cos_r, sin_r, cos_a, sin_a, B, S, norm1_g[i], norm2_g[i], w_in, w_out,
                    out_norm_g[i], ssm_a_re[i], ssm_a_im[i], ssm_log_dt[i], ssm_b_re[i], ssm_b_im[i],
                    ssm_c_re[i], ssm_c_im[i], ssm_d[i], ssm_w_glu, ssm_b_glu[i],
                    attn_q_norm[i], attn_k_norm[i], attn_sinks[i], pool_w[i], pool_scale[i],
                    peer_w_query, peer_sub_keys[i], peer_u, peer_v)
    return x2.reshape(B, S, D)
```

```python
import functools
import math

import numpy as np
import jax
import jax.numpy as jnp
from jax import lax
from jax.experimental import pallas as pl
from jax.experimental.pallas import tpu as pltpu

F32 = jnp.float32
BF16 = jnp.bfloat16
HIGHEST = lax.Precision.HIGHEST

V7X_LANES = 128
V7X_SUBLANES = 8
V7X_VMEM_BYTES = 64 * 1024 * 1024
VMEM_LIMIT = 52 * 1024 * 1024

D_MODEL = 4096
EPS = 1e-6
ROPE_THETA = 10000.0

RET_HEADS = 4
RET_DV = 256
RET_DK = 128
RET_CHUNK = 128

SSM_WIDTH = 1024
SSM_GROUP = 16
SSM_GROUPS = 64
SSM_STATE = 64
SSM_COLS = 128
SSM_CH = SSM_COLS // SSM_GROUP * SSM_STATE
SSM_CHUNK = 256

SWA_HEAD_DIM = 64
SWA_Q_HEADS = 16
SWA_KV_HEADS = 2
SWA_WINDOW = 128

POOL_WINDOWS = (2, 4, 8, 16)
POOL_GROUP = 256
POOL_MAXW = 16

OFF_QR, OFF_KR, OFF_VR, OFF_GR, OFF_US, OFF_QA, OFF_KA, OFF_VA, OFF_UP = (
    0, 512, 1024, 2048, 3072, 4096, 5120, 5248, 5376)
IN_COLS = 6400

PEER_N_KEYS = 128
PEER_HEADS = 8
PEER_TOPK = 16
PEER_HK = PEER_HEADS * PEER_TOPK
PEER_TB = 8
PEER_GROUPS = PEER_HK // V7X_SUBLANES
PEER_S1_GROUPS = 4
PEER_S1_CHAINS = 4
PEER_S2_GROUPS = 4


def _cparams(sem):
    return pltpu.CompilerParams(dimension_semantics=sem, vmem_limit_bytes=VMEM_LIMIT)


def _gelu(x):
    return 0.5 * x * (1.0 + lax.erf(x * math.sqrt(0.5)))


def _sigmoid(x):
    return 1.0 / (1.0 + jnp.exp(-x))


def _ada_kernel(c_ref, w_ref, b_ref, o_ref):
    c = c_ref[...]
    a = c * _sigmoid(c)
    o_ref[0] = jnp.dot(a, w_ref[0], preferred_element_type=F32, precision=HIGHEST) + b_ref[0]


def _ada_mod(c8, ada_w, ada_b):
    L, D, N = ada_w.shape
    tn = 1024
    return pl.pallas_call(
        _ada_kernel,
        grid=(L, N // tn),
        in_specs=[pl.BlockSpec((8, D), lambda l, j: (0, 0)),
                  pl.BlockSpec((1, D, tn), lambda l, j: (l, 0, j)),
                  pl.BlockSpec((1, 1, tn), lambda l, j: (l, 0, j))],
        out_specs=pl.BlockSpec((1, 8, tn), lambda l, j: (l, 0, j)),
        out_shape=jax.ShapeDtypeStruct((L, 8, N), F32),
        compiler_params=_cparams(("arbitrary", "arbitrary")),
        name="ada_mod",
    )(c8, ada_w, ada_b.reshape(L, 1, N))


def _norm_mod_kernel(x_ref, g_ref, sc_ref, sh_ref, *o_refs):
    x = x_ref[...]
    ms = jnp.mean(x * x, axis=-1, keepdims=True)
    y = x * lax.rsqrt(ms + EPS) * g_ref[...]
    h = y * (1.0 + sc_ref[0]) + sh_ref[0]
    for o in o_refs:
        o[...] = h.astype(o.dtype)


def _norm_mod(x2, gain, sc, sh, seq, out_dtypes):
    T, D = x2.shape
    B = sc.shape[0]
    tm = 256
    per_b = seq // tm
    row = pl.BlockSpec((tm, D), lambda i: (i, 0))
    bspec = pl.BlockSpec((1, 1, D), lambda i: (i // per_b, 0, 0))
    outs = pl.pallas_call(
        _norm_mod_kernel,
        grid=(T // tm,),
        in_specs=[row, pl.BlockSpec((1, D), lambda i: (0, 0)), bspec, bspec],
        out_specs=[row for _ in out_dtypes],
        out_shape=[jax.ShapeDtypeStruct((T, D), dt) for dt in out_dtypes],
        compiler_params=_cparams(("arbitrary",)),
        name="norm_mod",
    )(x2, gain.reshape(1, D), sc.reshape(B, 1, D), sh.reshape(B, 1, D))
    return outs


def _mm_kernel(a_ref, w_ref, o_ref):
    o_ref[...] = jnp.dot(a_ref[...], w_ref[...], preferred_element_type=F32)


def _matmul(a, w, tm=512, tn=512):
    M, K = a.shape
    N = w.shape[1]
    return pl.pallas_call(
        _mm_kernel,
        grid=(M // tm, N // tn),
        in_specs=[pl.BlockSpec((tm, K), lambda i, j: (i, 0)),
                  pl.BlockSpec((K, tn), lambda i, j: (0, j))],
        out_specs=pl.BlockSpec((tm, tn), lambda i, j: (i, j)),
        out_shape=jax.ShapeDtypeStruct((M, N), F32),
        compiler_params=_cparams(("arbitrary", "arbitrary")),
        name="matmul",
    )(a, w)


def _cast_kernel3(x_ref, o_ref):
    o_ref[...] = x_ref[0].astype(o_ref.dtype)


def _to_bf16(w_stack, layer, rows=256):
    _, K, N = w_stack.shape
    rows = min(rows, K)
    return pl.pallas_call(
        _cast_kernel3,
        grid=(K // rows,),
        in_specs=[pl.BlockSpec((1, rows, N), lambda i: (layer, i, 0))],
        out_specs=pl.BlockSpec((rows, N), lambda i: (i, 0)),
        out_shape=jax.ShapeDtypeStruct((K, N), BF16),
        compiler_params=_cparams(("arbitrary",)),
        name="cast_bf16",
    )(w_stack)


def _rope_tables(positions, d):
    half = d // 2
    inv = ROPE_THETA ** (-jnp.arange(half, dtype=F32) * 2.0 / d)
    ang = positions.astype(F32).reshape(-1, 1) * inv
    cos, sin = jnp.cos(ang), jnp.sin(ang)
    reps = V7X_LANES // d
    cosf = jnp.concatenate([cos, cos] * reps, axis=-1)
    sinf = jnp.concatenate([-sin, sin] * reps, axis=-1)
    return cosf, sinf


def _ret_consts():
    L, H = RET_CHUNK, RET_HEADS
    log_g = np.log(1.0 - 2.0 ** (-5.0 - np.arange(H, dtype=np.float64)))
    idx = np.arange(L, dtype=np.float64)
    diff = idx[:, None] - idx[None, :]
    decay = np.where(diff >= 0, np.exp(np.maximum(diff, 0.0)[None] * log_g[:, None, None]), 0.0)
    w_k = np.exp((L - 1 - idx)[None, :] * log_g[:, None])
    w_q = np.exp((idx + 1)[None, :] * log_g[:, None])
    gam = np.exp(L * log_g)
    wk_full = np.broadcast_to(w_k[:, :, None], (H, L, RET_DK))
    wq_full = np.broadcast_to(w_q[:, :, None], (H, L, RET_DK))
    gam_full = np.broadcast_to(gam[:, None, None], (H, 1, RET_DV))
    return (jnp.asarray(decay, F32), jnp.asarray(wq_full, F32), jnp.asarray(wk_full, F32),
            jnp.asarray(gam_full, F32))


def _ret_kernel(q_ref, k_ref, v_ref, g_ref, cos_ref, sin_ref, dec_ref, wq_ref, wk_ref, gam_ref,
                o_ref, r_ref):
    n = pl.program_id(1)

    @pl.when(n == 0)
    def _():
        r_ref[...] = jnp.zeros_like(r_ref)

    cos = cos_ref[...]
    sin = sin_ref[...]
    for h in range(RET_HEADS):
        ks = slice(h * RET_DK, (h + 1) * RET_DK)
        vs = slice(h * RET_DV, (h + 1) * RET_DV)
        q = q_ref[:, ks]
        k = k_ref[:, ks]
        q = q * cos + pltpu.roll(q, RET_DK // 2, 1) * sin
        k = (k * cos + pltpu.roll(k, RET_DK // 2, 1) * sin) * (RET_DK ** -0.5)
        vb = v_ref[:, vs].astype(BF16)
        s = lax.dot_general(q.astype(BF16), k.astype(BF16), (((1,), (1,)), ((), ())),
                            preferred_element_type=F32) * dec_ref[h]
        o = jnp.dot(s.astype(BF16), vb, preferred_element_type=F32)
        r = r_ref[h]
        o = o + jnp.dot((q * wq_ref[h]).astype(BF16), r.astype(BF16), preferred_element_type=F32)
        kv = lax.dot_general((k * wk_ref[h]).astype(BF16), vb, (((0,), (0,)), ((), ())),
                             preferred_element_type=F32)
        r_ref[h] = gam_ref[h] * r + kv
        mu = jnp.mean(o, axis=-1, keepdims=True)
        oc = o - mu
        var = jnp.mean(oc * oc, axis=-1, keepdims=True)
        g = g_ref[:, vs]
        o_ref[:, vs] = g * _sigmoid(g) * (oc * lax.rsqrt(var + EPS))


def _retention(proj, cosf, sinf, batch, seq):
    T = proj.shape[0]
    L = RET_CHUNK
    n = seq // L
    H = RET_HEADS
    dec, wq, wk, gam = _ret_consts()
    kw, vw = H * RET_DK, H * RET_DV
    const3 = lambda b, c: (0, 0, 0)
    return pl.pallas_call(
        _ret_kernel,
        grid=(batch, n),
        in_specs=[
            pl.BlockSpec((L, kw), lambda b, c: (b * n + c, OFF_QR // kw)),
            pl.BlockSpec((L, kw), lambda b, c: (b * n + c, OFF_KR // kw)),
            pl.BlockSpec((L, vw), lambda b, c: (b * n + c, OFF_VR // vw)),
            pl.BlockSpec((L, vw), lambda b, c: (b * n + c, OFF_GR // vw)),
            pl.BlockSpec((L, RET_DK), lambda b, c: (b * n + c, 0)),
            pl.BlockSpec((L, RET_DK), lambda b, c: (b * n + c, 0)),
            pl.BlockSpec((H, L, L), const3),
            pl.BlockSpec((H, L, RET_DK), const3),
            pl.BlockSpec((H, L, RET_DK), const3),
            pl.BlockSpec((H, 1, RET_DV), const3),
        ],
        out_specs=pl.BlockSpec((L, vw), lambda b, c: (b * n + c, 0)),
        out_shape=jax.ShapeDtypeStruct((T, vw), F32),
        scratch_shapes=[pltpu.VMEM((H, RET_DK, RET_DV), F32)],
        compiler_params=_cparams(("arbitrary", "arbitrary")),
        name="retention",
    )(proj, proj, proj, proj, cosf, sinf, dec, wq, wk, gam)


def _ssm_params(a_re, a_im, log_dt, b_re, b_im, c_re, c_im):
    dt = jnp.exp(log_dt)[:, None]
    mag = jnp.exp(a_re * dt)
    ab_re = mag * jnp.cos(a_im * dt)
    ab_im = mag * jnp.sin(a_im * dt)
    den = a_re * a_re + a_im * a_im
    nr = ab_re - 1.0
    f_re = (nr * a_re + ab_im * a_im) / den
    f_im = (ab_im * a_re - nr * a_im) / den
    bb_re = f_re[..., None] * b_re - f_im[..., None] * b_im
    bb_im = f_re[..., None] * b_im + f_im[..., None] * b_re
    nblk = SSM_WIDTH // SSM_COLS
    gpb = SSM_COLS // SSM_GROUP
    eye = jnp.eye(gpb, dtype=F32)

    def blockdiag_in(bb):
        t = bb.reshape(nblk, gpb, SSM_STATE, SSM_GROUP)
        m = jnp.einsum('ngpc,gh->ngchp', t, eye)
        return m.reshape(nblk, SSM_COLS, SSM_CH)

    def blockdiag_out(cc):
        t = cc.reshape(nblk, gpb, SSM_GROUP, SSM_STATE)
        m = jnp.einsum('ngcp,gh->ngphc', t, eye)
        return m.reshape(nblk, SSM_CH, SSM_COLS)

    rounds = int(math.log2(SSM_CHUNK))
    pr, pi = [ab_re], [ab_im]
    for _ in range(rounds - 1):
        r, i = pr[-1], pi[-1]
        pr.append(r * r - i * i)
        pi.append(2.0 * r * i)
    apr = jnp.stack(pr, 0).reshape(rounds, nblk, SSM_CH).transpose(1, 0, 2)
    api = jnp.stack(pi, 0).reshape(rounds, nblk, SSM_CH).transpose(1, 0, 2)
    qr, qi = [ab_re], [ab_im]
    for _ in range(V7X_SUBLANES - 1):
        r, i = qr[-1], qi[-1]
        qr.append(r * ab_re - i * ab_im)
        qi.append(r * ab_im + i * ab_re)
    ppr = jnp.stack(qr, 0).reshape(V7X_SUBLANES, nblk, SSM_CH).transpose(1, 0, 2)
    ppi = jnp.stack(qi, 0).reshape(V7X_SUBLANES, nblk, SSM_CH).transpose(1, 0, 2)
    return (blockdiag_in(bb_re).astype(BF16), blockdiag_in(bb_im).astype(BF16),
            blockdiag_out(c_re).astype(BF16), blockdiag_out(c_im).astype(BF16), apr, api, ppr, ppi)


def _cmul_add(xr, xi, p_r, p_i, sr, si):
    return xr + p_r * sr - p_i * si, xi + p_r * si + p_i * sr


def _ssm_kernel(u_ref, bre_ref, bim_ref, cre_ref, cim_ref, apr_ref, api_ref, ppr_ref, ppi_ref, d_ref, o_ref,
                cr_ref, ci_ref, xr_s, xi_s, hr_s, hi_s):
    n = pl.program_id(2)
    hs = V7X_SUBLANES
    ng = SSM_CHUNK // hs

    @pl.when(n == 0)
    def _():
        cr_ref[...] = jnp.zeros_like(cr_ref)
        ci_ref[...] = jnp.zeros_like(ci_ref)

    u = u_ref[...]
    ub = u.astype(BF16)
    xr = jnp.dot(ub, bre_ref[0], preferred_element_type=F32)
    xi = jnp.dot(ub, bim_ref[0], preferred_element_type=F32)
    sub = jnp.bitwise_and(lax.broadcasted_iota(jnp.int32, xr.shape, 0), hs - 1)
    in_rounds = int(math.log2(hs))
    for k in range(in_rounds):
        s = 1 << k
        keep = sub >= s
        sr = jnp.where(keep, pltpu.roll(xr, s, 0), 0.0)
        si = jnp.where(keep, pltpu.roll(xi, s, 0), 0.0)
        xr, xi = _cmul_add(xr, xi, apr_ref[0, k:k + 1, :], api_ref[0, k:k + 1, :], sr, si)
    nlt = SSM_CH // V7X_LANES
    for q in range(nlt):
        xr_s[q] = xr[:, q * V7X_LANES:(q + 1) * V7X_LANES]
        xi_s[q] = xi[:, q * V7X_LANES:(q + 1) * V7X_LANES]

    er = jnp.concatenate([xr_s[q, pl.ds(hs - 1, ng, stride=hs), :] for q in range(nlt)], axis=1)
    ei = jnp.concatenate([xi_s[q, pl.ds(hs - 1, ng, stride=hs), :] for q in range(nlt)], axis=1)
    c_r = cr_ref[...]
    c_i = ci_ref[...]
    grow = lax.broadcasted_iota(jnp.int32, er.shape, 0)
    first = grow == 0
    a8r = apr_ref[0, in_rounds:in_rounds + 1, :]
    a8i = api_ref[0, in_rounds:in_rounds + 1, :]
    er = er + jnp.where(first, a8r * c_r - a8i * c_i, 0.0)
    ei = ei + jnp.where(first, a8r * c_i + a8i * c_r, 0.0)
    for k in range(int(math.log2(ng))):
        s = 1 << k
        keep = grow >= s
        sr = jnp.where(keep, pltpu.roll(er, s, 0), 0.0)
        si = jnp.where(keep, pltpu.roll(ei, s, 0), 0.0)
        kk = in_rounds + k
        er, ei = _cmul_add(er, ei, apr_ref[0, kk:kk + 1, :], api_ref[0, kk:kk + 1, :], sr, si)
    cr_ref[...] = er[ng - 1:ng, :]
    ci_ref[...] = ei[ng - 1:ng, :]
    hr_s[...] = jnp.where(first, c_r, pltpu.roll(er, 1, 0))
    hi_s[...] = jnp.where(first, c_i, pltpu.roll(ei, 1, 0))
    p_r = ppr_ref[0]
    p_i = ppi_ref[0]
    for g in range(ng):
        rows = slice(g * hs, (g + 1) * hs)
        for q in range(nlt):
            ls = slice(q * V7X_LANES, (q + 1) * V7X_LANES)
            hb_r = jnp.broadcast_to(hr_s[g:g + 1, ls], (hs, V7X_LANES))
            hb_i = jnp.broadcast_to(hi_s[g:g + 1, ls], (hs, V7X_LANES))
            yr, yi = _cmul_add(xr_s[q, rows, :], xi_s[q, rows, :], p_r[:, ls], p_i[:, ls], hb_r, hb_i)
            xr_s[q, rows, :] = yr
            xi_s[q, rows, :] = yi
    hr = jnp.concatenate([xr_s[q] for q in range(nlt)], axis=1)
    hi = jnp.concatenate([xi_s[q] for q in range(nlt)], axis=1)
    y = (jnp.dot(hr.astype(BF16), cre_ref[0], preferred_element_type=F32)
         - jnp.dot(hi.astype(BF16), cim_ref[0], preferred_element_type=F32))
    o_ref[...] = _gelu(y + d_ref[...] * u)


def _ssm_scan(proj, params, d, batch, seq):
    T = proj.shape[0]
    bre, bim, cre, cim, apr, api, ppr, ppi = params
    nblk = SSM_WIDTH // SSM_COLS
    nt = seq // SSM_CHUNK
    rounds = apr.shape[1]
    ngroups = SSM_CHUNK // V7X_SUBLANES
    wspec_in = pl.BlockSpec((1, SSM_COLS, SSM_CH), lambda b, c, n: (c, 0, 0))
    wspec_out = pl.BlockSpec((1, SSM_CH, SSM_COLS), lambda b, c, n: (c, 0, 0))
    pspec = pl.BlockSpec((1, rounds, SSM_CH), lambda b, c, n: (c, 0, 0))
    qspec = pl.BlockSpec((1, V7X_SUBLANES, SSM_CH), lambda b, c, n: (c, 0, 0))
    return pl.pallas_call(
        _ssm_kernel,
        grid=(batch, nblk, nt),
        in_specs=[pl.BlockSpec((SSM_CHUNK, SSM_COLS), lambda b, c, n: (b * nt + n, OFF_US // SSM_COLS + c)),
                  wspec_in, wspec_in, wspec_out, wspec_out, pspec, pspec, qspec, qspec,
                  pl.BlockSpec((1, SSM_COLS), lambda b, c, n: (0, c))],
        out_specs=pl.BlockSpec((SSM_CHUNK, SSM_COLS), lambda b, c, n: (b * nt + n, c)),
        out_shape=jax.ShapeDtypeStruct((T, SSM_WIDTH), F32),
        scratch_shapes=[pltpu.VMEM((1, SSM_CH), F32), pltpu.VMEM((1, SSM_CH), F32),
                        pltpu.VMEM((SSM_CH // V7X_LANES, SSM_CHUNK, V7X_LANES), F32),
                        pltpu.VMEM((SSM_CH // V7X_LANES, SSM_CHUNK, V7X_LANES), F32),
                        pltpu.VMEM((ngroups, SSM_CH), F32), pltpu.VMEM((ngroups, SSM_CH), F32)],
        compiler_params=_cparams(("arbitrary", "arbitrary", "arbitrary")),
        name="ssm_scan",
    )(proj, bre, bim, cre, cim, apr, api, ppr, ppi, d.reshape(1, SSM_WIDTH))


def _glu_kernel(y_ref, w_ref, b_ref, o_ref):
    y = y_ref[...]
    z = jnp.dot(y.astype(BF16), w_ref[...], preferred_element_type=F32) + b_ref[...]
    o_ref[...] = y * _sigmoid(z)


def _glu(y, w, b):
    T, W = y.shape
    tm = 512
    return pl.pallas_call(
        _glu_kernel,
        grid=(T // tm,),
        in_specs=[pl.BlockSpec((tm, W), lambda i: (i, 0)),
                  pl.BlockSpec((W, W), lambda i: (0, 0)),
                  pl.BlockSpec((1, W), lambda i: (0, 0))],
        out_specs=pl.BlockSpec((tm, W), lambda i: (i, 0)),
        out_shape=jax.ShapeDtypeStruct((T, W), F32),
        compiler_params=_cparams(("arbitrary",)),
        name="ssm_glu",
    )(y, w, b.reshape(1, W))


def _swa_kernel(sink_ref, q_ref, k_ref, v_ref, cos_ref, sin_ref, qn_ref, kn_ref, o_ref, pk_ref, pv_ref):
    n = pl.program_id(1)
    W = SWA_WINDOW
    hd = SWA_HEAD_DIM

    @pl.when(n == 0)
    def _():
        pk_ref[...] = jnp.zeros_like(pk_ref)
        pv_ref[...] = jnp.zeros_like(pv_ref)

    cos = cos_ref[...]
    sin = sin_ref[...]
    lane = lax.broadcasted_iota(jnp.int32, (W, V7X_LANES), 1)
    lo = lane < hd
    first_half = (lane % hd) < (hd // 2)

    def head_norm(t, gain):
        tt = t * t
        s_all = jnp.sum(tt, axis=-1, keepdims=True)
        s_lo = jnp.sum(jnp.where(lo, tt, 0.0), axis=-1, keepdims=True)
        ms = jnp.where(lo, s_lo, s_all - s_lo) * (1.0 / hd)
        return t * lax.rsqrt(ms + EPS) * gain

    def rope(t):
        sw = jnp.where(first_half, pltpu.roll(t, V7X_LANES - hd // 2, 1), pltpu.roll(t, hd // 2, 1))
        return t * cos + sw * sin

    kc = rope(head_norm(k_ref[...], kn_ref[...]))
    vc = v_ref[...]
    keys = jnp.concatenate([pk_ref[...], kc], axis=0)
    vals = jnp.concatenate([pv_ref[...], vc], axis=0)
    lane2 = lax.broadcasted_iota(jnp.int32, (2 * W, V7X_LANES), 1)
    lo2 = lane2 < hd

    def dup(x, h):
        sw = pltpu.roll(x, hd, 1)
        return (jnp.where(lo2, x, sw) if h == 0 else jnp.where(lo2, sw, x)).astype(BF16)

    kk = [dup(keys, h) for h in range(SWA_KV_HEADS)]
    vv = [dup(vals, h) for h in range(SWA_KV_HEADS)]
    qi = lax.broadcasted_iota(jnp.int32, (W, 2 * W), 0)
    ci = lax.broadcasted_iota(jnp.int32, (W, 2 * W), 1)
    prev_floor = qi + jnp.where(n > 0, 0, 2 * W)
    valid = jnp.logical_or(jnp.logical_and(ci < W, ci > prev_floor),
                           jnp.logical_and(ci >= W, (ci - W) <= qi))
    tiles = SWA_Q_HEADS * hd // V7X_LANES
    per_kv = tiles // SWA_KV_HEADS
    for m in range(tiles):
        sl = slice(m * V7X_LANES, (m + 1) * V7X_LANES)
        qt = rope(head_norm(q_ref[:, sl], qn_ref[:, sl]))
        h = m // per_kv
        outs = []
        for half in range(2):
            sel = lo if half == 0 else jnp.logical_not(lo)
            qm = jnp.where(sel, qt, 0.0).astype(BF16)
            s = lax.dot_general(qm, kk[h], (((1,), (1,)), ((), ())),
                                preferred_element_type=F32) * (hd ** -0.5)
            s = jnp.where(valid, s, -1e30)
            sink = sink_ref[2 * m + half]
            mx = jnp.maximum(jnp.max(s, axis=-1, keepdims=True), sink)
            p = jnp.exp(s - mx)
            den = jnp.sum(p, axis=-1, keepdims=True) + jnp.exp(sink - mx)
            p = p / den
            outs.append(jnp.dot(p.astype(BF16), vv[h], preferred_element_type=F32))
        o_ref[:, sl] = jnp.where(lo, outs[0], outs[1])
    pk_ref[...] = kc
    pv_ref[...] = vc


def _swa(proj, cosf, sinf, q_norm, k_norm, sinks, batch, seq):
    T = proj.shape[0]
    W = SWA_WINDOW
    nb = seq // W
    qw = SWA_Q_HEADS * SWA_HEAD_DIM
    qn = jnp.tile(q_norm, SWA_Q_HEADS).reshape(1, qw)
    kn = jnp.tile(k_norm, SWA_KV_HEADS).reshape(1, V7X_LANES)
    return pl.pallas_call(
        _swa_kernel,
        grid=(batch, nb),
        in_specs=[pl.BlockSpec(memory_space=pltpu.SMEM),
                  pl.BlockSpec((W, qw), lambda b, n: (b * nb + n, OFF_QA // qw)),
                  pl.BlockSpec((W, V7X_LANES), lambda b, n: (b * nb + n, OFF_KA // V7X_LANES)),
                  pl.BlockSpec((W, V7X_LANES), lambda b, n: (b * nb + n, OFF_VA // V7X_LANES)),
                  pl.BlockSpec((W, V7X_LANES), lambda b, n: (b * nb + n, 0)),
                  pl.BlockSpec((W, V7X_LANES), lambda b, n: (b * nb + n, 0)),
                  pl.BlockSpec((1, qw), lambda b, n: (0, 0)),
                  pl.BlockSpec((1, V7X_LANES), lambda b, n: (0, 0))],
        out_specs=pl.BlockSpec((W, qw), lambda b, n: (b * nb + n, 0)),
        out_shape=jax.ShapeDtypeStruct((T, qw), F32),
        scratch_shapes=[pltpu.VMEM((W, V7X_LANES), F32), pltpu.VMEM((W, V7X_LANES), F32)],
        compiler_params=_cparams(("arbitrary", "arbitrary")),
        name="swa",
    )(sinks, proj, proj, proj, cosf, sinf, qn, kn)


POOL_CHUNK = 256


def _pool_kernel(u0_ref, u1_ref, u2_ref, u3_ref, w_ref, sc_ref, o_ref, ext_ref):
    n = pl.program_id(1)
    Lc = POOL_CHUNK

    @pl.when(n == 0)
    def _():
        ext_ref[:, 0:POOL_MAXW, :] = jnp.zeros((len(POOL_WINDOWS), POOL_MAXW, POOL_GROUP), F32)

    t = n * Lc + lax.broadcasted_iota(jnp.int32, (Lc, POOL_GROUP), 0)
    for g, (u_ref, w) in enumerate(zip((u0_ref, u1_ref, u2_ref, u3_ref), POOL_WINDOWS)):
        u = u_ref[...]
        ext_ref[g, POOL_MAXW:POOL_MAXW + Lc, :] = u
        acc = u
        for k in range(1, w):
            acc = acc + ext_ref[g, POOL_MAXW - k:POOL_MAXW - k + Lc, :]
        cnt = jnp.minimum(t + 1, w).astype(F32)
        pooled = acc / cnt - u
        y = jnp.dot(pooled.astype(BF16), w_ref[g], preferred_element_type=F32)
        sl = slice(g * POOL_GROUP, (g + 1) * POOL_GROUP)
        o_ref[:, sl] = y * sc_ref[:, sl]
        ext_ref[g, 0:POOL_MAXW, :] = u[Lc - POOL_MAXW:, :]


def _pool(proj, pool_w, pool_scale, batch, seq):
    T = proj.shape[0]
    Lc = POOL_CHUNK
    nt = seq // Lc
    ng = len(POOL_WINDOWS)
    width = ng * POOL_GROUP
    uspecs = [pl.BlockSpec((Lc, POOL_GROUP), functools.partial(
        lambda b, n, g: (b * nt + n, OFF_UP // POOL_GROUP + g), g=g)) for g in range(ng)]
    return pl.pallas_call(
        _pool_kernel,
        grid=(batch, nt),
        in_specs=uspecs + [pl.BlockSpec((ng, POOL_GROUP, POOL_GROUP), lambda b, n: (0, 0, 0)),
                           pl.BlockSpec((1, width), lambda b, n: (0, 0))],
        out_specs=pl.BlockSpec((Lc, width), lambda b, n: (b * nt + n, 0)),
        out_shape=jax.ShapeDtypeStruct((T, width), F32),
        scratch_shapes=[pltpu.VMEM((ng, POOL_MAXW + Lc, POOL_GROUP), F32)],
        compiler_params=_cparams(("arbitrary", "arbitrary")),
        name="pool",
    )(proj, proj, proj, proj, pool_w.astype(BF16), pool_scale.reshape(1, width))


def _out_kernel(r_ref, s_ref, a_ref, p_ref, gain_ref, w_ref, x_ref, g1_ref, o_ref, mix_ref):
    j = pl.program_id(1)

    @pl.when(j == 0)
    def _():
        for idx, ref in enumerate((r_ref, s_ref, a_ref, p_ref)):
            v = ref[...]
            width = v.shape[1]
            sl = slice(idx * width, (idx + 1) * width)
            ms = jnp.mean(v * v, axis=-1, keepdims=True)
            mix_ref[:, sl] = (v * lax.rsqrt(ms + EPS) * gain_ref[:, sl]).astype(BF16)

    acc = jnp.dot(mix_ref[...], w_ref[...], preferred_element_type=F32)
    o_ref[...] = x_ref[...] + g1_ref[0] * acc


def _out_proj(branches, gain, w_out_bf, x2, g1, seq):
    T, D = x2.shape
    B = g1.shape[0]
    tm, tn = 512, 512
    per_b = seq // tm
    width = branches[0].shape[1]
    bspec = pl.BlockSpec((tm, width), lambda i, j: (i, 0))
    return pl.pallas_call(
        _out_kernel,
        grid=(T // tm, D // tn),
        in_specs=[bspec, bspec, bspec, bspec,
                  pl.BlockSpec((1, D), lambda i, j: (0, 0)),
                  pl.BlockSpec((D, tn), lambda i, j: (0, j)),
                  pl.BlockSpec((tm, tn), lambda i, j: (i, j)),
                  pl.BlockSpec((1, 1, tn), lambda i, j: (i // per_b, 0, j))],
        out_specs=pl.BlockSpec((tm, tn), lambda i, j: (i, j)),
        out_shape=jax.ShapeDtypeStruct((T, D), F32),
        scratch_shapes=[pltpu.VMEM((tm, D), BF16)],
        compiler_params=_cparams(("arbitrary", "arbitrary")),
        name="out_proj",
    )(*branches, gain.reshape(1, D), w_out_bf, x2, g1.reshape(B, 1, D))


PEER_TOPK_TB = 128


def _iter_topk(s, k):
    n = s.shape[0]
    rows = lax.broadcasted_iota(jnp.int32, s.shape, 0).astype(F32)
    vals, idxs = [], []
    for _ in range(k):
        m = jnp.max(s, axis=0, keepdims=True)
        idx = jnp.min(jnp.where(s == m, rows, float(n)), axis=0, keepdims=True)
        s = jnp.where(rows == idx, -jnp.inf, s)
        vals.append(m)
        idxs.append(idx)
    return jnp.concatenate(vals, axis=0), jnp.concatenate(idxs, axis=0)


def _topk_kernel(q_ref, keys_ref, e_ref, g_ref):
    K = PEER_TOPK
    half = PEER_N_KEYS
    e_all, g_all = [], []
    for h in range(PEER_HEADS):
        tops = []
        for p in range(2):
            c0 = h * 2 * half + p * half
            qhp = q_ref[:, c0:c0 + half].astype(BF16)
            s = lax.dot_general(keys_ref[p], qhp, (((1,), (1,)), ((), ())), preferred_element_type=F32)
            tops.append(_iter_topk(s, K))
        (s0, i0), (s1, i1) = tops
        hs = V7X_SUBLANES
        io8 = lax.broadcasted_iota(jnp.int32, (hs, s0.shape[1]), 0).astype(F32)
        ps, pe, pp = [], [], []
        for a, b0 in [(0, 0), (0, hs)] + [(a, 0) for a in range(1, hs)]:
            ps.append(s0[a:a + 1, :] + s1[b0:b0 + hs, :])
            pe.append(i0[a:a + 1, :] * PEER_N_KEYS + i1[b0:b0 + hs, :])
            pp.append(io8 + float(a * K + b0))
        ps.append(s0[hs:K, :] + s1[0:1, :])
        pe.append(i0[hs:K, :] * PEER_N_KEYS + i1[0:1, :])
        pp.append((io8 + float(hs)) * float(K))
        cand_s = jnp.concatenate(ps, axis=0)
        cand_e = jnp.concatenate(pe, axis=0)
        rows = jnp.concatenate(pp, axis=0)
        bs, be = [], []
        for _ in range(K):
            m = jnp.max(cand_s, axis=0, keepdims=True)
            pos = jnp.min(jnp.where(cand_s == m, rows, float(K * K)), axis=0, keepdims=True)
            hit = rows == pos
            be.append(jnp.sum(jnp.where(hit, cand_e, 0.0), axis=0, keepdims=True))
            cand_s = jnp.where(hit, -jnp.inf, cand_s)
            bs.append(m)
        best = jnp.concatenate(bs, axis=0)
        pexp = jnp.exp(best - best[0:1, :])
        g_all.append(pexp / jnp.sum(pexp, axis=0, keepdims=True))
        e_all.append(jnp.concatenate(be, axis=0))
    e_ref[...] = jnp.concatenate(e_all, axis=0).T.astype(jnp.int32)
    g_ref[...] = jnp.concatenate(g_all, axis=0).T


def _peer_topk(q, sub_keys_bf):
    T, QW = q.shape
    tb = PEER_TOPK_TB
    return pl.pallas_call(
        _topk_kernel,
        grid=(T // tb,),
        in_specs=[pl.BlockSpec((tb, QW), lambda i: (i, 0)),
                  pl.BlockSpec((2, PEER_N_KEYS, PEER_N_KEYS), lambda i: (0, 0, 0))],
        out_specs=[pl.BlockSpec((tb, PEER_HK), lambda i: (i, 0)),
                   pl.BlockSpec((tb, PEER_HK), lambda i: (i, 0))],
        out_shape=[jax.ShapeDtypeStruct((T, PEER_HK), jnp.int32),
                   jax.ShapeDtypeStruct((T, PEER_HK), F32)],
        compiler_params=_cparams(("arbitrary",)),
        name="peer_topk",
    )(q, sub_keys_bf)


PEER_PACK_ROWS = 64


def _pack_kernel(u_ref, v_ref, o_ref):
    ub = lax.bitcast_convert_type(u_ref[0].astype(BF16).astype(F32), jnp.int32)
    vb = lax.bitcast_convert_type(v_ref[0].astype(BF16).astype(F32), jnp.int32)
    word = jnp.bitwise_or(lax.shift_right_logical(ub, 16), vb)
    for r in range(PEER_PACK_ROWS):
        o_ref[r] = word[r:r + 1, :]


def _pack_tables(u_stack, v_stack, layer):
    _, E, D = u_stack.shape
    R = PEER_PACK_ROWS
    spec = pl.BlockSpec((1, R, D), lambda i: (layer, i, 0))
    return pl.pallas_call(
        _pack_kernel,
        grid=(E // R,),
        in_specs=[spec, spec],
        out_specs=pl.BlockSpec((R, 1, D), lambda i: (i, 0, 0)),
        out_shape=jax.ShapeDtypeStruct((E, 1, D), jnp.int32),
        compiler_params=_cparams(("arbitrary",)),
        name="peer_pack",
    )(u_stack, v_stack)


def _peer_kernel(ec_ref, en_ref, h_ref, gate_ref, x_ref, g2_ref, tab_hbm, o_ref,
                 buf0, buf1, acc_ref, cmat_ref, sem0, sem1):
    i = pl.program_id(0)
    _peer_half(i, 0, buf0, sem0, buf1, sem1, ec_ref, PEER_TB, ec_ref, h_ref, gate_ref,
               tab_hbm, o_ref, acc_ref, cmat_ref)
    _peer_half(i, PEER_TB, buf1, sem1, buf0, sem0, en_ref, 0, ec_ref, h_ref, gate_ref,
               tab_hbm, o_ref, acc_ref, cmat_ref)
    o_ref[...] = x_ref[...] + g2_ref[0] * o_ref[...]


def _peer_half(i, row0, buf, csem, nbuf, nsem, nidx_ref, nrow0, ec_ref, h_ref, gate_ref, tab_hbm, o_ref,
               acc_ref, cmat_ref):
    D = h_ref.shape[1]
    nch = D // V7X_LANES
    half_ch = nch // 2
    G = PEER_GROUPS
    cur = 0
    nxt = 0
    zero = jnp.zeros((V7X_SUBLANES, V7X_LANES), F32)

    def start_group(idx_ref, r, g, slot, dst=nbuf, dsem=nsem):
        for s in range(V7X_SUBLANES):
            e = idx_ref[r * PEER_HK + g * V7X_SUBLANES + s]
            pltpu.make_async_copy(tab_hbm.at[e], dst.at[slot, g, pl.ds(s, 1), :],
                                  dsem.at[slot]).start(priority=s % 2)

    def wait_slot(slot, b=buf, bsem=csem):
        pltpu.make_async_copy(b.at[slot], b.at[slot], bsem.at[slot]).wait()

    if row0 == 0:
        @pl.when(i == 0)
        def _():
            for tt in range(PEER_TB):
                def body(g, carry, tt=tt):
                    start_group(ec_ref, tt, g, tt, dst=buf, dsem=csem)
                    return carry
                lax.fori_loop(0, G, body, 0)

    for tt in range(PEER_TB):
        wait_slot(cur + tt)

        def s1_body(j, carry, tt=tt):
            parts = [[None] * PEER_S1_CHAINS for _ in range(PEER_S1_GROUPS)]
            for c in range(nch):
                cs = slice(c * V7X_LANES, (c + 1) * V7X_LANES)
                hb = jnp.broadcast_to(h_ref[row0 + tt:row0 + tt + 1, cs], (V7X_SUBLANES, V7X_LANES))
                for q in range(PEER_S1_GROUPS):
                    w = buf[cur + tt, j * PEER_S1_GROUPS + q, :, cs]
                    t = lax.bitcast_convert_type(jnp.left_shift(w, 16), F32) * hb
                    p = parts[q][c % PEER_S1_CHAINS]
                    parts[q][c % PEER_S1_CHAINS] = t if p is None else p + t
            for q in range(PEER_S1_GROUPS):
                r0 = pl.multiple_of((tt * G + j * PEER_S1_GROUPS + q) * V7X_SUBLANES, V7X_SUBLANES)
                acc_ref[pl.ds(r0, V7X_SUBLANES), :] = (parts[q][0] + parts[q][1]) + (parts[q][2] + parts[q][3])
            for d in range(PEER_S1_GROUPS // 2):
                start_group(nidx_ref, nrow0 + tt, j * (PEER_S1_GROUPS // 2) + d, nxt + tt)
            return carry

        lax.fori_loop(0, G // PEER_S1_GROUPS, s1_body, 0)

    ones_row = jnp.ones((V7X_SUBLANES, V7X_LANES), BF16)
    ones_sq = jnp.ones((V7X_LANES, V7X_LANES), BF16)
    nt_dims = (((1,), (1,)), ((), ()))
    acc = acc_ref[...]
    acc_hi = acc.astype(BF16)
    acc_lo = (acc - acc_hi.astype(F32)).astype(BF16)
    a8 = (lax.dot_general(ones_row, acc_hi, nt_dims, preferred_element_type=F32)
          + lax.dot_general(ones_row, acc_lo, nt_dims, preferred_element_type=F32))
    a_tok = jnp.concatenate([a8[0:1, tt * PEER_HK:(tt + 1) * PEER_HK] for tt in range(PEER_TB)], axis=0)
    c_tok = gate_ref[row0:row0 + PEER_TB, :] * _gelu(a_tok)
    r_i = lax.broadcasted_iota(jnp.int32, (PEER_HK, V7X_LANES), 0)
    c_i = lax.broadcasted_iota(jnp.int32, (PEER_HK, V7X_LANES), 1)
    eye = r_i == c_i
    for tt in range(PEER_TB):
        c_diag = jnp.where(eye, jnp.broadcast_to(c_tok[tt:tt + 1, :], (PEER_HK, V7X_LANES)), 0.0)
        d_hi = c_diag.astype(BF16)
        d_lo = (c_diag - d_hi.astype(F32)).astype(BF16)
        cmat_ref[tt * PEER_HK:(tt + 1) * PEER_HK, :] = (
            jnp.dot(d_hi, ones_sq, preferred_element_type=F32) + jnp.dot(d_lo, ones_sq, preferred_element_type=F32))

    for tt in range(PEER_TB):
        for half in range(2):
            def s2_body(j, accs, tt=tt, half=half):
                new = list(accs)
                for q in range(PEER_S2_GROUPS):
                    g = j * PEER_S2_GROUPS + q
                    r0 = pl.multiple_of((tt * G + g) * V7X_SUBLANES, V7X_SUBLANES)
                    cm = cmat_ref[pl.ds(r0, V7X_SUBLANES), :]
                    for cc in range(half_ch):
                        c = half * half_ch + cc
                        w = buf[cur + tt, g, :, c * V7X_LANES:(c + 1) * V7X_LANES]
                        v = lax.bitcast_convert_type(jnp.bitwise_and(w, jnp.int32(-65536)), F32)
                        new[cc] = new[cc] + cm * v
                start_group(nidx_ref, nrow0 + tt, G // 2 + half * (G // 4) + j, nxt + tt)
                return tuple(new)

            accs = lax.fori_loop(0, G // PEER_S2_GROUPS, s2_body, tuple(zero for _ in range(half_ch)))
            for cc in range(half_ch):
                c = half * half_ch + cc
                o_ref[row0 + tt:row0 + tt + 1, c * V7X_LANES:(c + 1) * V7X_LANES] = jnp.sum(
                    accs[cc], axis=0, keepdims=True)

    if row0 != 0:
        @pl.when(i == pl.num_programs(0) - 1)
        def _():
            for tt in range(PEER_TB):
                wait_slot(tt, b=nbuf, bsem=nsem)


def _peer_gather(experts, gates, h32, x2, g2, uv_tab, seq):
    T, D = x2.shape
    B = g2.shape[0]
    tb = 2 * PEER_TB
    per_b = seq // tb
    nsteps = T // tb
    G = PEER_GROUPS
    assert PEER_S1_GROUPS % 2 == 0 and G % PEER_S1_GROUPS == 0 and G // PEER_S2_GROUPS == G // 4
    assert PEER_S1_CHAINS == 4
    row = pl.BlockSpec((tb, D), lambda i: (i, 0))
    eflat = experts.reshape(T * PEER_HK)
    return pl.pallas_call(
        _peer_kernel,
        grid=(nsteps,),
        in_specs=[pl.BlockSpec((tb * PEER_HK,), lambda i: (i,), memory_space=pltpu.SMEM),
                  pl.BlockSpec((tb * PEER_HK,), lambda i: (jnp.minimum(i + 1, nsteps - 1),),
                               memory_space=pltpu.SMEM),
                  row,
                  pl.BlockSpec((tb, PEER_HK), lambda i: (i, 0)),
                  row,
                  pl.BlockSpec((1, 1, D), lambda i: (i // per_b, 0, 0)),
                  pl.BlockSpec(memory_space=pl.ANY)],
        out_specs=row,
        out_shape=jax.ShapeDtypeStruct((T, D), F32),
        scratch_shapes=[pltpu.VMEM((PEER_TB, G, V7X_SUBLANES, D), jnp.int32),
                        pltpu.VMEM((PEER_TB, G, V7X_SUBLANES, D), jnp.int32),
                        pltpu.VMEM((PEER_TB * PEER_HK, V7X_LANES), F32),
                        pltpu.VMEM((PEER_TB * PEER_HK, V7X_LANES), F32),
                        pltpu.SemaphoreType.DMA((PEER_TB,)),
                        pltpu.SemaphoreType.DMA((PEER_TB,))],
        compiler_params=_cparams(("arbitrary",)),
        name="peer_gather",
    )(eflat, eflat, h32, gates, x2, g2.reshape(B, 1, D), uv_tab)


def _layer(li, x2, mod, cos_r, sin_r, cos_a, sin_a, batch, seq, norm1_g, norm2_g, w_in_all, w_out_all, out_norm_g,
           ssm_a_re, ssm_a_im, ssm_log_dt, ssm_b_re, ssm_b_im, ssm_c_re, ssm_c_im,
           ssm_d, ssm_w_glu_all, ssm_b_glu, attn_q_norm, attn_k_norm, attn_sinks,
           pool_w, pool_scale, peer_w_query_all, peer_sub_keys, peer_u_all, peer_v_all):
    D = D_MODEL
    sh1, sc1, g1, sh2, sc2, g2 = [mod[:batch, k * D:(k + 1) * D] for k in range(6)]

    (h1,) = _norm_mod(x2, norm1_g, sc1, sh1, seq, (BF16,))
    proj = _matmul(h1, _to_bf16(w_in_all, li), tm=1024, tn=640)

    ret = _retention(proj, cos_r, sin_r, batch, seq)
    ssm_params = _ssm_params(ssm_a_re, ssm_a_im, ssm_log_dt, ssm_b_re, ssm_b_im, ssm_c_re, ssm_c_im)
    ssm = _glu(_ssm_scan(proj, ssm_params, ssm_d, batch, seq), _to_bf16(ssm_w_glu_all, li), ssm_b_glu)
    swa = _swa(proj, cos_a, sin_a, attn_q_norm, attn_k_norm, attn_sinks, batch, seq)
    pool = _pool(proj, pool_w, pool_scale, batch, seq)
    x2 = _out_proj((ret, ssm, swa, pool), out_norm_g, _to_bf16(w_out_all, li), x2, g1, seq)

    h2b, h2f = _norm_mod(x2, norm2_g, sc2, sh2, seq, (BF16, F32))
    q = _matmul(h2b, _to_bf16(peer_w_query_all, li), tm=1024, tn=512)
    experts, gates = _peer_topk(q, peer_sub_keys.astype(BF16))
    return _peer_gather(experts, gates, h2f, x2, g2, _pack_tables(peer_u_all, peer_v_all, li), seq)


def kernel(x, c, positions, ada_w, ada_b, norm1_g, norm2_g, w_in, w_out, out_norm_g, ssm_a_re, ssm_a_im, ssm_log_dt, ssm_b_re, ssm_b_im, ssm_c_re, ssm_c_im, ssm_d, ssm_w_glu, ssm_b_glu, attn_q_norm, attn_k_norm, attn_sinks, pool_w, pool_scale, peer_w_query, peer_sub_keys, peer_u, peer_v):
    B, S, D = x.shape
    depth = ada_w.shape[0]
    c8 = jnp.concatenate([c, jnp.zeros((V7X_SUBLANES - B, D), c.dtype)], axis=0)
    mod = _ada_mod(c8, ada_w, ada_b)
    cos_r, sin_r = _rope_tables(positions, RET_DK)
    cos_a, sin_a = _rope_tables(positions, SWA_HEAD_DIM)
    x2 = x.reshape(B * S, D)
    for i in range(depth):
        x2 = _layer(i, x2, mod[i], cos_r, sin_r, cos_a, sin_a, B, S, norm1_g[i], norm2_g[i], w_in, w_out,
                    out_norm_g[i], ssm_a_re[i], ssm_a_im[i], ssm_log_dt[i], ssm_b_re[i], ssm_b_im[i],
                    ssm_c_re[i], ssm_c_im[i], ssm_d[i], ssm_w_glu, ssm_b_glu[i],
                    attn_q_norm[i], attn_k_norm[i], attn_sinks[i], pool_w[i], pool_scale[i],
                    peer_w_query, peer_sub_keys[i], peer_u, peer_v)
    return x2.reshape(B, S, D)
```
